```python
import math
import jax
import jax.numpy as jnp
from jax import lax
import numpy as np

D_MODEL = 1024
BATCH = 16
SEQ = 256
DEPTH = 4
DEC_BATCH = 4
DEC_SEQ = 2048
PAST_LEN = 256

GRID_W = 64
ROPE_BASE = 10000.0
EPS = 1e-6
Q_BLOCK = 128
CHUNK = 128
D_FF = 4 * D_MODEL
N_MOD = 6

H_A = 4
NOPE_A = 64
ROPE_A = 32
V_A = 64
Q_RANK = 256
KV_RANK = 128
H_B = 4
DH_B = 32
H_C = 8
P_C = 64
N_C = 64
G_C = 2
CONV_K = 5
D_INNER = H_C * P_C
CONV_DIM = D_INNER + 2 * G_C * N_C

SPLIT_SIZES = (Q_RANK, KV_RANK, ROPE_A, H_B * 2 * DH_B, H_B * 2 * DH_B, H_B * 2 * DH_B, D_INNER, CONV_DIM, 2 * H_C)
D_IN_PROJ = sum(SPLIT_SIZES)
MIX_WIDTH = H_A * V_A + H_B * 2 * DH_B + D_INNER

kernel_name = 'hybrid_mla_diff_ssd_flow_step'


def rmsnorm(x, g):
    xf = x.astype(jnp.float32)
    y = xf * lax.rsqrt(jnp.mean(xf * xf, axis=-1, keepdims=True) + EPS)
    return (y * g.astype(jnp.float32)).astype(x.dtype)


def grid_angles(n_tok, rot_dim):
    n_rows = n_tok // GRID_W
    rows = jnp.repeat(jnp.arange(n_rows, dtype=jnp.float32), GRID_W)
    cols = jnp.tile(jnp.arange(GRID_W, dtype=jnp.float32), n_rows)
    half = rot_dim // 2
    freqs = ROPE_BASE ** (-jnp.arange(0, half, 2, dtype=jnp.float32) / half)
    return rows[:, None] * freqs, cols[:, None] * freqs


def axial_rope(x, ang_r, ang_c):
    half = x.shape[-1] // 2
    extra = (1,) * (x.ndim - 3)

    def rot(xa, ang):
        m = xa.shape[-1] // 2
        cos = jnp.cos(ang).reshape(ang.shape[0], *extra, m)
        sin = jnp.sin(ang).reshape(ang.shape[0], *extra, m)
        x1 = xa[..., :m].astype(jnp.float32)
        x2 = xa[..., m:].astype(jnp.float32)
        return jnp.concatenate([x1 * cos - x2 * sin, x2 * cos + x1 * sin], axis=-1)

    out = jnp.concatenate([rot(x[..., :half], ang_r), rot(x[..., half:], ang_c)], axis=-1)
    return out.astype(x.dtype)


def to_query_blocks(q):
    b, n = q.shape[:2]
    return jnp.moveaxis(q.reshape(b, n // Q_BLOCK, Q_BLOCK, *q.shape[2:]), 1, 0)


def from_query_blocks(o):
    nb, b = o.shape[:2]
    return jnp.moveaxis(o, 0, 1).reshape(b, nb * Q_BLOCK, *o.shape[3:])


def softmax_attention(q, k, v):
    scale = q.shape[-1] ** -0.5
    kf = k.astype(jnp.float32)
    vf = v.astype(jnp.float32)

    def one_block(qb):
        s = jnp.einsum('bqhd,bkhd->bhqk', qb.astype(jnp.float32), kf) * scale
        p = jax.nn.softmax(s, axis=-1)
        return jnp.einsum('bhqk,bkhd->bqhd', p, vf)

    return from_query_blocks(lax.map(one_block, to_query_blocks(q))).astype(v.dtype)


def differential_attention(q, k, v, lam):
    scale = q.shape[-1] ** -0.5
    kf = k.astype(jnp.float32)
    vf = v.astype(jnp.float32)

    def one_block(qb):
        s = jnp.einsum('bqhmd,bkhmd->bmhqk', qb.astype(jnp.float32), kf) * scale
        p = jax.nn.softmax(s, axis=-1)
        w = p[:, 0] - lam * p[:, 1]
        return jnp.einsum('bhqk,bkhd->bqhd', w, vf)

    return from_query_blocks(lax.map(one_block, to_query_blocks(q))).astype(v.dtype)


def diff_lambda(p, layer):
    lam_init = 0.8 - 0.6 * math.exp(-0.3 * layer)
    lq1 = p['diff_lq1'].astype(jnp.float32)
    lk1 = p['diff_lk1'].astype(jnp.float32)
    lq2 = p['diff_lq2'].astype(jnp.float32)
    lk2 = p['diff_lk2'].astype(jnp.float32)
    lam = jnp.exp(jnp.sum(lq1 * lk1)) - jnp.exp(jnp.sum(lq2 * lk2)) + lam_init
    return lam, lam_init


def ssd_scan(x, dt, a_neg, b_in, c_in, h0):
    bsz, n = x.shape[:2]
    nc = n // CHUNK
    rep = H_C // G_C
    xc = x.astype(jnp.float32).reshape(bsz, nc, CHUNK, H_C, P_C)
    bc = jnp.repeat(b_in.astype(jnp.float32), rep, axis=2).reshape(bsz, nc, CHUNK, H_C, N_C)
    cc = jnp.repeat(c_in.astype(jnp.float32), rep, axis=2).reshape(bsz, nc, CHUNK, H_C, N_C)
    dtc = dt.reshape(bsz, nc, CHUNK, H_C)
    a_cum = jnp.cumsum(dtc * a_neg, axis=2)
    seg = a_cum[:, :, :, None, :] - a_cum[:, :, None, :, :]
    lower = jnp.tril(jnp.ones((CHUNK, CHUNK), dtype=bool))[:, :, None]
    decay_in = jnp.exp(jnp.where(lower, seg, -jnp.inf))
    scores = jnp.einsum('bcihn,bcjhn->bcijh', cc, bc) * decay_in
    y_diag = jnp.einsum('bcijh,bcjh,bcjhp->bcihp', scores, dtc, xc)
    decay_end = jnp.exp(a_cum[:, :, -1:, :] - a_cum)
    states = jnp.einsum('bcjhn,bcjh,bcjhp->bchpn', bc, decay_end * dtc, xc)
    chunk_decay = jnp.exp(a_cum[:, :, -1, :])

    def step(h, inp):
        dec, st = inp
        return dec[:, :, None, None] * h + st, h

    h_final, h_prev = lax.scan(step, h0.astype(jnp.float32),
                               (jnp.moveaxis(chunk_decay, 1, 0), jnp.moveaxis(states, 1, 0)))
    h_prev = jnp.moveaxis(h_prev, 0, 1)
    y_off = jnp.einsum('bcihn,bchpn->bcihp', cc, h_prev) * jnp.exp(a_cum)[..., None]
    return (y_diag + y_off).reshape(bsz, n, H_C, P_C), h_final


def dwconv(u, w, bias):
    out = lax.conv_general_dilated(u, w[:, None, :].astype(u.dtype), window_strides=(1,),
                                   padding=[(CONV_K // 2, CONV_K // 2)],
                                   dimension_numbers=('NWC', 'WIO', 'NWC'),
                                   feature_group_count=u.shape[-1])
    return out + bias


def ssm_mixer(z, xbc, dt_raw, p, h0):
    bsz, n = z.shape[:2]
    xbc = jax.nn.silu(dwconv(xbc, p['ssm_conv_w'], p['ssm_conv_b']))
    xs, b_in, c_in = jnp.split(xbc, [D_INNER, D_INNER + G_C * N_C], axis=-1)
    xs = xs.reshape(bsz, n, H_C, P_C)
    b_in = b_in.reshape(bsz, n, G_C, N_C)
    c_in = c_in.reshape(bsz, n, G_C, N_C)
    dt = jax.nn.softplus(dt_raw.astype(jnp.float32).reshape(bsz, n, 2, H_C)
                         + p['ssm_dt_bias'].astype(jnp.float32))
    a_neg = -jnp.exp(p['ssm_A_log'].astype(jnp.float32))
    y_f, h_f = ssd_scan(xs, dt[:, :, 0], a_neg[0], b_in, c_in, h0[:, 0])
    y_b, h_b = ssd_scan(xs[:, ::-1], dt[:, ::-1, 1], a_neg[1], b_in[:, ::-1], c_in[:, ::-1], h0[:, 1])
    y = y_f + y_b[:, ::-1] + p['ssm_D'].astype(jnp.float32)[:, None] * xs.astype(jnp.float32)
    y = y.reshape(bsz, n, D_INNER) * jax.nn.silu(z.astype(jnp.float32))
    y = rmsnorm(y, p['ssm_norm_g']).astype(z.dtype)
    return y, jnp.stack([h_f, h_b], axis=1)


def mla_expand(ckv, krope, p):
    bsz, n = ckv.shape[:2]
    kv = (ckv @ p['w_ukv']).reshape(bsz, n, H_A, NOPE_A + V_A)
    k_nope, v = kv[..., :NOPE_A], kv[..., NOPE_A:]
    k_rope = jnp.broadcast_to(krope[:, :, None, :], (bsz, n, H_A, ROPE_A))
    k = rmsnorm(jnp.concatenate([k_nope, k_rope], axis=-1), p['mla_qk_norm_k'])
    return k, v


def rope_tail(x, ang_r, ang_c):
    return jnp.concatenate([x[..., :NOPE_A], axial_rope(x[..., NOPE_A:], ang_r, ang_c)], axis=-1)


def mixing_sublayer(h, p, layer, ctx):
    bsz, n, _ = h.shape
    offs = np.cumsum(SPLIT_SIZES)[:-1].tolist()
    cq, ckv, krope, dq, dk, dv, z, xbc, dt_raw = jnp.split(h @ p['w_in'], offs, axis=-1)
    q_a = (rmsnorm(cq, p['mla_q_norm_g']) @ p['w_uq']).reshape(bsz, n, H_A, NOPE_A + ROPE_A)
    q_a = rmsnorm(q_a, p['mla_qk_norm_q'])
    ckv = rmsnorm(ckv, p['mla_kv_norm_g'])
    k_a, v_a = mla_expand(ckv, krope, p)
    q_d = rmsnorm(dq.reshape(bsz, n, H_B, 2, DH_B), p['diff_q_norm_g'])
    k_d = rmsnorm(dk.reshape(bsz, n, H_B, 2, DH_B), p['diff_k_norm_g'])
    v_d = dv.reshape(bsz, n, H_B, 2 * DH_B)
    if ctx is None:
        ctx_out = (ckv, krope, k_d.reshape(bsz, n, H_B, 2 * DH_B), v_d)
        h0 = jnp.zeros((bsz, 2, H_C, P_C, N_C), jnp.float32)
    else:
        ckv_c, krope_c, kd_c, vd_c, h0 = ctx
        ang_r, ang_c = grid_angles(n, ROPE_A)
        q_a = rope_tail(q_a, ang_r, ang_c)
        k_a = rope_tail(k_a, ang_r, ang_c)
        q_d = axial_rope(q_d, ang_r, ang_c)
        k_d = axial_rope(k_d, ang_r, ang_c)
        k_ac, v_ac = mla_expand(ckv_c, krope_c, p)
        k_a = jnp.concatenate([k_ac, k_a], axis=1)
        v_a = jnp.concatenate([v_ac, v_a], axis=1)
        k_d = jnp.concatenate([kd_c.reshape(bsz, -1, H_B, 2, DH_B), k_d], axis=1)
        v_d = jnp.concatenate([vd_c, v_d], axis=1)
        ctx_out = None
    o_a = softmax_attention(q_a, k_a, v_a).reshape(bsz, n, H_A * V_A)
    lam, lam_init = diff_lambda(p, layer)
    o_d = differential_attention(q_d, k_d, v_d, lam)
    o_d = (rmsnorm(o_d, p['diff_subln_g']) * (1.0 - lam_init)).reshape(bsz, n, H_B * 2 * DH_B)
    o_c, h_last = ssm_mixer(z, xbc, dt_raw, p, h0)
    out = jnp.concatenate([o_a, o_d, o_c], axis=-1) @ p['w_out']
    if ctx is None:
        return out, (ctx_out[0], ctx_out[1], ctx_out[2], ctx_out[3], h_last)
    return out, None


def trunk_layer(x, mod, p, layer, ctx):
    shift1, scale1, gate1, shift2, scale2, gate2 = jnp.split(mod, N_MOD, axis=-1)
    h = rmsnorm(x, p['norm1_g']) * (1.0 + scale1) + shift1
    mix, ctx_out = mixing_sublayer(h, p, layer, ctx)
    x = x + gate1 * mix
    h = rmsnorm(x, p['norm2_g']) * (1.0 + scale2) + shift2
    u = jnp.square(jax.nn.relu(h @ p['w_ff1']))
    x = x + gate2 * (u @ p['w_ff2'])
    return x, ctx_out


def setup_inputs(seed: int = 0) -> dict:
    key = jax.random.key(seed)
    ks = iter(jax.random.split(key, 40))

    def nrm(shape, scale):
        return jax.random.normal(next(ks), shape, jnp.float32) * scale

    def gain(shape):
        return 1.0 + 0.02 * jax.random.normal(next(ks), shape, jnp.float32)

    dt0 = jnp.exp(jax.random.uniform(next(ks), (DEPTH, 2, H_C), jnp.float32,
                                     minval=math.log(1e-3), maxval=math.log(1e-1)))
    return {
        'x_prompt': nrm((BATCH, SEQ, D_MODEL), 1.0),
        'x_sample': nrm((DEC_BATCH, DEC_SEQ, D_MODEL), 1.0),
        'cache_mla_ckv': nrm((DEC_BATCH, DEPTH, PAST_LEN, KV_RANK), 1.0),
        'cache_mla_krope': nrm((DEC_BATCH, DEPTH, PAST_LEN, ROPE_A), 1.0),
        'cache_diff_k': nrm((DEC_BATCH, DEPTH, PAST_LEN, H_B, 2 * DH_B), 1.0),
        'cache_diff_v': nrm((DEC_BATCH, DEPTH, PAST_LEN, H_B, 2 * DH_B), 1.0),
        'state_ssm': nrm((DEC_BATCH, DEPTH, 2, H_C, P_C, N_C), 0.1),
        'c': nrm((DEC_BATCH, D_MODEL), 1.0),
        'c_ctx': nrm((D_MODEL,), 1.0),
        'norm1_g': gain((DEPTH, D_MODEL)),
        'norm2_g': gain((DEPTH, D_MODEL)),
        'w_ada': nrm((DEPTH, D_MODEL, N_MOD * D_MODEL), D_MODEL ** -0.5),
        'b_ada': nrm((DEPTH, N_MOD * D_MODEL), 0.01),
        'w_in': nrm((DEPTH, D_MODEL, D_IN_PROJ), D_MODEL ** -0.5),
        'w_out': nrm((DEPTH, MIX_WIDTH, D_MODEL), MIX_WIDTH ** -0.5),
        'mla_q_norm_g': gain((DEPTH, Q_RANK)),
        'mla_kv_norm_g': gain((DEPTH, KV_RANK)),
        'w_uq': nrm((DEPTH, Q_RANK, H_A * (NOPE_A + ROPE_A)), Q_RANK ** -0.5),
        'w_ukv': nrm((DEPTH, KV_RANK, H_A * (NOPE_A + V_A)), KV_RANK ** -0.5),
        'mla_qk_norm_q': gain((DEPTH, NOPE_A + ROPE_A)),
        'mla_qk_norm_k': gain((DEPTH, NOPE_A + ROPE_A)),
        'diff_q_norm_g': gain((DEPTH, DH_B)),
        'diff_k_norm_g': gain((DEPTH, DH_B)),
        'diff_lq1': nrm((DEPTH, DH_B), 0.1),
        'diff_lk1': nrm((DEPTH, DH_B), 0.1),
        'diff_lq2': nrm((DEPTH, DH_B), 0.1),
        'diff_lk2': nrm((DEPTH, DH_B), 0.1),
        'diff_subln_g': gain((DEPTH, 2 * DH_B)),
        'ssm_conv_w': nrm((DEPTH, CONV_K, CONV_DIM), CONV_K ** -0.5),
        'ssm_conv_b': nrm((DEPTH, CONV_DIM), 0.01),
        'ssm_A_log': jnp.log(jax.random.uniform(next(ks), (DEPTH, 2, H_C), jnp.float32, minval=1.0, maxval=16.0)),
        'ssm_dt_bias': dt0 + jnp.log(-jnp.expm1(-dt0)),
        'ssm_D': gain((DEPTH, H_C)),
        'ssm_norm_g': gain((DEPTH, D_INNER)),
        'w_ff1': nrm((DEPTH, D_MODEL, D_FF), D_MODEL ** -0.5),
        'w_ff2': nrm((DEPTH, D_FF, D_MODEL), D_FF ** -0.5),
    }


def reference(x_prompt, x_sample, cache_mla_ckv, cache_mla_krope, cache_diff_k, cache_diff_v, state_ssm,
              c, c_ctx, norm1_g, norm2_g, w_ada, b_ada, w_in, w_out, mla_q_norm_g, mla_kv_norm_g,
              w_uq, w_ukv, mla_qk_norm_q, mla_qk_norm_k, diff_q_norm_g, diff_k_norm_g,
              diff_lq1, diff_lk1, diff_lq2, diff_lk2, diff_subln_g, ssm_conv_w, ssm_conv_b,
              ssm_A_log, ssm_dt_bias, ssm_D, ssm_norm_g, w_ff1, w_ff2):
    xp = x_prompt
    xs = x_sample
    ckv_l, krope_l, kd_l, vd_l, st_l = [], [], [], [], []
    for l in range(DEPTH):
        p = dict(norm1_g=norm1_g[l], norm2_g=norm2_g[l], w_in=w_in[l], w_out=w_out[l],
                 mla_q_norm_g=mla_q_norm_g[l], mla_kv_norm_g=mla_kv_norm_g[l], w_uq=w_uq[l], w_ukv=w_ukv[l],
                 mla_qk_norm_q=mla_qk_norm_q[l], mla_qk_norm_k=mla_qk_norm_k[l],
                 diff_q_norm_g=diff_q_norm_g[l], diff_k_norm_g=diff_k_norm_g[l],
                 diff_lq1=diff_lq1[l], diff_lk1=diff_lk1[l], diff_lq2=diff_lq2[l], diff_lk2=diff_lk2[l],
                 diff_subln_g=diff_subln_g[l], ssm_conv_w=ssm_conv_w[l], ssm_conv_b=ssm_conv_b[l],
                 ssm_A_log=ssm_A_log[l], ssm_dt_bias=ssm_dt_bias[l], ssm_D=ssm_D[l], ssm_norm_g=ssm_norm_g[l],
                 w_ff1=w_ff1[l], w_ff2=w_ff2[l])
        mod_ctx = (jax.nn.silu(c_ctx) @ w_ada[l] + b_ada[l])[None, None, :]
        xp, (ckv, krope, kd, vd, st) = trunk_layer(xp, mod_ctx, p, l, None)
        ckv_l.append(ckv)
        krope_l.append(krope)
        kd_l.append(kd)
        vd_l.append(vd)
        st_l.append(st)
        mod_lat = (jax.nn.silu(c) @ w_ada[l] + b_ada[l])[:, None, :]
        xs, _ = trunk_layer(xs, mod_lat, p, l,
                            (cache_mla_ckv[:, l], cache_mla_krope[:, l], cache_diff_k[:, l],
                             cache_diff_v[:, l], state_ssm[:, l]))
    new_mla_ckv = jnp.stack(ckv_l, axis=1)
    new_mla_krope = jnp.stack(krope_l, axis=1)
    new_diff_k = jnp.stack(kd_l, axis=1)
    new_diff_v = jnp.stack(vd_l, axis=1)
    new_ssm_state = jnp.stack(st_l, axis=1)
    return (xp, xs, new_mla_ckv, new_mla_krope, new_diff_k, new_diff_v, new_ssm_state)
```

```python
import functools
import math

import numpy as np
import jax
import jax.numpy as jnp
from jax import lax
from jax.experimental import pallas as pl
from jax.experimental.pallas import tpu as pltpu

D_MODEL = 1024
BATCH = 16
SEQ = 256
DEPTH = 4
DEC_BATCH = 4
DEC_SEQ = 2048
PAST_LEN = 256
GRID_W = 64
ROPE_BASE = 10000.0
EPS = 1e-6
CHUNK = 128
D_FF = 4 * D_MODEL
N_MOD = 6
H_A = 4
NOPE_A = 64
ROPE_A = 32
V_A = 64
Q_RANK = 256
KV_RANK = 128
H_B = 4
DH_B = 32
H_C = 8
P_C = 64
N_C = 64
G_C = 2
CONV_K = 5
D_INNER = H_C * P_C
CONV_DIM = D_INNER + 2 * G_C * N_C

F32 = jnp.float32
BF16 = jnp.bfloat16

LANES = 128
QK_A = NOPE_A + ROPE_A
WA = H_A * LANES
WD = H_B * 2 * DH_B
WVA = H_A * V_A
C_CQ, C_CKV, C_KR, C_DQ, C_DK, C_DV, C_Z, C_XBC, C_DT = 0, 256, 384, 512, 768, 1024, 1280, 1792, 2560
NP_IN = 2688
V7X_VMEM_LIMIT = 56 * 1024 * 1024
TM_TOK = 512
TQ = 256
FF_CHUNK = 1024


def _mm(a, b):
    return jnp.dot(a, b, preferred_element_type=F32)


def _mm_nt(a, b):
    return lax.dot_general(a, b, (((1,), (1,)), ((), ())), preferred_element_type=F32)


def _split3(x):
    hi = x.astype(BF16)
    r = x - hi.astype(F32)
    mid = r.astype(BF16)
    lo = (r - mid.astype(F32)).astype(BF16)
    return hi, mid, lo


def _sel_right(x, m01):
    hi, mid, lo = _split3(x)
    return _mm(hi, m01) + _mm(mid, m01) + _mm(lo, m01)


def _sel_left(m01, x):
    hi, mid, lo = _split3(x)
    return _mm(m01, hi) + _mm(m01, mid) + _mm(m01, lo)


def _sigmoid(x):
    return 1.0 / (1.0 + jnp.exp(-x))


def _rms(x, n=None):
    n = x.shape[-1] if n is None else n
    return x * lax.rsqrt(jnp.sum(x * x, axis=-1, keepdims=True) * (1.0 / n) + EPS)


def _rms_heads128(x, n_real):
    parts = [_rms(x[:, i * LANES:(i + 1) * LANES], n_real) for i in range(x.shape[1] // LANES)]
    return jnp.concatenate(parts, axis=-1)


def _rms_segments(x, bd, seg):
    ssq = _sel_right(x * x, bd)
    return x * lax.rsqrt(ssq * (1.0 / seg) + EPS)


def _rope(x, cos, sin_signed):
    lane = lax.broadcasted_iota(jnp.int32, (1, LANES), 1)
    lo = (lane % 16) < 8
    parts = []
    for i in range(x.shape[1] // LANES):
        xb = x[:, i * LANES:(i + 1) * LANES]
        parts.append(jnp.where(lo, pltpu.roll(xb, LANES - 8, 1), pltpu.roll(xb, 8, 1)))
    return x * cos + jnp.concatenate(parts, axis=-1) * sin_signed


def _mod_kernel(c_ref, w_ref, b_ref, o_ref):
    cv = c_ref[...]
    s = cv * _sigmoid(cv)
    o_ref[0] = _mm(s.astype(BF16), w_ref[0].astype(BF16)) + b_ref[0]


def _mod_call(cvec, w_ada, b_ada):
    tn = 1536
    n_out = N_MOD * D_MODEL
    return pl.pallas_call(
        _mod_kernel,
        grid=(DEPTH, n_out // tn),
        in_specs=[pl.BlockSpec((8, D_MODEL), lambda l, j: (0, 0)),
                  pl.BlockSpec((1, D_MODEL, tn), lambda l, j: (l, 0, j)),
                  pl.BlockSpec((1, 1, tn), lambda l, j: (l, 0, j))],
        out_specs=pl.BlockSpec((1, 8, tn), lambda l, j: (l, 0, j)),
        out_shape=jax.ShapeDtypeStruct((DEPTH, 8, n_out), F32),
        compiler_params=pltpu.CompilerParams(dimension_semantics=("arbitrary", "arbitrary")),
        name="mod",
    )(cvec, w_ada, b_ada.reshape(DEPTH, 1, n_out))


def _cache_kernel(ckv_ref, kr_ref, wuk_ref, wuv_ref, epl_ref, gka_ref, k_ref, v_ref):
    ckv = ckv_ref[0, 0].astype(BF16)
    kpre = _mm(ckv, wuk_ref[0]) + _sel_right(kr_ref[0, 0], epl_ref[...])
    k_ref[0, 0] = (_rms_heads128(kpre, QK_A) * gka_ref[0]).astype(BF16)
    v_ref[0, 0] = _mm(ckv, wuv_ref[0]).astype(BF16)


def _cache_call(cache_ckv, cache_kr_pad, P):
    bl = lambda b, l: (b, l, 0, 0)
    lay = lambda b, l: (l, 0, 0)
    return pl.pallas_call(
        _cache_kernel,
        grid=(DEC_BATCH, DEPTH),
        in_specs=[pl.BlockSpec((1, 1, PAST_LEN, KV_RANK), bl),
                  pl.BlockSpec((1, 1, PAST_LEN, LANES), bl),
                  pl.BlockSpec((1, KV_RANK, WA), lay),
                  pl.BlockSpec((1, KV_RANK, WVA), lay),
                  pl.BlockSpec((LANES, WA), lambda b, l: (0, 0)),
                  pl.BlockSpec((1, 1, WA), lay)],
        out_specs=[pl.BlockSpec((1, 1, PAST_LEN, WA), bl),
                   pl.BlockSpec((1, 1, PAST_LEN, WVA), bl)],
        out_shape=[jax.ShapeDtypeStruct((DEC_BATCH, DEPTH, PAST_LEN, WA), BF16),
                   jax.ShapeDtypeStruct((DEC_BATCH, DEPTH, PAST_LEN, WVA), BF16)],
        compiler_params=pltpu.CompilerParams(dimension_semantics=("arbitrary", "arbitrary")),
        name="cache_expand",
    )(cache_ckv, cache_kr_pad, P["wuk"], P["wuv"], P["epl"], P["gka"])


def _pre_kernel(latent, *refs):
    (x_ref, mod_ref, g1_ref, win_ref, gq_ref, gkv_ref, wuq_ref, wuk_ref, wuv_ref, epl_ref,
     gqa_ref, gka_ref, gdq_ref, gdk_ref, bd32_ref) = refs[:15]
    refs = refs[15:]
    if latent:
        cosa_ref, sina_ref, cosd_ref, sind_ref = refs[:4]
        refs = refs[4:]
    qa_ref, ka_ref, va_ref, qd_ref, kd_ref, vd_ref, z_ref, xbc_ref, dt_ref = refs[:9]
    refs = refs[9:]
    if not latent:
        ckvn_ref, krope_ref, kd32_ref, vd32_ref = refs

    x = x_ref[...]
    mod = mod_ref[0, 0]
    shift1 = mod[:, 0:D_MODEL]
    scale1 = mod[:, D_MODEL:2 * D_MODEL]
    h = (_rms(x) * g1_ref[0] * (1.0 + scale1) + shift1).astype(BF16)

    def proj(c0, c1):
        return _mm(h, win_ref[0, :, c0:c1])

    cqn = (_rms(proj(C_CQ, C_CKV)) * gq_ref[0]).astype(BF16)
    q = _rms_heads128(_mm(cqn, wuq_ref[0]), QK_A) * gqa_ref[0]
    if latent:
        q = _rope(q, cosa_ref[...], sina_ref[...])
    qa_ref[...] = q.astype(BF16)

    ckvn = _rms(proj(C_CKV, C_KR)) * gkv_ref[0]
    kr = proj(C_KR, C_DQ)
    ckvb = ckvn.astype(BF16)
    k = _rms_heads128(_mm(ckvb, wuk_ref[0]) + _sel_right(kr, epl_ref[...]), QK_A) * gka_ref[0]
    if latent:
        k = _rope(k, cosa_ref[...], sina_ref[...])
    ka_ref[...] = k.astype(BF16)
    va_ref[...] = _mm(ckvb, wuv_ref[0]).astype(BF16)

    bd32 = bd32_ref[...]
    qd = _rms_segments(proj(C_DQ, C_DK), bd32, DH_B) * gdq_ref[0]
    kd = _rms_segments(proj(C_DK, C_DV), bd32, DH_B) * gdk_ref[0]
    vd = proj(C_DV, C_Z)
    if latent:
        qd = _rope(qd, cosd_ref[...], sind_ref[...])
        kd = _rope(kd, cosd_ref[...], sind_ref[...])
    qd_ref[...] = qd.astype(BF16)
    kd_ref[...] = kd.astype(BF16)
    vd_ref[...] = vd.astype(BF16)

    z_ref[...] = proj(C_Z, C_XBC)
    xbc_ref[...] = proj(C_XBC, C_DT)
    dt_ref[...] = proj(C_DT, NP_IN)

    if not latent:
        ckvn_ref[...] = ckvn
        krope_ref[...] = kr[:, 0:ROPE_A]
        kd32_ref[...] = kd
        vd32_ref[...] = vd


def _pre_call(x2d, mod4, P, l, latent):
    T = x2d.shape[0]
    tm = TM_TOK
    tpb = DEC_SEQ // tm
    row = lambda i: (i, 0)
    lay = lambda i: (l, 0, 0)
    if latent:
        mod_map = lambda i: (l, i // tpb, 0, 0)
    else:
        mod_map = lambda i: (l, DEC_BATCH, 0, 0)
    full = lambda i: (0, 0)
    in_specs = [pl.BlockSpec((tm, D_MODEL), row),
                pl.BlockSpec((1, 1, 1, N_MOD * D_MODEL), mod_map),
                pl.BlockSpec((1, 1, D_MODEL), lay),
                pl.BlockSpec((1, D_MODEL, NP_IN), lay),
                pl.BlockSpec((1, 1, Q_RANK), lay),
                pl.BlockSpec((1, 1, KV_RANK), lay),
                pl.BlockSpec((1, Q_RANK, WA), lay),
                pl.BlockSpec((1, KV_RANK, WA), lay),
                pl.BlockSpec((1, KV_RANK, WVA), lay),
                pl.BlockSpec((LANES, WA), full),
                pl.BlockSpec((1, 1, WA), lay),
                pl.BlockSpec((1, 1, WA), lay),
                pl.BlockSpec((1, 1, WD), lay),
                pl.BlockSpec((1, 1, WD), lay),
                pl.BlockSpec((WD, WD), full)]
    args = [x2d, mod4, P["g1"], P["win"], P["gq"], P["gkv"], P["wuq"], P["wuk"], P["wuv"], P["epl"],
            P["gqa"], P["gka"], P["gdq"], P["gdk"], P["bd32"]]
    if latent:
        pos = lambda i: (i % tpb, 0)
        in_specs += [pl.BlockSpec((tm, WA), pos), pl.BlockSpec((tm, WA), pos),
                     pl.BlockSpec((tm, WD), pos), pl.BlockSpec((tm, WD), pos)]
        args += [P["cosa"], P["sina"], P["cosd"], P["sind"]]
    widths = [(WA, BF16), (WA, BF16), (WVA, BF16), (WD, BF16), (WD, BF16), (WD, BF16),
              (D_INNER, F32), (CONV_DIM, F32), (LANES, F32)]
    if not latent:
        widths += [(KV_RANK, F32), (ROPE_A, F32), (WD, F32), (WD, F32)]
    out_specs = [pl.BlockSpec((tm, w), row) for w, _ in widths]
    out_shape = [jax.ShapeDtypeStruct((T, w), dt) for w, dt in widths]
    return pl.pallas_call(
        functools.partial(_pre_kernel, latent),
        grid=(T // tm,),
        in_specs=in_specs, out_specs=out_specs, out_shape=out_shape,
        compiler_params=pltpu.CompilerParams(dimension_semantics=("arbitrary",),
                                             vmem_limit_bytes=V7X_VMEM_LIMIT),
        name="pre_lat" if latent else "pre_ctx",
    )(*args)


def _softmax_pv(q, key_refs, val_refs, c0, c1):
    s = [_mm_nt(q, kr[:, c0:c1]) for kr in key_refs]
    m = s[0].max(axis=-1, keepdims=True)
    for si in s[1:]:
        m = jnp.maximum(m, si.max(axis=-1, keepdims=True))
    p = [jnp.exp(si - m) for si in s]
    den = p[0].sum(axis=-1, keepdims=True)
    for pi in p[1:]:
        den = den + pi.sum(axis=-1, keepdims=True)
    pv = _mm(p[0].astype(BF16), val_refs[0][...])
    for pi, vr in zip(p[1:], val_refs[1:]):
        pv = pv + _mm(pi.astype(BF16), vr[...])
    return pv, 1.0 / den


def _attn_kernel(has_cache, lam_init, *refs):
    qa_ref, ka_ref, va_ref, qd_ref, kd_ref, vd_ref = refs[:6]
    refs = refs[6:]
    if has_cache:
        kca_ref, vca_ref, kcd_ref, vcd_ref = refs[:4]
        refs = refs[4:]
    lq1_ref, lk1_ref, lq2_ref, lk2_ref, gsub_ref, bd64_ref, o_ref = refs

    ka_refs = [ka_ref.at[0]]
    va_refs = [va_ref.at[0]]
    kd_refs = [kd_ref.at[0]]
    vd_refs = [vd_ref.at[0]]
    if has_cache:
        ka_refs = [kca_ref.at[0, 0]] + ka_refs
        va_refs = [vca_ref.at[0, 0]] + va_refs
        kd_refs = [kcd_ref.at[0, 0]] + kd_refs
        vd_refs = [vcd_ref.at[0, 0]] + vd_refs

    lane256 = lax.broadcasted_iota(jnp.int32, (1, WD), 1)
    lane128 = lax.broadcasted_iota(jnp.int32, (1, LANES), 1)

    qa = qa_ref[0]
    oa = None
    for hh in range(H_A):
        pv, rden = _softmax_pv(qa[:, hh * LANES:(hh + 1) * LANES], ka_refs, va_refs,
                               hh * LANES, (hh + 1) * LANES)
        contrib = jnp.where(lane256 // V_A == hh, pv * rden, 0.0)
        oa = contrib if oa is None else oa + contrib

    lam = (jnp.exp(jnp.sum(lq1_ref[0] * lk1_ref[0], axis=-1, keepdims=True))
           - jnp.exp(jnp.sum(lq2_ref[0] * lk2_ref[0], axis=-1, keepdims=True)) + lam_init)
    qd = qd_ref[0]
    od = None
    for hh in range(H_B):
        blk = hh // 2
        qblk = qd[:, blk * LANES:(blk + 1) * LANES]
        maps = []
        for mm_i in range(2):
            seg = (hh % 2) * 2 + mm_i
            qm = jnp.where(lane128 // DH_B == seg, qblk, jnp.zeros_like(qblk))
            pv, rden = _softmax_pv(qm, kd_refs, vd_refs, blk * LANES, (blk + 1) * LANES)
            maps.append(pv * rden)
        contrib = jnp.where(lane256 // (2 * DH_B) == hh, maps[0] - lam * maps[1], 0.0)
        od = contrib if od is None else od + contrib
    od = _rms_segments(od, bd64_ref[...], 2 * DH_B) * (gsub_ref[0] * (1.0 - lam_init))
    o_ref[0] = jnp.concatenate([oa, od], axis=-1).astype(BF16)


def _attn_call(qa, ka, va, qd, kd, vd, caches, P, l):
    B, L = qa.shape[0], qa.shape[1]
    has_cache = caches is not None
    lam_init = 0.8 - 0.6 * math.exp(-0.3 * l)
    qmap = lambda b, i: (b, i, 0)
    kmap = lambda b, i: (b, 0, 0)
    lay = lambda b, i: (l, 0, 0)
    in_specs = [pl.BlockSpec((1, TQ, WA), qmap), pl.BlockSpec((1, L, WA), kmap),
                pl.BlockSpec((1, L, WVA), kmap), pl.BlockSpec((1, TQ, WD), qmap),
                pl.BlockSpec((1, L, WD), kmap), pl.BlockSpec((1, L, WD), kmap)]
    args = [qa, ka, va, qd, kd, vd]
    if has_cache:
        cmap = lambda b, i: (b, l, 0, 0)
        in_specs += [pl.BlockSpec((1, 1, PAST_LEN, WA), cmap), pl.BlockSpec((1, 1, PAST_LEN, WVA), cmap),
                     pl.BlockSpec((1, 1, PAST_LEN, WD), cmap), pl.BlockSpec((1, 1, PAST_LEN, WD), cmap)]
        args += list(caches)
    in_specs += [pl.BlockSpec((1, 1, DH_B), lay)] * 4
    in_specs += [pl.BlockSpec((1, 1, WD), lay), pl.BlockSpec((WD, WD), lambda b, i: (0, 0))]
    args += [P["lq1"], P["lk1"], P["lq2"], P["lk2"], P["gsub"], P["bd64"]]
    return pl.pallas_call(
        functools.partial(_attn_kernel, has_cache, lam_init),
        grid=(B, L // TQ),
        in_specs=in_specs,
        out_specs=pl.BlockSpec((1, TQ, WVA + WD), qmap),
        out_shape=jax.ShapeDtypeStruct((B, L, WVA + WD), BF16),
        compiler_params=pltpu.CompilerParams(dimension_semantics=("arbitrary", "arbitrary"),
                                             vmem_limit_bytes=V7X_VMEM_LIMIT),
        name="attn_lat" if has_cache else "attn_ctx",
    )(*args)


def _ssd_kernel(L, has_h0, *refs):
    z_ref, xbc_ref, dt_ref = refs[:3]
    refs = refs[3:]
    if has_h0:
        s0_ref = refs[0]
        refs = refs[1:]
    cw_ref, cb_ref, alog_ref, dtb_ref, dexp_ref, g_ref, ex_ref, tri_ref = refs[:8]
    refs = refs[8:]
    o_ref = refs[0]
    refs = refs[1:]
    if not has_h0:
        st_ref = refs[0]
        refs = refs[1:]
    xpad, xc, yacc, dtp, s_scr = refs
    nc = L // CHUNK
    halo = 8

    xpad[0:halo, :] = jnp.zeros((halo, CONV_DIM), F32)
    xpad[L + halo:L + 2 * halo, :] = jnp.zeros((halo, CONV_DIM), F32)
    xpad[halo:L + halo, :] = xbc_ref[0]
    cw = cw_ref[0]
    cb = cb_ref[0]

    def conv_body(c, carry):
        base = pl.multiple_of(c * CHUNK, CHUNK)
        win = xpad[pl.ds(base, CHUNK + 2 * halo), :]
        acc = cb
        for k in range(CONV_K):
            off = halo - CONV_K // 2 + k
            acc = acc + win[off:off + CHUNK, :] * cw[k:k + 1, :]
        xc[pl.ds(base, CHUNK), :] = acc * _sigmoid(acc)
        return carry

    lax.fori_loop(0, nc, conv_body, 0)

    dtr = dt_ref[0] + dtb_ref[0]
    dtp[...] = jnp.maximum(dtr, 0.0) + jnp.log(1.0 + jnp.exp(-jnp.abs(dtr)))
    a_neg = -jnp.exp(alog_ref[0])

    row_i = lax.broadcasted_iota(jnp.int32, (CHUNK, CHUNK), 0)
    col_j = lax.broadcasted_iota(jnp.int32, (CHUNK, CHUNK), 1)
    lane128 = lax.broadcasted_iota(jnp.int32, (1, LANES), 1)
    blockmask = (lax.broadcasted_iota(jnp.int32, (2 * N_C, D_INNER), 0) // N_C
                 == lax.broadcasted_iota(jnp.int32, (2 * N_C, D_INNER), 1) // (D_INNER // G_C))
    dexp = dexp_ref[0]
    gnorm = g_ref[0]

    for d in range(2):
        if has_h0:
            s_scr[...] = s0_ref[0, 0, d]
        else:
            s_scr[...] = jnp.zeros((2 * N_C, D_INNER), F32)
        causal = (row_i >= col_j) if d == 0 else (col_j >= row_i)
        ex = ex_ref[d]
        tri = tri_ref[d]

        def chunk_body(step, carry, d=d, causal=causal, ex=ex, tri=tri):
            c = step if d == 0 else nc - 1 - step
            base = pl.multiple_of(c * CHUNK, CHUNK)
            xs = xc[pl.ds(base, CHUNK), 0:D_INNER]
            b_c = xc[pl.ds(base, CHUNK), D_INNER:D_INNER + LANES]
            c_c = xc[pl.ds(base, CHUNK), D_INNER + LANES:D_INNER + 2 * LANES]
            dtc = dtp[pl.ds(base, CHUNK), :]
            cum = _sel_left(tri, dtc * a_neg)
            cum_t = cum.T
            dt_t = dtc.T
            last = cum[CHUNK - 1:CHUNK, :] if d == 0 else cum[0:1, :]
            wexp = _sel_right(jnp.exp(last - cum) * dtc, ex)
            eexp = _sel_right(jnp.exp(cum), ex)
            cb16 = c_c.astype(BF16)
            bb16 = b_c.astype(BF16)
            cbs = [_mm_nt(jnp.where(lane128 // N_C == g, cb16, jnp.zeros_like(cb16)), bb16)
                   for g in range(G_C)]
            sv = s_scr[...]
            y_off = _mm(cb16, sv.astype(BF16)) * eexp
            ys = []
            for pair in range(H_C // 2):
                xp2 = xs[:, pair * LANES:(pair + 1) * LANES].astype(BF16)
                res = []
                for hh in (2 * pair, 2 * pair + 1):
                    idx = d * H_C + hh
                    seg = cum[:, idx:idx + 1] - cum_t[idx:idx + 1, :]
                    dec = jnp.where(causal, jnp.exp(seg), 0.0)
                    m = cbs[hh // (H_C // G_C)] * dec * dt_t[idx:idx + 1, :]
                    res.append(_mm(m.astype(BF16), xp2))
                ys.append(jnp.where(lane128 < P_C, res[0], res[1]))
            y = jnp.concatenate(ys, axis=-1) + y_off
            upd = _mm(b_c.T.astype(BF16), (xs * wexp).astype(BF16))
            cd = eexp[CHUNK - 1:CHUNK, :] if d == 0 else eexp[0:1, :]
            s_scr[...] = sv * cd + jnp.where(blockmask, upd, 0.0)
            if d == 0:
                yacc[pl.ds(base, CHUNK), :] = y
            else:
                zc = z_ref[0, pl.ds(base, CHUNK), :]
                yt = (yacc[pl.ds(base, CHUNK), :] + y + dexp * xs) * (zc * _sigmoid(zc))
                o_ref[0, pl.ds(base, CHUNK), :] = (_rms(yt) * gnorm).astype(BF16)
            return carry

        lax.fori_loop(0, nc, chunk_body, 0)
        if not has_h0:
            st = s_scr[...].T
            st_ref[0, d] = (st + pltpu.roll(st, N_C, 1))[:, 0:N_C]


def _ssd_call(z, xbc, dt, s0, P, l):
    B, L = z.shape[0], z.shape[1]
    has_h0 = s0 is not None
    bmap = lambda b: (b, 0, 0)
    lay = lambda b: (l, 0, 0)
    in_specs = [pl.BlockSpec((1, L, D_INNER), bmap), pl.BlockSpec((1, L, CONV_DIM), bmap),
                pl.BlockSpec((1, L, LANES), bmap)]
    args = [z, xbc, dt]
    if has_h0:
        in_specs.append(pl.BlockSpec((1, 1, 2, 2 * N_C, D_INNER), lambda b: (b, l, 0, 0, 0)))
        args.append(s0)
    in_specs += [pl.BlockSpec((1, 8, CONV_DIM), lay), pl.BlockSpec((1, 1, CONV_DIM), lay),
                 pl.BlockSpec((1, 1, LANES), lay), pl.BlockSpec((1, 1, LANES), lay),
                 pl.BlockSpec((1, 1, D_INNER), lay), pl.BlockSpec((1, 1, D_INNER), lay),
                 pl.BlockSpec((2, LANES, D_INNER), lambda b: (0, 0, 0)),
                 pl.BlockSpec((2, CHUNK, CHUNK), lambda b: (0, 0, 0))]
    args += [P["cw"], P["cb"], P["alog"], P["dtb"], P["dexp"], P["gssm"], P["ex"], P["tri"]]
    out_specs = [pl.BlockSpec((1, L, D_INNER), bmap)]
    out_shape = [jax.ShapeDtypeStruct((B, L, D_INNER), BF16)]
    if not has_h0:
        out_specs.append(pl.BlockSpec((1, 2, D_INNER, N_C), lambda b: (b, 0, 0, 0)))
        out_shape.append(jax.ShapeDtypeStruct((B, 2, D_INNER, N_C), F32))
    return pl.pallas_call(
        functools.partial(_ssd_kernel, L, has_h0),
        grid=(B,),
        in_specs=in_specs, out_specs=out_specs, out_shape=out_shape,
        scratch_shapes=[pltpu.VMEM((L + 16, CONV_DIM), F32), pltpu.VMEM((L, CONV_DIM), F32),
                        pltpu.VMEM((L, D_INNER), F32), pltpu.VMEM((L, LANES), F32),
                        pltpu.VMEM((2 * N_C, D_INNER), F32)],
        compiler_params=pltpu.CompilerParams(dimension_semantics=("arbitrary",),
                                             vmem_limit_bytes=V7X_VMEM_LIMIT),
        name="ssd_lat" if has_h0 else "ssd_ctx",
    )(*args)


def _post_kernel(x_ref, oat_ref, oc_ref, mod_ref, wo_ref, g2_ref, w1_ref, w2_ref, o_ref):
    mod = mod_ref[0, 0]
    gate1 = mod[:, 2 * D_MODEL:3 * D_MODEL]
    shift2 = mod[:, 3 * D_MODEL:4 * D_MODEL]
    scale2 = mod[:, 4 * D_MODEL:5 * D_MODEL]
    gate2 = mod[:, 5 * D_MODEL:6 * D_MODEL]
    n_att = WVA + WD
    mix = _mm(oat_ref[...], wo_ref[0, 0:n_att, :]) + _mm(oc_ref[...], wo_ref[0, n_att:n_att + D_INNER, :])
    x1 = x_ref[...] + gate1 * mix
    h2 = (_rms(x1) * g2_ref[0] * (1.0 + scale2) + shift2).astype(BF16)
    ff = None
    for c in range(D_FF // FF_CHUNK):
        u = jnp.maximum(_mm(h2, w1_ref[0, :, c * FF_CHUNK:(c + 1) * FF_CHUNK]), 0.0)
        part = _mm((u * u).astype(BF16), w2_ref[0, c * FF_CHUNK:(c + 1) * FF_CHUNK, :])
        ff = part if ff is None else ff + part
    o_ref[...] = x1 + gate2 * ff


def _post_call(x2d, oat, oc, mod4, P, l, latent):
    T = x2d.shape[0]
    tm = TM_TOK
    tpb = DEC_SEQ // tm
    row = lambda i: (i, 0)
    lay = lambda i: (l, 0, 0)
    if latent:
        mod_map = lambda i: (l, i // tpb, 0, 0)
    else:
        mod_map = lambda i: (l, DEC_BATCH, 0, 0)
    single = pl.Buffered(1)
    return pl.pallas_call(
        _post_kernel,
        grid=(T // tm,),
        in_specs=[pl.BlockSpec((tm, D_MODEL), row),
                  pl.BlockSpec((tm, WVA + WD), row),
                  pl.BlockSpec((tm, D_INNER), row),
                  pl.BlockSpec((1, 1, 1, N_MOD * D_MODEL), mod_map),
                  pl.BlockSpec((1, D_MODEL, D_MODEL), lay, pipeline_mode=single),
                  pl.BlockSpec((1, 1, D_MODEL), lay),
                  pl.BlockSpec((1, D_MODEL, D_FF), lay, pipeline_mode=single),
                  pl.BlockSpec((1, D_FF, D_MODEL), lay, pipeline_mode=single)],
        out_specs=pl.BlockSpec((tm, D_MODEL), row),
        out_shape=jax.ShapeDtypeStruct((T, D_MODEL), F32),
        compiler_params=pltpu.CompilerParams(dimension_semantics=("arbitrary",),
                                             vmem_limit_bytes=V7X_VMEM_LIMIT),
        name="post_lat" if latent else "post_ctx",
    )(x2d, oat, oc, mod4, P["wout"], P["g2"], P["wff1"], P["wff2"])


def _rope_tables():
    t = np.arange(DEC_SEQ)
    rows = (t // GRID_W).astype(np.float32)
    cols = (t % GRID_W).astype(np.float32)
    half = ROPE_A // 2
    freqs = jnp.asarray(ROPE_BASE, F32) ** (-jnp.arange(0, half, 2, dtype=F32) / half)
    ang_r = jnp.asarray(rows)[:, None] * freqs
    ang_c = jnp.asarray(cols)[:, None] * freqs
    cos32 = jnp.concatenate([jnp.cos(ang_r), jnp.cos(ang_r), jnp.cos(ang_c), jnp.cos(ang_c)], axis=-1)
    sin32 = jnp.concatenate([-jnp.sin(ang_r), jnp.sin(ang_r), -jnp.sin(ang_c), jnp.sin(ang_c)], axis=-1)
    ones = jnp.ones((DEC_SEQ, NOPE_A), F32)
    zeros = jnp.zeros((DEC_SEQ, NOPE_A), F32)
    pad1 = jnp.ones((DEC_SEQ, LANES - QK_A), F32)
    pad0 = jnp.zeros((DEC_SEQ, LANES - QK_A), F32)
    cosa = jnp.tile(jnp.concatenate([ones, cos32, pad1], axis=-1), (1, H_A))
    sina = jnp.tile(jnp.concatenate([zeros, sin32, pad0], axis=-1), (1, H_A))
    cosd = jnp.tile(cos32, (1, WD // ROPE_A))
    sind = jnp.tile(sin32, (1, WD // ROPE_A))
    return cosa, sina, cosd, sind


def _constants():
    epl = np.zeros((LANES, WA), np.float32)
    for hh in range(H_A):
        for r in range(ROPE_A):
            epl[r, hh * LANES + NOPE_A + r] = 1.0
    lane = np.arange(WD)
    bd32 = (lane[:, None] // DH_B == lane[None, :] // DH_B).astype(np.float32)
    bd64 = (lane[:, None] // (2 * DH_B) == lane[None, :] // (2 * DH_B)).astype(np.float32)
    ex = np.zeros((2, LANES, D_INNER), np.float32)
    for d in range(2):
        for hh in range(H_C):
            ex[d, d * H_C + hh, hh * P_C:(hh + 1) * P_C] = 1.0
    i = np.arange(CHUNK)
    tri = np.stack([(i[None, :] <= i[:, None]), (i[None, :] >= i[:, None])]).astype(np.float32)
    return dict(epl=jnp.asarray(epl, BF16), bd32=jnp.asarray(bd32, BF16), bd64=jnp.asarray(bd64, BF16),
                ex=jnp.asarray(ex, BF16), tri=jnp.asarray(tri, BF16))


def _pad_last(a, n):
    return jnp.pad(a, [(0, 0)] * (a.ndim - 1) + [(0, n - a.shape[-1])])


def _prep_params(norm1_g, norm2_g, w_in, w_out, mla_q_norm_g, mla_kv_norm_g, w_uq, w_ukv, mla_qk_norm_q,
                 mla_qk_norm_k, diff_q_norm_g, diff_k_norm_g, diff_lq1, diff_lk1, diff_lq2, diff_lk2,
                 diff_subln_g, ssm_conv_w, ssm_conv_b, ssm_A_log, ssm_dt_bias, ssm_D, ssm_norm_g, w_ff1, w_ff2):
    P = _constants()
    offs = np.cumsum((Q_RANK, KV_RANK, ROPE_A, WD, WD, WD, D_INNER, CONV_DIM, 2 * H_C))
    o = [0] + offs.tolist()
    sl = lambda i: w_in[:, :, o[i]:o[i + 1]]
    win = jnp.concatenate([sl(0), sl(1), _pad_last(sl(2), LANES), sl(3), sl(4), sl(5), sl(6), sl(7),
                           _pad_last(sl(8), LANES)], axis=-1)
    P["win"] = win.astype(BF16)
    P["wuq"] = _pad_last(w_uq.reshape(DEPTH, Q_RANK, H_A, QK_A), LANES).reshape(DEPTH, Q_RANK, WA).astype(BF16)
    wkv = w_ukv.reshape(DEPTH, KV_RANK, H_A, NOPE_A + V_A)
    P["wuk"] = _pad_last(wkv[..., :NOPE_A], LANES).reshape(DEPTH, KV_RANK, WA).astype(BF16)
    P["wuv"] = wkv[..., NOPE_A:].reshape(DEPTH, KV_RANK, WVA).astype(BF16)
    P["wout"] = w_out.astype(BF16)
    P["wff1"] = w_ff1.astype(BF16)
    P["wff2"] = w_ff2.astype(BF16)
    vec = lambda a: a.reshape(DEPTH, 1, -1)
    P["g1"] = vec(norm1_g)
    P["g2"] = vec(norm2_g)
    P["gq"] = vec(mla_q_norm_g)
    P["gkv"] = vec(mla_kv_norm_g)
    P["gqa"] = vec(jnp.tile(_pad_last(mla_qk_norm_q, LANES), (1, H_A))) * (QK_A ** -0.5)
    P["gka"] = vec(jnp.tile(_pad_last(mla_qk_norm_k, LANES), (1, H_A)))
    P["gdq"] = vec(jnp.tile(diff_q_norm_g, (1, WD // DH_B))) * (DH_B ** -0.5)
    P["gdk"] = vec(jnp.tile(diff_k_norm_g, (1, WD // DH_B)))
    P["gsub"] = vec(jnp.tile(diff_subln_g, (1, H_B)))
    P["lq1"], P["lk1"], P["lq2"], P["lk2"] = vec(diff_lq1), vec(diff_lk1), vec(diff_lq2), vec(diff_lk2)
    P["cw"] = jnp.pad(ssm_conv_w, ((0, 0), (0, 8 - CONV_K), (0, 0)))
    P["cb"] = vec(ssm_conv_b)
    P["alog"] = vec(_pad_last(ssm_A_log.reshape(DEPTH, 2 * H_C), LANES))
    P["dtb"] = vec(_pad_last(ssm_dt_bias.reshape(DEPTH, 2 * H_C), LANES))
    P["dexp"] = vec(jnp.repeat(ssm_D, P_C, axis=-1))
    P["gssm"] = vec(ssm_norm_g)
    P["cosa"], P["sina"], P["cosd"], P["sind"] = _rope_tables()
    return P


def kernel(x_prompt, x_sample, cache_mla_ckv, cache_mla_krope, cache_diff_k, cache_diff_v, state_ssm, c, c_ctx, norm1_g, norm2_g, w_ada, b_ada, w_in, w_out, mla_q_norm_g, mla_kv_norm_g, w_uq, w_ukv, mla_qk_norm_q, mla_qk_norm_k, diff_q_norm_g, diff_k_norm_g, diff_lq1, diff_lk1, diff_lq2, diff_lk2, diff_subln_g, ssm_conv_w, ssm_conv_b, ssm_A_log, ssm_dt_bias, ssm_D, ssm_norm_g, w_ff1, w_ff2):
    P = _prep_params(norm1_g, norm2_g, w_in, w_out, mla_q_norm_g, mla_kv_norm_g, w_uq, w_ukv, mla_qk_norm_q,
                     mla_qk_norm_k, diff_q_norm_g, diff_k_norm_g, diff_lq1, diff_lk1, diff_lq2, diff_lk2,
                     diff_subln_g, ssm_conv_w, ssm_conv_b, ssm_A_log, ssm_dt_bias, ssm_D, ssm_norm_g,
                     w_ff1, w_ff2)

    cvec = jnp.concatenate([c, c_ctx[None, :], jnp.zeros((8 - DEC_BATCH - 1, D_MODEL), F32)], axis=0)
    mod4 = _mod_call(cvec, w_ada, b_ada).reshape(DEPTH, 8, 1, N_MOD * D_MODEL)

    kca, vca = _cache_call(cache_mla_ckv, _pad_last(cache_mla_krope, LANES), P)
    kcd = cache_diff_k.reshape(DEC_BATCH, DEPTH, PAST_LEN, WD).astype(BF16)
    vcd = cache_diff_v.reshape(DEC_BATCH, DEPTH, PAST_LEN, WD).astype(BF16)
    st = jnp.transpose(state_ssm, (0, 1, 2, 5, 3, 4)).reshape(DEC_BATCH, DEPTH, 2, N_C, D_INNER)
    half = D_INNER // G_C
    zero = jnp.zeros_like(st[..., :half])
    s0 = jnp.concatenate([jnp.concatenate([st[..., :half], zero], axis=-1),
                          jnp.concatenate([zero, st[..., half:]], axis=-1)], axis=-2)

    xp = x_prompt.reshape(BATCH * SEQ, D_MODEL)
    xs = x_sample.reshape(DEC_BATCH * DEC_SEQ, D_MODEL)
    ckv_l, krope_l, kd_l, vd_l, st_l = [], [], [], [], []
    for l in range(DEPTH):
        qa, ka, va, qd, kd, vd, z, xbc, dt, ckvn, krope, kd32, vd32 = _pre_call(xp, mod4, P, l, False)
        r3 = lambda a: a.reshape(BATCH, SEQ, a.shape[-1])
        oat = _attn_call(r3(qa), r3(ka), r3(va), r3(qd), r3(kd), r3(vd), None, P, l)
        oc, st_new = _ssd_call(r3(z), r3(xbc), r3(dt), None, P, l)
        xp = _post_call(xp, oat.reshape(BATCH * SEQ, -1), oc.reshape(BATCH * SEQ, -1), mod4, P, l, False)
        ckv_l.append(ckvn.reshape(BATCH, SEQ, KV_RANK))
        krope_l.append(krope.reshape(BATCH, SEQ, ROPE_A))
        kd_l.append(kd32.reshape(BATCH, SEQ, H_B, 2 * DH_B))
        vd_l.append(vd32.reshape(BATCH, SEQ, H_B, 2 * DH_B))
        st_l.append(st_new.reshape(BATCH, 2, H_C, P_C, N_C))
        qa, ka, va, qd, kd, vd, z, xbc, dt = _pre_call(xs, mod4, P, l, True)
        r3 = lambda a: a.reshape(DEC_BATCH, DEC_SEQ, a.shape[-1])
        oat = _attn_call(r3(qa), r3(ka), r3(va), r3(qd), r3(kd), r3(vd), (kca, vca, kcd, vcd), P, l)
        oc, = _ssd_call(r3(z), r3(xbc), r3(dt), s0, P, l)
        xs = _post_call(xs, oat.reshape(DEC_BATCH * DEC_SEQ, -1), oc.reshape(DEC_BATCH * DEC_SEQ, -1),
                        mod4, P, l, True)
    return (xp.reshape(BATCH, SEQ, D_MODEL), xs.reshape(DEC_BATCH, DEC_SEQ, D_MODEL),
            jnp.stack(ckv_l, axis=1), jnp.stack(krope_l, axis=1), jnp.stack(kd_l, axis=1),
            jnp.stack(vd_l, axis=1), jnp.stack(st_l, axis=1))
```

```python
import functools
import math

import numpy as np
import jax
import jax.numpy as jnp
from jax import lax
from jax.experimental import pallas as pl
from jax.experimental.pallas import tpu as pltpu

D_MODEL = 1024
BATCH = 16
SEQ = 256
DEPTH = 4
DEC_BATCH = 4
DEC_SEQ = 2048
PAST_LEN = 256
GRID_W = 64
ROPE_BASE = 10000.0
EPS = 1e-6
CHUNK = 128
D_FF = 4 * D_MODEL
N_MOD = 6
H_A = 4
NOPE_A = 64
ROPE_A = 32
V_A = 64
Q_RANK = 256
KV_RANK = 128
H_B = 4
DH_B = 32
H_C = 8
P_C = 64
N_C = 64
G_C = 2
CONV_K = 5
D_INNER = H_C * P_C
CONV_DIM = D_INNER + 2 * G_C * N_C

F32 = jnp.float32
BF16 = jnp.bfloat16

LANES = 128
QK_A = NOPE_A + ROPE_A
WA = H_A * LANES
WD = H_B * 2 * DH_B
WVA = H_A * V_A
C_CQ, C_CKV, C_KR, C_DQ, C_DK, C_DV, C_Z, C_XBC, C_DT = 0, 256, 384, 512, 768, 1024, 1280, 1792, 2560
NP_IN = 2688
V7X_VMEM_LIMIT = 56 * 1024 * 1024
TM_TOK = 512
TQ = 256
FF_CHUNK = 1024


def _mm(a, b):
    return jnp.dot(a, b, preferred_element_type=F32)


def _mm_nt(a, b):
    return lax.dot_general(a, b, (((1,), (1,)), ((), ())), preferred_element_type=F32)


def _split3(x):
    hi = x.astype(BF16)
    r = x - hi.astype(F32)
    mid = r.astype(BF16)
    lo = (r - mid.astype(F32)).astype(BF16)
    return hi, mid, lo


def _sel_right(x, m01):
    hi, mid, lo = _split3(x)
    return _mm(hi, m01) + _mm(mid, m01) + _mm(lo, m01)


def _sel_left(m01, x):
    hi, mid, lo = _split3(x)
    return _mm(m01, hi) + _mm(m01, mid) + _mm(m01, lo)


def _sigmoid(x):
    return 1.0 / (1.0 + jnp.exp(-x))


def _rms(x, n=None):
    n = x.shape[-1] if n is None else n
    return x * lax.rsqrt(jnp.sum(x * x, axis=-1, keepdims=True) * (1.0 / n) + EPS)


def _rms_heads128(x, n_real):
    parts = [_rms(x[:, i * LANES:(i + 1) * LANES], n_real) for i in range(x.shape[1] // LANES)]
    return jnp.concatenate(parts, axis=-1)


def _rms_segments(x, bd, seg):
    sq = x * x
    hi = sq.astype(BF16)
    lo = (sq - hi.astype(F32)).astype(BF16)
    ssq = _mm(hi, bd) + _mm(lo, bd)
    return x * lax.rsqrt(ssq * (1.0 / seg) + EPS)


def _rope(x, cos, sin_signed):
    lane = lax.broadcasted_iota(jnp.int32, (1, LANES), 1)
    lo = (lane % 16) < 8
    parts = []
    for i in range(x.shape[1] // LANES):
        xb = x[:, i * LANES:(i + 1) * LANES]
        parts.append(jnp.where(lo, pltpu.roll(xb, LANES - 8, 1), pltpu.roll(xb, 8, 1)))
    return x * cos + jnp.concatenate(parts, axis=-1) * sin_signed


def _mod_kernel(c_ref, w_ref, b_ref, o_ref):
    cv = c_ref[...]
    s = cv * _sigmoid(cv)
    o_ref[0] = _mm(s.astype(BF16), w_ref[0].astype(BF16)) + b_ref[0]


def _mod_call(cvec, w_ada, b_ada):
    tn = 1536
    n_out = N_MOD * D_MODEL
    return pl.pallas_call(
        _mod_kernel,
        grid=(DEPTH, n_out // tn),
        in_specs=[pl.BlockSpec((8, D_MODEL), lambda l, j: (0, 0)),
                  pl.BlockSpec((1, D_MODEL, tn), lambda l, j: (l, 0, j)),
                  pl.BlockSpec((1, 1, tn), lambda l, j: (l, 0, j))],
        out_specs=pl.BlockSpec((1, 8, tn), lambda l, j: (l, 0, j)),
        out_shape=jax.ShapeDtypeStruct((DEPTH, 8, n_out), F32),
        compiler_params=pltpu.CompilerParams(dimension_semantics=("arbitrary", "arbitrary")),
        name="mod",
    )(cvec, w_ada, b_ada.reshape(DEPTH, 1, n_out))


def _tile_rope_key(kr):
    return jnp.concatenate([pltpu.roll(kr, NOPE_A, 1)] * H_A, axis=-1)


def _cache_kernel(ckv_ref, kr_ref, wuk_ref, wuv_ref, gka_ref, k_ref, v_ref):
    ckv = ckv_ref[0, 0].astype(BF16)
    kpre = _mm(ckv, wuk_ref[0]) + _tile_rope_key(kr_ref[0, 0])
    k_ref[0, 0] = (_rms_heads128(kpre, QK_A) * gka_ref[0]).astype(BF16)
    v_ref[0, 0] = _mm(ckv, wuv_ref[0]).astype(BF16)


def _cache_call(cache_ckv, cache_kr_pad, P):
    bl = lambda b, l: (b, l, 0, 0)
    lay = lambda b, l: (l, 0, 0)
    return pl.pallas_call(
        _cache_kernel,
        grid=(DEC_BATCH, DEPTH),
        in_specs=[pl.BlockSpec((1, 1, PAST_LEN, KV_RANK), bl),
                  pl.BlockSpec((1, 1, PAST_LEN, LANES), bl),
                  pl.BlockSpec((1, KV_RANK, WA), lay),
                  pl.BlockSpec((1, KV_RANK, WVA), lay),
                  pl.BlockSpec((1, 1, WA), lay)],
        out_specs=[pl.BlockSpec((1, 1, PAST_LEN, WA), bl),
                   pl.BlockSpec((1, 1, PAST_LEN, WVA), bl)],
        out_shape=[jax.ShapeDtypeStruct((DEC_BATCH, DEPTH, PAST_LEN, WA), BF16),
                   jax.ShapeDtypeStruct((DEC_BATCH, DEPTH, PAST_LEN, WVA), BF16)],
        compiler_params=pltpu.CompilerParams(dimension_semantics=("arbitrary", "arbitrary")),
        name="cache_expand",
    )(cache_ckv, cache_kr_pad, P["wuk"], P["wuv"], P["gka"])


def _pre_kernel(latent, *refs):
    (x_ref, mod_ref, g1_ref, win_ref, gq_ref, gkv_ref, wuq_ref, wuk_ref, wuv_ref,
     gqa_ref, gka_ref, gdq_ref, gdk_ref, bd32_ref) = refs[:14]
    refs = refs[14:]
    if latent:
        cosa_ref, sina_ref, cosd_ref, sind_ref = refs[:4]
        refs = refs[4:]
    qa_ref, ka_ref, va_ref, qd_ref, kd_ref, vd_ref, z_ref, xbc_ref, dt_ref = refs[:9]
    refs = refs[9:]
    if not latent:
        ckvn_ref, krope_ref, kd32_ref, vd32_ref = refs

    x = x_ref[...]
    mod = mod_ref[0, 0]
    shift1 = mod[:, 0:D_MODEL]
    scale1 = mod[:, D_MODEL:2 * D_MODEL]
    h = (_rms(x) * g1_ref[0] * (1.0 + scale1) + shift1).astype(BF16)

    def proj(c0, c1):
        return _mm(h, win_ref[0, :, c0:c1])

    cqn = (_rms(proj(C_CQ, C_CKV)) * gq_ref[0]).astype(BF16)
    q = _rms_heads128(_mm(cqn, wuq_ref[0]), QK_A) * gqa_ref[0]
    if latent:
        q = _rope(q, cosa_ref[...], sina_ref[...])
    qa_ref[...] = q.astype(BF16)

    ckvn = _rms(proj(C_CKV, C_KR)) * gkv_ref[0]
    kr = proj(C_KR, C_DQ)
    ckvb = ckvn.astype(BF16)
    k = _rms_heads128(_mm(ckvb, wuk_ref[0]) + _tile_rope_key(kr), QK_A) * gka_ref[0]
    if latent:
        k = _rope(k, cosa_ref[...], sina_ref[...])
    ka_ref[...] = k.astype(BF16)
    va_ref[...] = _mm(ckvb, wuv_ref[0]).astype(BF16)

    bd32 = bd32_ref[...]
    qd = _rms_segments(proj(C_DQ, C_DK), bd32, DH_B) * gdq_ref[0]
    kd = _rms_segments(proj(C_DK, C_DV), bd32, DH_B) * gdk_ref[0]
    vd = proj(C_DV, C_Z)
    if latent:
        qd = _rope(qd, cosd_ref[...], sind_ref[...])
        kd = _rope(kd, cosd_ref[...], sind_ref[...])
    qd_ref[...] = qd.astype(BF16)
    kd_ref[...] = kd.astype(BF16)
    vd_ref[...] = vd.astype(BF16)

    z_ref[...] = proj(C_Z, C_XBC)
    xbc_ref[...] = proj(C_XBC, C_DT)
    dt_ref[...] = proj(C_DT, NP_IN)

    if not latent:
        ckvn_ref[...] = ckvn
        krope_ref[...] = kr[:, 0:ROPE_A]
        kd32_ref[...] = kd
        vd32_ref[...] = vd


def _pre_call(x2d, mod4, P, l, latent):
    T = x2d.shape[0]
    tm = TM_TOK
    tpb = DEC_SEQ // tm
    row = lambda i: (i, 0)
    lay = lambda i: (l, 0, 0)
    if latent:
        mod_map = lambda i: (l, i // tpb, 0, 0)
    else:
        mod_map = lambda i: (l, DEC_BATCH, 0, 0)
    full = lambda i: (0, 0)
    in_specs = [pl.BlockSpec((tm, D_MODEL), row),
                pl.BlockSpec((1, 1, 1, N_MOD * D_MODEL), mod_map),
                pl.BlockSpec((1, 1, D_MODEL), lay),
                pl.BlockSpec((1, D_MODEL, NP_IN), lay),
                pl.BlockSpec((1, 1, Q_RANK), lay),
                pl.BlockSpec((1, 1, KV_RANK), lay),
                pl.BlockSpec((1, Q_RANK, WA), lay),
                pl.BlockSpec((1, KV_RANK, WA), lay),
                pl.BlockSpec((1, KV_RANK, WVA), lay),
                pl.BlockSpec((1, 1, WA), lay),
                pl.BlockSpec((1, 1, WA), lay),
                pl.BlockSpec((1, 1, WD), lay),
                pl.BlockSpec((1, 1, WD), lay),
                pl.BlockSpec((WD, WD), full)]
    args = [x2d, mod4, P["g1"], P["win"], P["gq"], P["gkv"], P["wuq"], P["wuk"], P["wuv"],
            P["gqa"], P["gka"], P["gdq"], P["gdk"], P["bd32"]]
    if latent:
        pos = lambda i: (i % tpb, 0)
        in_specs += [pl.BlockSpec((tm, WA), pos), pl.BlockSpec((tm, WA), pos),
                     pl.BlockSpec((tm, WD), pos), pl.BlockSpec((tm, WD), pos)]
        args += [P["cosa"], P["sina"], P["cosd"], P["sind"]]
    widths = [(WA, BF16), (WA, BF16), (WVA, BF16), (WD, BF16), (WD, BF16), (WD, BF16),
              (D_INNER, F32), (CONV_DIM, F32), (LANES, F32)]
    if not latent:
        widths += [(KV_RANK, F32), (ROPE_A, F32), (WD, F32), (WD, F32)]
    out_specs = [pl.BlockSpec((tm, w), row) for w, _ in widths]
    out_shape = [jax.ShapeDtypeStruct((T, w), dt) for w, dt in widths]
    return pl.pallas_call(
        functools.partial(_pre_kernel, latent),
        grid=(T // tm,),
        in_specs=in_specs, out_specs=out_specs, out_shape=out_shape,
        compiler_params=pltpu.CompilerParams(dimension_semantics=("arbitrary",),
                                             vmem_limit_bytes=V7X_VMEM_LIMIT),
        name="pre_lat" if latent else "pre_ctx",
    )(*args)


def _softmax(q, key_refs, c0, c1):
    s = [_mm_nt(q, kr[:, c0:c1]) for kr in key_refs]
    m = s[0].max(axis=-1, keepdims=True)
    for si in s[1:]:
        m = jnp.maximum(m, si.max(axis=-1, keepdims=True))
    p = [jnp.exp(si - m) for si in s]
    den = p[0].sum(axis=-1, keepdims=True)
    for pi in p[1:]:
        den = den + pi.sum(axis=-1, keepdims=True)
    return p, den


def _pv(p, val_refs):
    pv = _mm(p[0].astype(BF16), val_refs[0][...])
    for pi, vr in zip(p[1:], val_refs[1:]):
        pv = pv + _mm(pi.astype(BF16), vr[...])
    return pv


def _attn_kernel(has_cache, lam_init, *refs):
    qa_ref, ka_ref, va_ref, qd_ref, kd_ref, vd_ref = refs[:6]
    refs = refs[6:]
    if has_cache:
        kca_ref, vca_ref, kcd_ref, vcd_ref = refs[:4]
        refs = refs[4:]
    lq1_ref, lk1_ref, lq2_ref, lk2_ref, gsub_ref, bd64_ref, o_ref = refs

    ka_refs = [ka_ref.at[0]]
    va_refs = [va_ref.at[0]]
    kd_refs = [kd_ref.at[0]]
    vd_refs = [vd_ref.at[0]]
    if has_cache:
        ka_refs = [kca_ref.at[0, 0]] + ka_refs
        va_refs = [vca_ref.at[0, 0]] + va_refs
        kd_refs = [kcd_ref.at[0, 0]] + kd_refs
        vd_refs = [vcd_ref.at[0, 0]] + vd_refs

    lane256 = lax.broadcasted_iota(jnp.int32, (1, WD), 1)
    lane128 = lax.broadcasted_iota(jnp.int32, (1, LANES), 1)

    qa = qa_ref[0]
    oa = None
    for hh in range(H_A):
        p, den = _softmax(qa[:, hh * LANES:(hh + 1) * LANES], ka_refs, hh * LANES, (hh + 1) * LANES)
        contrib = jnp.where(lane256 // V_A == hh, _pv(p, va_refs) * (1.0 / den), 0.0)
        oa = contrib if oa is None else oa + contrib

    lam = (jnp.exp(jnp.sum(lq1_ref[0] * lk1_ref[0], axis=-1, keepdims=True))
           - jnp.exp(jnp.sum(lq2_ref[0] * lk2_ref[0], axis=-1, keepdims=True)) + lam_init)
    qd = qd_ref[0]
    od = None
    for hh in range(H_B):
        blk = hh // 2
        qblk = qd[:, blk * LANES:(blk + 1) * LANES]
        maps = []
        for mm_i in range(2):
            seg = (hh % 2) * 2 + mm_i
            qm = jnp.where(lane128 // DH_B == seg, qblk, jnp.zeros_like(qblk))
            p, den = _softmax(qm, kd_refs, blk * LANES, (blk + 1) * LANES)
            maps.append(_pv(p, vd_refs) * (1.0 / den))
        contrib = jnp.where(lane256 // (2 * DH_B) == hh, maps[0] - lam * maps[1], 0.0)
        od = contrib if od is None else od + contrib
    od = _rms_segments(od, bd64_ref[...], 2 * DH_B) * (gsub_ref[0] * (1.0 - lam_init))
    o_ref[0] = jnp.concatenate([oa, od], axis=-1).astype(BF16)


def _attn_call(qa, ka, va, qd, kd, vd, caches, P, l):
    B, L = qa.shape[0], qa.shape[1]
    has_cache = caches is not None
    lam_init = 0.8 - 0.6 * math.exp(-0.3 * l)
    qmap = lambda b, i: (b, i, 0)
    kmap = lambda b, i: (b, 0, 0)
    lay = lambda b, i: (l, 0, 0)
    in_specs = [pl.BlockSpec((1, TQ, WA), qmap), pl.BlockSpec((1, L, WA), kmap),
                pl.BlockSpec((1, L, WVA), kmap), pl.BlockSpec((1, TQ, WD), qmap),
                pl.BlockSpec((1, L, WD), kmap), pl.BlockSpec((1, L, WD), kmap)]
    args = [qa, ka, va, qd, kd, vd]
    if has_cache:
        cmap = lambda b, i: (b, l, 0, 0)
        in_specs += [pl.BlockSpec((1, 1, PAST_LEN, WA), cmap), pl.BlockSpec((1, 1, PAST_LEN, WVA), cmap),
                     pl.BlockSpec((1, 1, PAST_LEN, WD), cmap), pl.BlockSpec((1, 1, PAST_LEN, WD), cmap)]
        args += list(caches)
    in_specs += [pl.BlockSpec((1, 1, DH_B), lay)] * 4
    in_specs += [pl.BlockSpec((1, 1, WD), lay), pl.BlockSpec((WD, WD), lambda b, i: (0, 0))]
    args += [P["lq1"], P["lk1"], P["lq2"], P["lk2"], P["gsub"], P["bd64"]]
    return pl.pallas_call(
        functools.partial(_attn_kernel, has_cache, lam_init),
        grid=(B, L // TQ),
        in_specs=in_specs,
        out_specs=pl.BlockSpec((1, TQ, WVA + WD), qmap),
        out_shape=jax.ShapeDtypeStruct((B, L, WVA + WD), BF16),
        compiler_params=pltpu.CompilerParams(dimension_semantics=("arbitrary", "arbitrary"),
                                             vmem_limit_bytes=V7X_VMEM_LIMIT),
        name="attn_lat" if has_cache else "attn_ctx",
    )(*args)


def _ssd_kernel(L, has_h0, *refs):
    z_ref, xbc_ref, dt_ref = refs[:3]
    refs = refs[3:]
    if has_h0:
        s0_ref = refs[0]
        refs = refs[1:]
    cw_ref, cb_ref, alog_ref, dtb_ref, dexp_ref, g_ref, tri_ref = refs[:7]
    refs = refs[7:]
    o_ref = refs[0]
    refs = refs[1:]
    if not has_h0:
        st_ref = refs[0]
        refs = refs[1:]
    xpad, xc, yacc, cum_s, bt_s, cumt_s, dtt_s, wt_s, el_s, s_scr = refs
    nc = L // CHUNK
    halo = 8
    nd = 2 * H_C

    xpad[0:halo, :] = jnp.zeros((halo, CONV_DIM), F32)
    xpad[L + halo:L + 2 * halo, :] = jnp.zeros((halo, CONV_DIM), F32)
    xpad[halo:L + halo, :] = xbc_ref[0]
    cw = cw_ref[0]
    cb = cb_ref[0]
    dtb = dtb_ref[0]
    a_neg = -jnp.exp(alog_ref[0])
    lane128 = lax.broadcasted_iota(jnp.int32, (1, LANES), 1)
    fwd_lane = lane128 < H_C
    tri_f = tri_ref[0]
    tri_b = tri_ref[1]

    def prep_body(c, carry):
        base = pl.multiple_of(c * CHUNK, CHUNK)
        win = xpad[pl.ds(base, CHUNK + 2 * halo), :]
        acc = cb
        for k in range(CONV_K):
            off = halo - CONV_K // 2 + k
            acc = acc + win[off:off + CHUNK, :] * cw[k:k + 1, :]
        act = acc * _sigmoid(acc)
        xc[pl.ds(base, CHUNK), :] = act
        bt_s[pl.ds(base, CHUNK), :] = act[:, D_INNER:D_INNER + LANES].T
        dtr = dt_ref[0, pl.ds(base, CHUNK), :] + dtb
        dtc = jnp.maximum(dtr, 0.0) + jnp.log(1.0 + jnp.exp(-jnp.abs(dtr)))
        hi, mid, lo = _split3(dtc * a_neg)
        cum_f = _mm(tri_f, hi) + _mm(tri_f, mid) + _mm(tri_f, lo)
        cum_b = _mm(tri_b, hi) + _mm(tri_b, mid) + _mm(tri_b, lo)
        cum = jnp.where(fwd_lane, cum_f, cum_b)
        last = jnp.where(fwd_lane, cum[CHUNK - 1:CHUNK, :], cum[0:1, :])
        cum_s[pl.ds(base, CHUNK), :] = cum
        rows = pl.ds(pl.multiple_of(c * nd, nd), nd)
        cumt_s[rows, :] = cum.T[0:nd, :]
        dtt_s[rows, :] = dtc.T[0:nd, :]
        wt_s[rows, :] = (jnp.exp(last - cum) * dtc).T[0:nd, :]
        el_s[pl.ds(pl.multiple_of(c * 8, 8), 8), :] = jnp.broadcast_to(jnp.exp(last), (8, LANES))
        return carry

    lax.fori_loop(0, nc, prep_body, 0)

    row_i = lax.broadcasted_iota(jnp.int32, (CHUNK, CHUNK), 0)
    col_j = lax.broadcasted_iota(jnp.int32, (CHUNK, CHUNK), 1)
    blockmask = (lax.broadcasted_iota(jnp.int32, (2 * N_C, D_INNER), 0) // N_C
                 == lax.broadcasted_iota(jnp.int32, (2 * N_C, D_INNER), 1) // (D_INNER // G_C))
    dexp = dexp_ref[0]
    gnorm = g_ref[0]

    for d in range(2):
        if has_h0:
            s_scr[...] = s0_ref[0, 0, d]
        else:
            s_scr[...] = jnp.zeros((2 * N_C, D_INNER), F32)
        causal = (row_i >= col_j) if d == 0 else (col_j >= row_i)

        def per_head_lanes(cols, d=d):
            parts = []
            for pair in range(H_C // 2):
                i0 = d * H_C + 2 * pair
                parts.append(jnp.where(lane128 < P_C, cols[:, i0:i0 + 1], cols[:, i0 + 1:i0 + 2]))
            return jnp.concatenate(parts, axis=-1)

        def chunk_body(step, carry, d=d, causal=causal, per_head_lanes=per_head_lanes):
            c = step if d == 0 else nc - 1 - step
            base = pl.multiple_of(c * CHUNK, CHUNK)
            rows = pl.ds(pl.multiple_of(c * nd, nd), nd)
            xs = xc[pl.ds(base, CHUNK), 0:D_INNER]
            c_c = xc[pl.ds(base, CHUNK), D_INNER + LANES:D_INNER + 2 * LANES]
            cum = cum_s[pl.ds(base, CHUNK), :]
            cum_t = cumt_s[rows, :]
            dt_t = dtt_s[rows, :]
            w_t = wt_s[rows, :]
            b_t = bt_s[pl.ds(base, CHUNK), :]
            cb16 = c_c.astype(BF16)
            bt16 = b_t.astype(BF16)
            cbs = [_mm(jnp.where(lane128 // N_C == g, cb16, jnp.zeros_like(cb16)), bt16)
                   for g in range(G_C)]
            sv = s_scr[...]

            def blockdiag(v):
                return jnp.concatenate([jnp.where(lane128 < P_C, v, 0.0),
                                        jnp.where(lane128 < P_C, 0.0, v)], axis=0).astype(BF16)

            ys, upds = [], []
            for pair in range(H_C // 2):
                xbd = blockdiag(xs[:, pair * LANES:(pair + 1) * LANES])
                sbd = blockdiag(sv[:, pair * LANES:(pair + 1) * LANES])
                ms, es, ws = [], [], []
                for hh in (2 * pair, 2 * pair + 1):
                    idx = d * H_C + hh
                    col = jnp.broadcast_to(cum[:, idx:idx + 1], (CHUNK, CHUNK))
                    dec = jnp.where(causal, jnp.exp(col - cum_t[idx:idx + 1, :]), 0.0)
                    ms.append((cbs[hh // (H_C // G_C)] * dec * dt_t[idx:idx + 1, :]).astype(BF16))
                    es.append((jnp.exp(col) * c_c).astype(BF16))
                    ws.append((b_t * w_t[idx:idx + 1, :]).astype(BF16))
                ys.append(_mm(jnp.concatenate(ms + es, axis=-1), jnp.concatenate([xbd, sbd], axis=0)))
                upds.append(_mm(jnp.concatenate(ws, axis=-1), xbd))
            y = jnp.concatenate(ys, axis=-1)
            cd = per_head_lanes(el_s[pl.ds(pl.multiple_of(c * 8, 8), 1), :])
            s_scr[...] = sv * cd + jnp.where(blockmask, jnp.concatenate(upds, axis=-1), 0.0)
            if d == 0:
                yacc[pl.ds(base, CHUNK), :] = y
            else:
                zc = z_ref[0, pl.ds(base, CHUNK), :]
                yt = (yacc[pl.ds(base, CHUNK), :] + y + dexp * xs) * (zc * _sigmoid(zc))
                o_ref[0, pl.ds(base, CHUNK), :] = (_rms(yt) * gnorm).astype(BF16)
            return carry

        lax.fori_loop(0, nc, chunk_body, 0, unroll=2)
        if not has_h0:
            st = s_scr[...].T
            st_ref[0, d] = (st + pltpu.roll(st, N_C, 1))[:, 0:N_C]


def _ssd_call(z, xbc, dt, s0, P, l):
    B, L = z.shape[0], z.shape[1]
    has_h0 = s0 is not None
    bmap = lambda b: (b, 0, 0)
    lay = lambda b: (l, 0, 0)
    in_specs = [pl.BlockSpec((1, L, D_INNER), bmap), pl.BlockSpec((1, L, CONV_DIM), bmap),
                pl.BlockSpec((1, L, LANES), bmap)]
    args = [z, xbc, dt]
    if has_h0:
        in_specs.append(pl.BlockSpec((1, 1, 2, 2 * N_C, D_INNER), lambda b: (b, l, 0, 0, 0)))
        args.append(s0)
    in_specs += [pl.BlockSpec((1, 8, CONV_DIM), lay), pl.BlockSpec((1, 1, CONV_DIM), lay),
                 pl.BlockSpec((1, 1, LANES), lay), pl.BlockSpec((1, 1, LANES), lay),
                 pl.BlockSpec((1, 1, D_INNER), lay), pl.BlockSpec((1, 1, D_INNER), lay),
                 pl.BlockSpec((2, CHUNK, CHUNK), lambda b: (0, 0, 0))]
    args += [P["cw"], P["cb"], P["alog"], P["dtb"], P["dexp"], P["gssm"], P["tri"]]
    out_specs = [pl.BlockSpec((1, L, D_INNER), bmap)]
    out_shape = [jax.ShapeDtypeStruct((B, L, D_INNER), BF16)]
    if not has_h0:
        out_specs.append(pl.BlockSpec((1, 2, D_INNER, N_C), lambda b: (b, 0, 0, 0)))
        out_shape.append(jax.ShapeDtypeStruct((B, 2, D_INNER, N_C), F32))
    return pl.pallas_call(
        functools.partial(_ssd_kernel, L, has_h0),
        grid=(B,),
        in_specs=in_specs, out_specs=out_specs, out_shape=out_shape,
        scratch_shapes=[pltpu.VMEM((L + 16, CONV_DIM), F32), pltpu.VMEM((L, CONV_DIM), F32),
                        pltpu.VMEM((L, D_INNER), F32), pltpu.VMEM((L, LANES), F32),
                        pltpu.VMEM((L, LANES), F32)]
        + [pltpu.VMEM((L // CHUNK * 2 * H_C, LANES), F32)] * 3
        + [pltpu.VMEM((L // CHUNK * 8, LANES), F32), pltpu.VMEM((2 * N_C, D_INNER), F32)],
        compiler_params=pltpu.CompilerParams(dimension_semantics=("arbitrary",),
                                             vmem_limit_bytes=V7X_VMEM_LIMIT),
        name="ssd_lat" if has_h0 else "ssd_ctx",
    )(*args)


def _post_kernel(x_ref, oat_ref, oc_ref, mod_ref, wo_ref, g2_ref, w1_ref, w2_ref, o_ref):
    mod = mod_ref[0, 0]
    gate1 = mod[:, 2 * D_MODEL:3 * D_MODEL]
    shift2 = mod[:, 3 * D_MODEL:4 * D_MODEL]
    scale2 = mod[:, 4 * D_MODEL:5 * D_MODEL]
    gate2 = mod[:, 5 * D_MODEL:6 * D_MODEL]
    n_att = WVA + WD
    mix = _mm(oat_ref[...], wo_ref[0, 0:n_att, :]) + _mm(oc_ref[...], wo_ref[0, n_att:n_att + D_INNER, :])
    x1 = x_ref[...] + gate1 * mix
    h2 = (_rms(x1) * g2_ref[0] * (1.0 + scale2) + shift2).astype(BF16)
    ff = None
    for c in range(D_FF // FF_CHUNK):
        u = jnp.maximum(_mm(h2, w1_ref[0, :, c * FF_CHUNK:(c + 1) * FF_CHUNK]), 0.0)
        part = _mm((u * u).astype(BF16), w2_ref[0, c * FF_CHUNK:(c + 1) * FF_CHUNK, :])
        ff = part if ff is None else ff + part
    o_ref[...] = x1 + gate2 * ff


def _post_call(x2d, oat, oc, mod4, P, l, latent):
    T = x2d.shape[0]
    tm = TM_TOK
    tpb = DEC_SEQ // tm
    row = lambda i: (i, 0)
    lay = lambda i: (l, 0, 0)
    if latent:
        mod_map = lambda i: (l, i // tpb, 0, 0)
    else:
        mod_map = lambda i: (l, DEC_BATCH, 0, 0)
    single = pl.Buffered(1)
    return pl.pallas_call(
        _post_kernel,
        grid=(T // tm,),
        in_specs=[pl.BlockSpec((tm, D_MODEL), row),
                  pl.BlockSpec((tm, WVA + WD), row),
                  pl.BlockSpec((tm, D_INNER), row),
                  pl.BlockSpec((1, 1, 1, N_MOD * D_MODEL), mod_map),
                  pl.BlockSpec((1, D_MODEL, D_MODEL), lay, pipeline_mode=single),
                  pl.BlockSpec((1, 1, D_MODEL), lay),
                  pl.BlockSpec((1, D_MODEL, D_FF), lay, pipeline_mode=single),
                  pl.BlockSpec((1, D_FF, D_MODEL), lay, pipeline_mode=single)],
        out_specs=pl.BlockSpec((tm, D_MODEL), row),
        out_shape=jax.ShapeDtypeStruct((T, D_MODEL), F32),
        compiler_params=pltpu.CompilerParams(dimension_semantics=("arbitrary",),
                                             vmem_limit_bytes=V7X_VMEM_LIMIT),
        name="post_lat" if latent else "post_ctx",
    )(x2d, oat, oc, mod4, P["wout"], P["g2"], P["wff1"], P["wff2"])


def _rope_tables():
    t = np.arange(DEC_SEQ)
    rows = (t // GRID_W).astype(np.float32)
    cols = (t % GRID_W).astype(np.float32)
    half = ROPE_A // 2
    freqs = jnp.asarray(ROPE_BASE, F32) ** (-jnp.arange(0, half, 2, dtype=F32) / half)
    ang_r = jnp.asarray(rows)[:, None] * freqs
    ang_c = jnp.asarray(cols)[:, None] * freqs
    cos32 = jnp.concatenate([jnp.cos(ang_r), jnp.cos(ang_r), jnp.cos(ang_c), jnp.cos(ang_c)], axis=-1)
    sin32 = jnp.concatenate([-jnp.sin(ang_r), jnp.sin(ang_r), -jnp.sin(ang_c), jnp.sin(ang_c)], axis=-1)
    ones = jnp.ones((DEC_SEQ, NOPE_A), F32)
    zeros = jnp.zeros((DEC_SEQ, NOPE_A), F32)
    pad1 = jnp.ones((DEC_SEQ, LANES - QK_A), F32)
    pad0 = jnp.zeros((DEC_SEQ, LANES - QK_A), F32)
    cosa = jnp.tile(jnp.concatenate([ones, cos32, pad1], axis=-1), (1, H_A))
    sina = jnp.tile(jnp.concatenate([zeros, sin32, pad0], axis=-1), (1, H_A))
    cosd = jnp.tile(cos32, (1, WD // ROPE_A))
    sind = jnp.tile(sin32, (1, WD // ROPE_A))
    return cosa, sina, cosd, sind


def _constants():
    lane = np.arange(WD)
    bd32 = (lane[:, None] // DH_B == lane[None, :] // DH_B).astype(np.float32)
    bd64 = (lane[:, None] // (2 * DH_B) == lane[None, :] // (2 * DH_B)).astype(np.float32)
    i = np.arange(CHUNK)
    tri = np.stack([(i[None, :] <= i[:, None]), (i[None, :] >= i[:, None])]).astype(np.float32)
    return dict(bd32=jnp.asarray(bd32, BF16), bd64=jnp.asarray(bd64, BF16),
                tri=jnp.asarray(tri, BF16))


def _pad_last(a, n):
    return jnp.pad(a, [(0, 0)] * (a.ndim - 1) + [(0, n - a.shape[-1])])


def _prep_params(norm1_g, norm2_g, w_in, w_out, mla_q_norm_g, mla_kv_norm_g, w_uq, w_ukv, mla_qk_norm_q,
                 mla_qk_norm_k, diff_q_norm_g, diff_k_norm_g, diff_lq1, diff_lk1, diff_lq2, diff_lk2,
                 diff_subln_g, ssm_conv_w, ssm_conv_b, ssm_A_log, ssm_dt_bias, ssm_D, ssm_norm_g, w_ff1, w_ff2):
    P = _constants()
    offs = np.cumsum((Q_RANK, KV_RANK, ROPE_A, WD, WD, WD, D_INNER, CONV_DIM, 2 * H_C))
    o = [0] + offs.tolist()
    sl = lambda i: w_in[:, :, o[i]:o[i + 1]]
    win = jnp.concatenate([sl(0), sl(1), _pad_last(sl(2), LANES), sl(3), sl(4), sl(5), sl(6), sl(7),
                           _pad_last(sl(8), LANES)], axis=-1)
    P["win"] = win.astype(BF16)
    P["wuq"] = _pad_last(w_uq.reshape(DEPTH, Q_RANK, H_A, QK_A), LANES).reshape(DEPTH, Q_RANK, WA).astype(BF16)
    wkv = w_ukv.reshape(DEPTH, KV_RANK, H_A, NOPE_A + V_A)
    P["wuk"] = _pad_last(wkv[..., :NOPE_A], LANES).reshape(DEPTH, KV_RANK, WA).astype(BF16)
    P["wuv"] = wkv[..., NOPE_A:].reshape(DEPTH, KV_RANK, WVA).astype(BF16)
    P["wout"] = w_out.astype(BF16)
    P["wff1"] = w_ff1.astype(BF16)
    P["wff2"] = w_ff2.astype(BF16)
    vec = lambda a: a.reshape(DEPTH, 1, -1)
    P["g1"] = vec(norm1_g)
    P["g2"] = vec(norm2_g)
    P["gq"] = vec(mla_q_norm_g)
    P["gkv"] = vec(mla_kv_norm_g)
    P["gqa"] = vec(jnp.tile(_pad_last(mla_qk_norm_q, LANES), (1, H_A))) * (QK_A ** -0.5)
    P["gka"] = vec(jnp.tile(_pad_last(mla_qk_norm_k, LANES), (1, H_A)))
    P["gdq"] = vec(jnp.tile(diff_q_norm_g, (1, WD // DH_B))) * (DH_B ** -0.5)
    P["gdk"] = vec(jnp.tile(diff_k_norm_g, (1, WD // DH_B)))
    P["gsub"] = vec(jnp.tile(diff_subln_g, (1, H_B)))
    P["lq1"], P["lk1"], P["lq2"], P["lk2"] = vec(diff_lq1), vec(diff_lk1), vec(diff_lq2), vec(diff_lk2)
    P["cw"] = jnp.pad(ssm_conv_w, ((0, 0), (0, 8 - CONV_K), (0, 0)))
    P["cb"] = vec(ssm_conv_b)
    P["alog"] = vec(_pad_last(ssm_A_log.reshape(DEPTH, 2 * H_C), LANES))
    P["dtb"] = vec(_pad_last(ssm_dt_bias.reshape(DEPTH, 2 * H_C), LANES))
    P["dexp"] = vec(jnp.repeat(ssm_D, P_C, axis=-1))
    P["gssm"] = vec(ssm_norm_g)
    P["cosa"], P["sina"], P["cosd"], P["sind"] = _rope_tables()
    return P


def kernel(x_prompt, x_sample, cache_mla_ckv, cache_mla_krope, cache_diff_k, cache_diff_v, state_ssm, c, c_ctx, norm1_g, norm2_g, w_ada, b_ada, w_in, w_out, mla_q_norm_g, mla_kv_norm_g, w_uq, w_ukv, mla_qk_norm_q, mla_qk_norm_k, diff_q_norm_g, diff_k_norm_g, diff_lq1, diff_lk1, diff_lq2, diff_lk2, diff_subln_g, ssm_conv_w, ssm_conv_b, ssm_A_log, ssm_dt_bias, ssm_D, ssm_norm_g, w_ff1, w_ff2):
    P = _prep_params(norm1_g, norm2_g, w_in, w_out, mla_q_norm_g, mla_kv_norm_g, w_uq, w_ukv, mla_qk_norm_q,
                     mla_qk_norm_k, diff_q_norm_g, diff_k_norm_g, diff_lq1, diff_lk1, diff_lq2, diff_lk2,
                     diff_subln_g, ssm_conv_w, ssm_conv_b, ssm_A_log, ssm_dt_bias, ssm_D, ssm_norm_g,
                     w_ff1, w_ff2)

    cvec = jnp.concatenate([c, c_ctx[None, :], jnp.zeros((8 - DEC_BATCH - 1, D_MODEL), F32)], axis=0)
    mod4 = _mod_call(cvec, w_ada, b_ada).reshape(DEPTH, 8, 1, N_MOD * D_MODEL)

    kca, vca = _cache_call(cache_mla_ckv, _pad_last(cache_mla_krope, LANES), P)
    kcd = cache_diff_k.reshape(DEC_BATCH, DEPTH, PAST_LEN, WD).astype(BF16)
    vcd = cache_diff_v.reshape(DEC_BATCH, DEPTH, PAST_LEN, WD).astype(BF16)
    st = jnp.transpose(state_ssm, (0, 1, 2, 5, 3, 4)).reshape(DEC_BATCH, DEPTH, 2, N_C, D_INNER)
    half = D_INNER // G_C
    zero = jnp.zeros_like(st[..., :half])
    s0 = jnp.concatenate([jnp.concatenate([st[..., :half], zero], axis=-1),
                          jnp.concatenate([zero, st[..., half:]], axis=-1)], axis=-2)

    xp = x_prompt.reshape(BATCH * SEQ, D_MODEL)
    xs = x_sample.reshape(DEC_BATCH * DEC_SEQ, D_MODEL)
    ckv_l, krope_l, kd_l, vd_l, st_l = [], [], [], [], []
    for l in range(DEPTH):
        qa, ka, va, qd, kd, vd, z, xbc, dt, ckvn, krope, kd32, vd32 = _pre_call(xp, mod4, P, l, False)
        r3 = lambda a: a.reshape(BATCH, SEQ, a.shape[-1])
        oat = _attn_call(r3(qa), r3(ka), r3(va), r3(qd), r3(kd), r3(vd), None, P, l)
        oc, st_new = _ssd_call(r3(z), r3(xbc), r3(dt), None, P, l)
        xp = _post_call(xp, oat.reshape(BATCH * SEQ, -1), oc.reshape(BATCH * SEQ, -1), mod4, P, l, False)
        ckv_l.append(ckvn.reshape(BATCH, SEQ, KV_RANK))
        krope_l.append(krope.reshape(BATCH, SEQ, ROPE_A))
        kd_l.append(kd32.reshape(BATCH, SEQ, H_B, 2 * DH_B))
        vd_l.append(vd32.reshape(BATCH, SEQ, H_B, 2 * DH_B))
        st_l.append(st_new.reshape(BATCH, 2, H_C, P_C, N_C))
        qa, ka, va, qd, kd, vd, z, xbc, dt = _pre_call(xs, mod4, P, l, True)
        r3 = lambda a: a.reshape(DEC_BATCH, DEC_SEQ, a.shape[-1])
        oat = _attn_call(r3(qa), r3(ka), r3(va), r3(qd), r3(kd), r3(vd), (kca, vca, kcd, vcd), P, l)
        oc, = _ssd_call(r3(z), r3(xbc), r3(dt), s0, P, l)
        xs = _post_call(xs, oat.reshape(DEC_BATCH * DEC_SEQ, -1), oc.reshape(DEC_BATCH * DEC_SEQ, -1),
                        mod4, P, l, True)
    return (xp.reshape(BATCH, SEQ, D_MODEL), xs.reshape(DEC_BATCH, DEC_SEQ, D_MODEL),
            jnp.stack(ckv_l, axis=1), jnp.stack(krope_l, axis=1), jnp.stack(kd_l, axis=1),
            jnp.stack(vd_l, axis=1), jnp.stack(st_l, axis=1))
```

```python
import functools
import math

import numpy as np
import jax
import jax.numpy as jnp
from jax import lax
from jax.experimental import pallas as pl
from jax.experimental.pallas import tpu as pltpu

D_MODEL = 1024
BATCH = 16
SEQ = 256
DEPTH = 4
DEC_BATCH = 4
DEC_SEQ = 2048
PAST_LEN = 256
GRID_W = 64
ROPE_BASE = 10000.0
EPS = 1e-6
CHUNK = 128
D_FF = 4 * D_MODEL
N_MOD = 6
H_A = 4
NOPE_A = 64
ROPE_A = 32
V_A = 64
Q_RANK = 256
KV_RANK = 128
H_B = 4
DH_B = 32
H_C = 8
P_C = 64
N_C = 64
G_C = 2
CONV_K = 5
D_INNER = H_C * P_C
CONV_DIM = D_INNER + 2 * G_C * N_C

F32 = jnp.float32
BF16 = jnp.bfloat16

LANES = 128
QK_A = NOPE_A + ROPE_A
WA = H_A * LANES
WD = H_B * 2 * DH_B
WVA = H_A * V_A
C_CQ, C_CKV, C_KR, C_DQ, C_DK, C_DV, C_Z, C_XBC, C_DT = 0, 256, 384, 512, 768, 1024, 1280, 1792, 2560
NP_IN = 2688
V7X_VMEM_LIMIT = 56 * 1024 * 1024
TM_TOK = 512
TQ = 512
FF_CHUNK = 1024
CONV_COLS = 256


def _mm(a, b):
    return jnp.dot(a, b, preferred_element_type=F32)


def _mm_nt(a, b):
    return lax.dot_general(a, b, (((1,), (1,)), ((), ())), preferred_element_type=F32)


def _split3(x):
    hi = x.astype(BF16)
    r = x - hi.astype(F32)
    mid = r.astype(BF16)
    lo = (r - mid.astype(F32)).astype(BF16)
    return hi, mid, lo


def _sigmoid(x):
    return 1.0 / (1.0 + jnp.exp(-x))


def _rms(x, n=None):
    n = x.shape[-1] if n is None else n
    return x * lax.rsqrt(jnp.sum(x * x, axis=-1, keepdims=True) * (1.0 / n) + EPS)


def _rms_heads128(x, n_real):
    parts = [_rms(x[:, i * LANES:(i + 1) * LANES], n_real) for i in range(x.shape[1] // LANES)]
    return jnp.concatenate(parts, axis=-1)


def _rms_segments(x, bd, seg):
    sq = x * x
    hi = sq.astype(BF16)
    lo = (sq - hi.astype(F32)).astype(BF16)
    ssq = _mm(hi, bd) + _mm(lo, bd)
    return x * lax.rsqrt(ssq * (1.0 / seg) + EPS)


def _rope(x, cos, sin_signed):
    lane = lax.broadcasted_iota(jnp.int32, (1, LANES), 1)
    lo = (lane % 16) < 8
    parts = []
    for i in range(x.shape[1] // LANES):
        xb = x[:, i * LANES:(i + 1) * LANES]
        parts.append(jnp.where(lo, pltpu.roll(xb, LANES - 8, 1), pltpu.roll(xb, 8, 1)))
    return x * cos + jnp.concatenate(parts, axis=-1) * sin_signed


def _mod_kernel(c_ref, w_ref, b_ref, o_ref):
    cv = c_ref[...]
    s = cv * _sigmoid(cv)
    o_ref[0] = _mm(s.astype(BF16), w_ref[0].astype(BF16)) + b_ref[0]


def _mod_call(cvec, w_ada, b_ada):
    tn = 1536
    n_out = N_MOD * D_MODEL
    return pl.pallas_call(
        _mod_kernel,
        grid=(DEPTH, n_out // tn),
        in_specs=[pl.BlockSpec((8, D_MODEL), lambda l, j: (0, 0)),
                  pl.BlockSpec((1, D_MODEL, tn), lambda l, j: (l, 0, j)),
                  pl.BlockSpec((1, 1, tn), lambda l, j: (l, 0, j))],
        out_specs=pl.BlockSpec((1, 8, tn), lambda l, j: (l, 0, j)),
        out_shape=jax.ShapeDtypeStruct((DEPTH, 8, n_out), F32),
        compiler_params=pltpu.CompilerParams(dimension_semantics=("arbitrary", "arbitrary")),
        name="mod",
    )(cvec, w_ada, b_ada.reshape(DEPTH, 1, n_out))


def _tile_rope_key(kr):
    return jnp.concatenate([pltpu.roll(kr, NOPE_A, 1)] * H_A, axis=-1)


def _cache_kernel(ckv_ref, kr_ref, wuk_ref, wuv_ref, gka_ref, k_ref, v_ref):
    ckv = ckv_ref[0, 0].astype(BF16)
    kpre = _mm(ckv, wuk_ref[0]) + _tile_rope_key(kr_ref[0, 0])
    k_ref[0, 0] = (_rms_heads128(kpre, QK_A) * gka_ref[0]).astype(BF16)
    v_ref[0, 0] = _mm(ckv, wuv_ref[0]).astype(BF16)


def _cache_call(cache_ckv, cache_kr_pad, P):
    bl = lambda b, l: (b, l, 0, 0)
    lay = lambda b, l: (l, 0, 0)
    return pl.pallas_call(
        _cache_kernel,
        grid=(DEC_BATCH, DEPTH),
        in_specs=[pl.BlockSpec((1, 1, PAST_LEN, KV_RANK), bl),
                  pl.BlockSpec((1, 1, PAST_LEN, LANES), bl),
                  pl.BlockSpec((1, KV_RANK, WA), lay),
                  pl.BlockSpec((1, KV_RANK, WVA), lay),
                  pl.BlockSpec((1, 1, WA), lay)],
        out_specs=[pl.BlockSpec((1, 1, PAST_LEN, WA), bl),
                   pl.BlockSpec((1, 1, PAST_LEN, WVA), bl)],
        out_shape=[jax.ShapeDtypeStruct((DEC_BATCH, DEPTH, PAST_LEN, WA), BF16),
                   jax.ShapeDtypeStruct((DEC_BATCH, DEPTH, PAST_LEN, WVA), BF16)],
        compiler_params=pltpu.CompilerParams(dimension_semantics=("arbitrary", "arbitrary")),
        name="cache_expand",
    )(cache_ckv, cache_kr_pad, P["wuk"], P["wuv"], P["gka"])


def _pre_kernel(latent, *refs):
    (x_ref, mod_ref, g1_ref, win_ref, gq_ref, gkv_ref, wuq_ref, wuk_ref, wuv_ref,
     gqa_ref, gka_ref, gdq_ref, gdk_ref, bd32_ref) = refs[:14]
    refs = refs[14:]
    if latent:
        cosa_ref, sina_ref, cosd_ref, sind_ref = refs[:4]
        refs = refs[4:]
    qa_ref, ka_ref, va_ref, qd_ref, kd_ref, vd_ref, z_ref, xbc_ref, dt_ref = refs[:9]
    refs = refs[9:]
    if not latent:
        ckvn_ref, krope_ref, kd32_ref, vd32_ref = refs

    x = x_ref[...]
    mod = mod_ref[0, 0]
    shift1 = mod[:, 0:D_MODEL]
    scale1 = mod[:, D_MODEL:2 * D_MODEL]
    h = (_rms(x) * g1_ref[0] * (1.0 + scale1) + shift1).astype(BF16)

    def proj(c0, c1):
        return _mm(h, win_ref[0, :, c0:c1])

    cqn = (_rms(proj(C_CQ, C_CKV)) * gq_ref[0]).astype(BF16)
    q = _rms_heads128(_mm(cqn, wuq_ref[0]), QK_A) * gqa_ref[0]
    if latent:
        q = _rope(q, cosa_ref[...], sina_ref[...])
    qa_ref[...] = q.astype(BF16)

    ckvn = _rms(proj(C_CKV, C_KR)) * gkv_ref[0]
    kr = proj(C_KR, C_DQ)
    ckvb = ckvn.astype(BF16)
    k = _rms_heads128(_mm(ckvb, wuk_ref[0]) + _tile_rope_key(kr), QK_A) * gka_ref[0]
    if latent:
        k = _rope(k, cosa_ref[...], sina_ref[...])
    ka_ref[...] = k.astype(BF16)
    va_ref[...] = _mm(ckvb, wuv_ref[0]).astype(BF16)

    bd32 = bd32_ref[...]
    qd = _rms_segments(proj(C_DQ, C_DK), bd32, DH_B) * gdq_ref[0]
    kd = _rms_segments(proj(C_DK, C_DV), bd32, DH_B) * gdk_ref[0]
    vd = proj(C_DV, C_Z)
    if latent:
        qd = _rope(qd, cosd_ref[...], sind_ref[...])
        kd = _rope(kd, cosd_ref[...], sind_ref[...])
    qd_ref[...] = qd.astype(BF16)
    kd_ref[...] = kd.astype(BF16)
    vd_ref[...] = vd.astype(BF16)

    z_ref[...] = proj(C_Z, C_XBC)
    xbc_ref[...] = proj(C_XBC, C_DT)
    dt_ref[...] = proj(C_DT, NP_IN)

    if not latent:
        ckvn_ref[...] = ckvn
        krope_ref[...] = kr[:, 0:ROPE_A]
        kd32_ref[...] = kd
        vd32_ref[...] = vd


def _pre_call(x2d, mod4, P, l, latent):
    T = x2d.shape[0]
    tm = TM_TOK
    tpb = DEC_SEQ // tm
    row = lambda i: (i, 0)
    lay = lambda i: (l, 0, 0)
    if latent:
        mod_map = lambda i: (l, i // tpb, 0, 0)
    else:
        mod_map = lambda i: (l, DEC_BATCH, 0, 0)
    full = lambda i: (0, 0)
    in_specs = [pl.BlockSpec((tm, D_MODEL), row),
                pl.BlockSpec((1, 1, 1, N_MOD * D_MODEL), mod_map),
                pl.BlockSpec((1, 1, D_MODEL), lay),
                pl.BlockSpec((1, D_MODEL, NP_IN), lay),
                pl.BlockSpec((1, 1, Q_RANK), lay),
                pl.BlockSpec((1, 1, KV_RANK), lay),
                pl.BlockSpec((1, Q_RANK, WA), lay),
                pl.BlockSpec((1, KV_RANK, WA), lay),
                pl.BlockSpec((1, KV_RANK, WVA), lay),
                pl.BlockSpec((1, 1, WA), lay),
                pl.BlockSpec((1, 1, WA), lay),
                pl.BlockSpec((1, 1, WD), lay),
                pl.BlockSpec((1, 1, WD), lay),
                pl.BlockSpec((WD, WD), full)]
    args = [x2d, mod4, P["g1"], P["win"], P["gq"], P["gkv"], P["wuq"], P["wuk"], P["wuv"],
            P["gqa"], P["gka"], P["gdq"], P["gdk"], P["bd32"]]
    if latent:
        pos = lambda i: (i % tpb, 0)
        in_specs += [pl.BlockSpec((tm, WA), pos), pl.BlockSpec((tm, WA), pos),
                     pl.BlockSpec((tm, WD), pos), pl.BlockSpec((tm, WD), pos)]
        args += [P["cosa"], P["sina"], P["cosd"], P["sind"]]
    widths = [(WA, BF16), (WA, BF16), (WVA, BF16), (WD, BF16), (WD, BF16), (WD, BF16),
              (D_INNER, F32), (CONV_DIM, F32), (LANES, F32)]
    if not latent:
        widths += [(KV_RANK, F32), (ROPE_A, F32), (WD, F32), (WD, F32)]
    out_specs = [pl.BlockSpec((tm, w), row) for w, _ in widths]
    out_shape = [jax.ShapeDtypeStruct((T, w), dt) for w, dt in widths]
    return pl.pallas_call(
        functools.partial(_pre_kernel, latent),
        grid=(T // tm,),
        in_specs=in_specs, out_specs=out_specs, out_shape=out_shape,
        compiler_params=pltpu.CompilerParams(dimension_semantics=("arbitrary",),
                                             vmem_limit_bytes=V7X_VMEM_LIMIT),
        name="pre_lat" if latent else "pre_ctx",
    )(*args)


def _softmax(q, key_refs, c0, c1):
    s = [_mm_nt(q, kr[:, c0:c1]) for kr in key_refs]
    m = s[0].max(axis=-1, keepdims=True)
    for si in s[1:]:
        m = jnp.maximum(m, si.max(axis=-1, keepdims=True))
    p = [jnp.exp(si - m) for si in s]
    den = p[0].sum(axis=-1, keepdims=True)
    for pi in p[1:]:
        den = den + pi.sum(axis=-1, keepdims=True)
    return p, den


def _pv(p, val_refs):
    pv = _mm(p[0].astype(BF16), val_refs[0][...])
    for pi, vr in zip(p[1:], val_refs[1:]):
        pv = pv + _mm(pi.astype(BF16), vr[...])
    return pv


def _attn_kernel(has_cache, lam_init, *refs):
    qa_ref, ka_ref, va_ref, qd_ref, kd_ref, vd_ref = refs[:6]
    refs = refs[6:]
    if has_cache:
        kca_ref, vca_ref, kcd_ref, vcd_ref = refs[:4]
        refs = refs[4:]
    lq1_ref, lk1_ref, lq2_ref, lk2_ref, gsub_ref, bd64_ref, o_ref = refs

    ka_refs = [ka_ref.at[0]]
    va_refs = [va_ref.at[0]]
    kd_refs = [kd_ref.at[0]]
    vd_refs = [vd_ref.at[0]]
    if has_cache:
        ka_refs = [kca_ref.at[0, 0]] + ka_refs
        va_refs = [vca_ref.at[0, 0]] + va_refs
        kd_refs = [kcd_ref.at[0, 0]] + kd_refs
        vd_refs = [vcd_ref.at[0, 0]] + vd_refs

    lane256 = lax.broadcasted_iota(jnp.int32, (1, WD), 1)
    lane128 = lax.broadcasted_iota(jnp.int32, (1, LANES), 1)

    qa = qa_ref[0]
    oa = None
    for hh in range(H_A):
        p, den = _softmax(qa[:, hh * LANES:(hh + 1) * LANES], ka_refs, hh * LANES, (hh + 1) * LANES)
        contrib = jnp.where(lane256 // V_A == hh, _pv(p, va_refs) * (1.0 / den), 0.0)
        oa = contrib if oa is None else oa + contrib

    lam = (jnp.exp(jnp.sum(lq1_ref[0] * lk1_ref[0], axis=-1, keepdims=True))
           - jnp.exp(jnp.sum(lq2_ref[0] * lk2_ref[0], axis=-1, keepdims=True)) + lam_init)
    qd = qd_ref[0]
    od = None
    for hh in range(H_B):
        blk = hh // 2
        qblk = qd[:, blk * LANES:(blk + 1) * LANES]
        maps = []
        for mm_i in range(2):
            seg = (hh % 2) * 2 + mm_i
            qm = jnp.where(lane128 // DH_B == seg, qblk, jnp.zeros_like(qblk))
            p, den = _softmax(qm, kd_refs, blk * LANES, (blk + 1) * LANES)
            maps.append(_pv(p, vd_refs) * (1.0 / den))
        contrib = jnp.where(lane256 // (2 * DH_B) == hh, maps[0] - lam * maps[1], 0.0)
        od = contrib if od is None else od + contrib
    od = _rms_segments(od, bd64_ref[...], 2 * DH_B) * (gsub_ref[0] * (1.0 - lam_init))
    o_ref[0] = jnp.concatenate([oa, od], axis=-1).astype(BF16)


def _attn_call(qa, ka, va, qd, kd, vd, caches, P, l):
    B, L = qa.shape[0], qa.shape[1]
    has_cache = caches is not None
    lam_init = 0.8 - 0.6 * math.exp(-0.3 * l)
    qmap = lambda b, i: (b, i, 0)
    kmap = lambda b, i: (b, 0, 0)
    lay = lambda b, i: (l, 0, 0)
    tq = min(TQ, L)
    in_specs = [pl.BlockSpec((1, tq, WA), qmap), pl.BlockSpec((1, L, WA), kmap),
                pl.BlockSpec((1, L, WVA), kmap), pl.BlockSpec((1, tq, WD), qmap),
                pl.BlockSpec((1, L, WD), kmap), pl.BlockSpec((1, L, WD), kmap)]
    args = [qa, ka, va, qd, kd, vd]
    if has_cache:
        cmap = lambda b, i: (b, l, 0, 0)
        in_specs += [pl.BlockSpec((1, 1, PAST_LEN, WA), cmap), pl.BlockSpec((1, 1, PAST_LEN, WVA), cmap),
                     pl.BlockSpec((1, 1, PAST_LEN, WD), cmap), pl.BlockSpec((1, 1, PAST_LEN, WD), cmap)]
        args += list(caches)
    in_specs += [pl.BlockSpec((1, 1, DH_B), lay)] * 4
    in_specs += [pl.BlockSpec((1, 1, WD), lay), pl.BlockSpec((WD, WD), lambda b, i: (0, 0))]
    args += [P["lq1"], P["lk1"], P["lq2"], P["lk2"], P["gsub"], P["bd64"]]
    return pl.pallas_call(
        functools.partial(_attn_kernel, has_cache, lam_init),
        grid=(B, L // tq),
        in_specs=in_specs,
        out_specs=pl.BlockSpec((1, tq, WVA + WD), qmap),
        out_shape=jax.ShapeDtypeStruct((B, L, WVA + WD), BF16),
        compiler_params=pltpu.CompilerParams(dimension_semantics=("arbitrary", "arbitrary"),
                                             vmem_limit_bytes=V7X_VMEM_LIMIT),
        name="attn_lat" if has_cache else "attn_ctx",
    )(*args)


def _ssd_kernel(L, has_h0, *refs):
    z_ref, xbc_ref, dt_ref = refs[:3]
    refs = refs[3:]
    if has_h0:
        s0_ref = refs[0]
        refs = refs[1:]
    cw_ref, cb_ref, alog_ref, dtb_ref, dexp_ref, g_ref, tri_ref = refs[:7]
    refs = refs[7:]
    o_ref = refs[0]
    refs = refs[1:]
    if not has_h0:
        st_ref = refs[0]
        refs = refs[1:]
    xpad, xc, yacc, cum_s, bt_s, cumt_s, dtt_s, wt_s, el_s, s_scr = refs
    nc = L // CHUNK
    halo = 8
    nd = 2 * H_C

    xpad[0:halo, :] = jnp.zeros((halo, CONV_DIM), F32)
    xpad[L + halo:L + 2 * halo, :] = jnp.zeros((halo, CONV_DIM), F32)
    xpad[halo:L + halo, :] = xbc_ref[0]
    cw = cw_ref[0]
    cb = cb_ref[0]
    dtb = dtb_ref[0]
    a_neg = -jnp.exp(alog_ref[0])
    lane128 = lax.broadcasted_iota(jnp.int32, (1, LANES), 1)
    fwd_lane = lane128 < H_C
    tri_f = tri_ref[0]
    tri_b = tri_ref[1]

    static_prep = nc <= 2

    def prep_body(c, carry):
        base = c * CHUNK if static_prep else pl.multiple_of(c * CHUNK, CHUNK)
        accs = []
        for g0 in range(0, CONV_DIM, CONV_COLS):
            a = cb[:, g0:g0 + CONV_COLS]
            if not static_prep:
                win = xpad[pl.ds(base, CHUNK + 2 * halo), g0:g0 + CONV_COLS]
            for k in range(CONV_K):
                off = halo - CONV_K // 2 + k
                if static_prep:
                    tap = xpad[base + off:base + off + CHUNK, g0:g0 + CONV_COLS]
                else:
                    tap = win[off:off + CHUNK, :]
                a = a + tap * cw[k:k + 1, g0:g0 + CONV_COLS]
            accs.append(a)
        acc = jnp.concatenate(accs, axis=-1)
        act = acc * _sigmoid(acc)
        xc[pl.ds(base, CHUNK), :] = act
        bt_s[pl.ds(base, CHUNK), :] = act[:, D_INNER:D_INNER + LANES].T
        dtr = dt_ref[0, pl.ds(base, CHUNK), :] + dtb
        dtc = jnp.maximum(dtr, 0.0) + jnp.log(1.0 + jnp.exp(-jnp.abs(dtr)))
        hi, mid, lo = _split3(dtc * a_neg)
        cum_f = _mm(tri_f, hi) + _mm(tri_f, mid) + _mm(tri_f, lo)
        cum_b = _mm(tri_b, hi) + _mm(tri_b, mid) + _mm(tri_b, lo)
        cum = jnp.where(fwd_lane, cum_f, cum_b)
        last = jnp.where(fwd_lane, cum[CHUNK - 1:CHUNK, :], cum[0:1, :])
        cum_s[pl.ds(base, CHUNK), :] = cum
        rows = pl.ds(c * nd, nd) if static_prep else pl.ds(pl.multiple_of(c * nd, nd), nd)
        cumt_s[rows, :] = cum.T[0:nd, :]
        dtt_s[rows, :] = dtc.T[0:nd, :]
        wt_s[rows, :] = (jnp.exp(last - cum) * dtc).T[0:nd, :]
        el_rows = pl.ds(c * 8, 8) if static_prep else pl.ds(pl.multiple_of(c * 8, 8), 8)
        el_s[el_rows, :] = jnp.broadcast_to(jnp.exp(last), (8, LANES))
        return carry

    if static_prep:
        for c_static in range(nc):
            prep_body(c_static, 0)
    else:
        lax.fori_loop(0, nc, prep_body, 0)

    row_i = lax.broadcasted_iota(jnp.int32, (CHUNK, CHUNK), 0)
    col_j = lax.broadcasted_iota(jnp.int32, (CHUNK, CHUNK), 1)
    blockmask = (lax.broadcasted_iota(jnp.int32, (2 * N_C, D_INNER), 0) // N_C
                 == lax.broadcasted_iota(jnp.int32, (2 * N_C, D_INNER), 1) // (D_INNER // G_C))
    dexp = dexp_ref[0]
    gnorm = g_ref[0]

    for d in range(2):
        if has_h0:
            s_scr[...] = s0_ref[0, 0, d]
        else:
            s_scr[...] = jnp.zeros((2 * N_C, D_INNER), F32)
        causal = (row_i >= col_j) if d == 0 else (col_j >= row_i)

        def per_head_lanes(cols, d=d):
            parts = []
            for pair in range(H_C // 2):
                i0 = d * H_C + 2 * pair
                parts.append(jnp.where(lane128 < P_C, cols[:, i0:i0 + 1], cols[:, i0 + 1:i0 + 2]))
            return jnp.concatenate(parts, axis=-1)

        def chunk_body(step, carry, d=d, causal=causal, per_head_lanes=per_head_lanes):
            c = step if d == 0 else nc - 1 - step
            base = pl.multiple_of(c * CHUNK, CHUNK)
            rows = pl.ds(pl.multiple_of(c * nd, nd), nd)
            xs = xc[pl.ds(base, CHUNK), 0:D_INNER]
            c_c = xc[pl.ds(base, CHUNK), D_INNER + LANES:D_INNER + 2 * LANES]
            cum = cum_s[pl.ds(base, CHUNK), :]
            cum_t = cumt_s[rows, :]
            dt_t = dtt_s[rows, :]
            w_t = wt_s[rows, :]
            b_t = bt_s[pl.ds(base, CHUNK), :]
            cb16 = c_c.astype(BF16)
            bt16 = b_t.astype(BF16)
            cbs = [_mm(jnp.where(lane128 // N_C == g, cb16, jnp.zeros_like(cb16)), bt16)
                   for g in range(G_C)]
            sv = s_scr[...]

            def blockdiag(v):
                return jnp.concatenate([jnp.where(lane128 < P_C, v, 0.0),
                                        jnp.where(lane128 < P_C, 0.0, v)], axis=0).astype(BF16)

            ys, upds = [], []
            for pair in range(H_C // 2):
                xbd = blockdiag(xs[:, pair * LANES:(pair + 1) * LANES])
                sbd = blockdiag(sv[:, pair * LANES:(pair + 1) * LANES])
                ms, es, ws = [], [], []
                for hh in (2 * pair, 2 * pair + 1):
                    idx = d * H_C + hh
                    col = jnp.broadcast_to(cum[:, idx:idx + 1], (CHUNK, CHUNK))
                    dec = jnp.where(causal, jnp.exp(col - cum_t[idx:idx + 1, :]), 0.0)
                    ms.append((cbs[hh // (H_C // G_C)] * dec * dt_t[idx:idx + 1, :]).astype(BF16))
                    es.append((jnp.exp(col) * c_c).astype(BF16))
                    ws.append((b_t * w_t[idx:idx + 1, :]).astype(BF16))
                ys.append(_mm(jnp.concatenate(ms + es, axis=-1), jnp.concatenate([xbd, sbd], axis=0)))
                upds.append(_mm(jnp.concatenate(ws, axis=-1), xbd))
            y = jnp.concatenate(ys, axis=-1)
            cd = per_head_lanes(el_s[pl.ds(pl.multiple_of(c * 8, 8), 1), :])
            s_scr[...] = sv * cd + jnp.where(blockmask, jnp.concatenate(upds, axis=-1), 0.0)
            if d == 0:
                yacc[pl.ds(base, CHUNK), :] = y
            else:
                zc = z_ref[0, pl.ds(base, CHUNK), :]
                yt = (yacc[pl.ds(base, CHUNK), :] + y + dexp * xs) * (zc * _sigmoid(zc))
                o_ref[0, pl.ds(base, CHUNK), :] = (_rms(yt) * gnorm).astype(BF16)
            return carry

        lax.fori_loop(0, nc, chunk_body, 0, unroll=2)
        if not has_h0:
            st = s_scr[...].T
            st_ref[0, d] = (st + pltpu.roll(st, N_C, 1))[:, 0:N_C]


def _ssd_call(z, xbc, dt, s0, P, l):
    B, L = z.shape[0], z.shape[1]
    has_h0 = s0 is not None
    bmap = lambda b: (b, 0, 0)
    lay = lambda b: (l, 0, 0)
    in_specs = [pl.BlockSpec((1, L, D_INNER), bmap), pl.BlockSpec((1, L, CONV_DIM), bmap),
                pl.BlockSpec((1, L, LANES), bmap)]
    args = [z, xbc, dt]
    if has_h0:
        in_specs.append(pl.BlockSpec((1, 1, 2, 2 * N_C, D_INNER), lambda b: (b, l, 0, 0, 0)))
        args.append(s0)
    in_specs += [pl.BlockSpec((1, 8, CONV_DIM), lay), pl.BlockSpec((1, 1, CONV_DIM), lay),
                 pl.BlockSpec((1, 1, LANES), lay), pl.BlockSpec((1, 1, LANES), lay),
                 pl.BlockSpec((1, 1, D_INNER), lay), pl.BlockSpec((1, 1, D_INNER), lay),
                 pl.BlockSpec((2, CHUNK, CHUNK), lambda b: (0, 0, 0))]
    args += [P["cw"], P["cb"], P["alog"], P["dtb"], P["dexp"], P["gssm"], P["tri"]]
    out_specs = [pl.BlockSpec((1, L, D_INNER), bmap)]
    out_shape = [jax.ShapeDtypeStruct((B, L, D_INNER), BF16)]
    if not has_h0:
        out_specs.append(pl.BlockSpec((1, 2, D_INNER, N_C), lambda b: (b, 0, 0, 0)))
        out_shape.append(jax.ShapeDtypeStruct((B, 2, D_INNER, N_C), F32))
    return pl.pallas_call(
        functools.partial(_ssd_kernel, L, has_h0),
        grid=(B,),
        in_specs=in_specs, out_specs=out_specs, out_shape=out_shape,
        scratch_shapes=[pltpu.VMEM((L + 16, CONV_DIM), F32), pltpu.VMEM((L, CONV_DIM), F32),
                        pltpu.VMEM((L, D_INNER), F32), pltpu.VMEM((L, LANES), F32),
                        pltpu.VMEM((L, LANES), F32)]
        + [pltpu.VMEM((L // CHUNK * 2 * H_C, LANES), F32)] * 3
        + [pltpu.VMEM((L // CHUNK * 8, LANES), F32), pltpu.VMEM((2 * N_C, D_INNER), F32)],
        compiler_params=pltpu.CompilerParams(dimension_semantics=("arbitrary",),
                                             vmem_limit_bytes=V7X_VMEM_LIMIT),
        name="ssd_lat" if has_h0 else "ssd_ctx",
    )(*args)


def _post_kernel(x_ref, oat_ref, oc_ref, mod_ref, wo_ref, g2_ref, w1_ref, w2_ref, o_ref):
    mod = mod_ref[0, 0]
    gate1 = mod[:, 2 * D_MODEL:3 * D_MODEL]
    shift2 = mod[:, 3 * D_MODEL:4 * D_MODEL]
    scale2 = mod[:, 4 * D_MODEL:5 * D_MODEL]
    gate2 = mod[:, 5 * D_MODEL:6 * D_MODEL]
    n_att = WVA + WD
    mix = _mm(oat_ref[...], wo_ref[0, 0:n_att, :]) + _mm(oc_ref[...], wo_ref[0, n_att:n_att + D_INNER, :])
    x1 = x_ref[...] + gate1 * mix
    h2 = (_rms(x1) * g2_ref[0] * (1.0 + scale2) + shift2).astype(BF16)
    ff = None
    for c in range(D_FF // FF_CHUNK):
        u = jnp.maximum(_mm(h2, w1_ref[0, :, c * FF_CHUNK:(c + 1) * FF_CHUNK]), 0.0)
        part = _mm((u * u).astype(BF16), w2_ref[0, c * FF_CHUNK:(c + 1) * FF_CHUNK, :])
        ff = part if ff is None else ff + part
    o_ref[...] = x1 + gate2 * ff


def _post_call(x2d, oat, oc, mod4, P, l, latent):
    T = x2d.shape[0]
    tm = TM_TOK
    tpb = DEC_SEQ // tm
    row = lambda i: (i, 0)
    lay = lambda i: (l, 0, 0)
    if latent:
        mod_map = lambda i: (l, i // tpb, 0, 0)
    else:
        mod_map = lambda i: (l, DEC_BATCH, 0, 0)
    single = pl.Buffered(1)
    return pl.pallas_call(
        _post_kernel,
        grid=(T // tm,),
        in_specs=[pl.BlockSpec((tm, D_MODEL), row),
                  pl.BlockSpec((tm, WVA + WD), row),
                  pl.BlockSpec((tm, D_INNER), row),
                  pl.BlockSpec((1, 1, 1, N_MOD * D_MODEL), mod_map),
                  pl.BlockSpec((1, D_MODEL, D_MODEL), lay, pipeline_mode=single),
                  pl.BlockSpec((1, 1, D_MODEL), lay),
                  pl.BlockSpec((1, D_MODEL, D_FF), lay, pipeline_mode=single),
                  pl.BlockSpec((1, D_FF, D_MODEL), lay, pipeline_mode=single)],
        out_specs=pl.BlockSpec((tm, D_MODEL), row),
        out_shape=jax.ShapeDtypeStruct((T, D_MODEL), F32),
        compiler_params=pltpu.CompilerParams(dimension_semantics=("arbitrary",),
                                             vmem_limit_bytes=V7X_VMEM_LIMIT),
        name="post_lat" if latent else "post_ctx",
    )(x2d, oat, oc, mod4, P["wout"], P["g2"], P["wff1"], P["wff2"])


def _rope_tables():
    t = np.arange(DEC_SEQ)
    rows = (t // GRID_W).astype(np.float32)
    cols = (t % GRID_W).astype(np.float32)
    half = ROPE_A // 2
    freqs = jnp.asarray(ROPE_BASE, F32) ** (-jnp.arange(0, half, 2, dtype=F32) / half)
    ang_r = jnp.asarray(rows)[:, None] * freqs
    ang_c = jnp.asarray(cols)[:, None] * freqs
    cos32 = jnp.concatenate([jnp.cos(ang_r), jnp.cos(ang_r), jnp.cos(ang_c), jnp.cos(ang_c)], axis=-1)
    sin32 = jnp.concatenate([-jnp.sin(ang_r), jnp.sin(ang_r), -jnp.sin(ang_c), jnp.sin(ang_c)], axis=-1)
    ones = jnp.ones((DEC_SEQ, NOPE_A), F32)
    zeros = jnp.zeros((DEC_SEQ, NOPE_A), F32)
    pad1 = jnp.ones((DEC_SEQ, LANES - QK_A), F32)
    pad0 = jnp.zeros((DEC_SEQ, LANES - QK_A), F32)
    cosa = jnp.tile(jnp.concatenate([ones, cos32, pad1], axis=-1), (1, H_A))
    sina = jnp.tile(jnp.concatenate([zeros, sin32, pad0], axis=-1), (1, H_A))
    cosd = jnp.tile(cos32, (1, WD // ROPE_A))
    sind = jnp.tile(sin32, (1, WD // ROPE_A))
    return cosa, sina, cosd, sind


def _constants():
    lane = np.arange(WD)
    bd32 = (lane[:, None] // DH_B == lane[None, :] // DH_B).astype(np.float32)
    bd64 = (lane[:, None] // (2 * DH_B) == lane[None, :] // (2 * DH_B)).astype(np.float32)
    i = np.arange(CHUNK)
    tri = np.stack([(i[None, :] <= i[:, None]), (i[None, :] >= i[:, None])]).astype(np.float32)
    return dict(bd32=jnp.asarray(bd32, BF16), bd64=jnp.asarray(bd64, BF16),
                tri=jnp.asarray(tri, BF16))


def _pad_last(a, n):
    return jnp.pad(a, [(0, 0)] * (a.ndim - 1) + [(0, n - a.shape[-1])])


def _prep_params(norm1_g, norm2_g, w_in, w_out, mla_q_norm_g, mla_kv_norm_g, w_uq, w_ukv, mla_qk_norm_q,
                 mla_qk_norm_k, diff_q_norm_g, diff_k_norm_g, diff_lq1, diff_lk1, diff_lq2, diff_lk2,
                 diff_subln_g, ssm_conv_w, ssm_conv_b, ssm_A_log, ssm_dt_bias, ssm_D, ssm_norm_g, w_ff1, w_ff2):
    P = _constants()
    offs = np.cumsum((Q_RANK, KV_RANK, ROPE_A, WD, WD, WD, D_INNER, CONV_DIM, 2 * H_C))
    o = [0] + offs.tolist()
    w_in16 = w_in.astype(BF16)
    sl = lambda i: w_in16[:, :, o[i]:o[i + 1]]
    win = jnp.concatenate([sl(0), sl(1), _pad_last(sl(2), LANES), sl(3), sl(4), sl(5), sl(6), sl(7),
                           _pad_last(sl(8), LANES)], axis=-1)
    P["win"] = win
    P["wuq"] = _pad_last(w_uq.reshape(DEPTH, Q_RANK, H_A, QK_A), LANES).reshape(DEPTH, Q_RANK, WA).astype(BF16)
    wkv = w_ukv.reshape(DEPTH, KV_RANK, H_A, NOPE_A + V_A)
    P["wuk"] = _pad_last(wkv[..., :NOPE_A], LANES).reshape(DEPTH, KV_RANK, WA).astype(BF16)
    P["wuv"] = wkv[..., NOPE_A:].reshape(DEPTH, KV_RANK, WVA).astype(BF16)
    P["wout"] = w_out.astype(BF16)
    P["wff1"] = w_ff1.astype(BF16)
    P["wff2"] = w_ff2.astype(BF16)
    vec = lambda a: a.reshape(DEPTH, 1, -1)
    P["g1"] = vec(norm1_g)
    P["g2"] = vec(norm2_g)
    P["gq"] = vec(mla_q_norm_g)
    P["gkv"] = vec(mla_kv_norm_g)
    P["gqa"] = vec(jnp.tile(_pad_last(mla_qk_norm_q, LANES), (1, H_A))) * (QK_A ** -0.5)
    P["gka"] = vec(jnp.tile(_pad_last(mla_qk_norm_k, LANES), (1, H_A)))
    P["gdq"] = vec(jnp.tile(diff_q_norm_g, (1, WD // DH_B))) * (DH_B ** -0.5)
    P["gdk"] = vec(jnp.tile(diff_k_norm_g, (1, WD // DH_B)))
    P["gsub"] = vec(jnp.tile(diff_subln_g, (1, H_B)))
    P["lq1"], P["lk1"], P["lq2"], P["lk2"] = vec(diff_lq1), vec(diff_lk1), vec(diff_lq2), vec(diff_lk2)
    P["cw"] = jnp.pad(ssm_conv_w, ((0, 0), (0, 8 - CONV_K), (0, 0)))
    P["cb"] = vec(ssm_conv_b)
    P["alog"] = vec(_pad_last(ssm_A_log.reshape(DEPTH, 2 * H_C), LANES))
    P["dtb"] = vec(_pad_last(ssm_dt_bias.reshape(DEPTH, 2 * H_C), LANES))
    P["dexp"] = vec(jnp.repeat(ssm_D, P_C, axis=-1))
    P["gssm"] = vec(ssm_norm_g)
    P["cosa"], P["sina"], P["cosd"], P["sind"] = _rope_tables()
    return P


def kernel(x_prompt, x_sample, cache_mla_ckv, cache_mla_krope, cache_diff_k, cache_diff_v, state_ssm, c, c_ctx, norm1_g, norm2_g, w_ada, b_ada, w_in, w_out, mla_q_norm_g, mla_kv_norm_g, w_uq, w_ukv, mla_qk_norm_q, mla_qk_norm_k, diff_q_norm_g, diff_k_norm_g, diff_lq1, diff_lk1, diff_lq2, diff_lk2, diff_subln_g, ssm_conv_w, ssm_conv_b, ssm_A_log, ssm_dt_bias, ssm_D, ssm_norm_g, w_ff1, w_ff2):
    P = _prep_params(norm1_g, norm2_g, w_in, w_out, mla_q_norm_g, mla_kv_norm_g, w_uq, w_ukv, mla_qk_norm_q,
                     mla_qk_norm_k, diff_q_norm_g, diff_k_norm_g, diff_lq1, diff_lk1, diff_lq2, diff_lk2,
                     diff_subln_g, ssm_conv_w, ssm_conv_b, ssm_A_log, ssm_dt_bias, ssm_D, ssm_norm_g,
                     w_ff1, w_ff2)

    cvec = jnp.concatenate([c, c_ctx[None, :], jnp.zeros((8 - DEC_BATCH - 1, D_MODEL), F32)], axis=0)
    mod4 = _mod_call(cvec, w_ada, b_ada).reshape(DEPTH, 8, 1, N_MOD * D_MODEL)

    kca, vca = _cache_call(cache_mla_ckv, _pad_last(cache_mla_krope, LANES), P)
    kcd = cache_diff_k.reshape(DEC_BATCH, DEPTH, PAST_LEN, WD).astype(BF16)
    vcd = cache_diff_v.reshape(DEC_BATCH, DEPTH, PAST_LEN, WD).astype(BF16)
    st = jnp.transpose(state_ssm, (0, 1, 2, 5, 3, 4)).reshape(DEC_BATCH, DEPTH, 2, N_C, D_INNER)
    half = D_INNER // G_C
    zero = jnp.zeros_like(st[..., :half])
    s0 = jnp.concatenate([jnp.concatenate([st[..., :half], zero], axis=-1),
                          jnp.concatenate([zero, st[..., half:]], axis=-1)], axis=-2)

    xp = x_prompt.reshape(BATCH * SEQ, D_MODEL)
    xs = x_sample.reshape(DEC_BATCH * DEC_SEQ, D_MODEL)
    ckv_l, krope_l, kd_l, vd_l, st_l = [], [], [], [], []
    for l in range(DEPTH):
        qa, ka, va, qd, kd, vd, z, xbc, dt, ckvn, krope, kd32, vd32 = _pre_call(xp, mod4, P, l, False)
        r3 = lambda a: a.reshape(BATCH, SEQ, a.shape[-1])
        oat = _attn_call(r3(qa), r3(ka), r3(va), r3(qd), r3(kd), r3(vd), None, P, l)
        oc, st_new = _ssd_call(r3(z), r3(xbc), r3(dt), None, P, l)
        xp = _post_call(xp, oat.reshape(BATCH * SEQ, -1), oc.reshape(BATCH * SEQ, -1), mod4, P, l, False)
        ckv_l.append(ckvn.reshape(BATCH, SEQ, KV_RANK))
        krope_l.append(krope.reshape(BATCH, SEQ, ROPE_A))
        kd_l.append(kd32.reshape(BATCH, SEQ, H_B, 2 * DH_B))
        vd_l.append(vd32.reshape(BATCH, SEQ, H_B, 2 * DH_B))
        st_l.append(st_new.reshape(BATCH, 2, H_C, P_C, N_C))
        qa, ka, va, qd, kd, vd, z, xbc, dt = _pre_call(xs, mod4, P, l, True)
        r3 = lambda a: a.reshape(DEC_BATCH, DEC_SEQ, a.shape[-1])
        oat = _attn_call(r3(qa), r3(ka), r3(va), r3(qd), r3(kd), r3(vd), (kca, vca, kcd, vcd), P, l)
        oc, = _ssd_call(r3(z), r3(xbc), r3(dt), s0, P, l)
        xs = _post_call(xs, oat.reshape(DEC_BATCH * DEC_SEQ, -1), oc.reshape(DEC_BATCH * DEC_SEQ, -1),
                        mod4, P, l, True)
    return (xp.reshape(BATCH, SEQ, D_MODEL), xs.reshape(DEC_BATCH, DEC_SEQ, D_MODEL),
            jnp.stack(ckv_l, axis=1), jnp.stack(krope_l, axis=1), jnp.stack(kd_l, axis=1),
            jnp.stack(vd_l, axis=1), jnp.stack(st_l, axis=1))
```

```python
import functools
import math

import numpy as np
import jax
import jax.numpy as jnp
from jax import lax
from jax.experimental import pallas as pl
from jax.experimental.pallas import tpu as pltpu

D_MODEL = 1024
BATCH = 16
SEQ = 256
DEPTH = 4
DEC_BATCH = 4
DEC_SEQ = 2048
PAST_LEN = 256
GRID_W = 64
ROPE_BASE = 10000.0
EPS = 1e-6
CHUNK = 128
D_FF = 4 * D_MODEL
N_MOD = 6
H_A = 4
NOPE_A = 64
ROPE_A = 32
V_A = 64
Q_RANK = 256
KV_RANK = 128
H_B = 4
DH_B = 32
H_C = 8
P_C = 64
N_C = 64
G_C = 2
CONV_K = 5
D_INNER = H_C * P_C
CONV_DIM = D_INNER + 2 * G_C * N_C

F32 = jnp.float32
BF16 = jnp.bfloat16

LANES = 128
QK_A = NOPE_A + ROPE_A
WA = H_A * LANES
WD = H_B * 2 * DH_B
WVA = H_A * V_A
C_CQ, C_CKV, C_KR, C_DQ, C_DK, C_DV, C_Z, C_XBC, C_DT = 0, 256, 384, 512, 768, 1024, 1280, 1792, 2560
NP_IN = 2688
V7X_VMEM_LIMIT = 56 * 1024 * 1024
TM_TOK = 512
TQ = 512
FF_CHUNK = 1024
CONV_COLS = 256


def _mm(a, b):
    return jnp.dot(a, b, preferred_element_type=F32)


def _mm_nt(a, b):
    return lax.dot_general(a, b, (((1,), (1,)), ((), ())), preferred_element_type=F32)


def _split3(x):
    hi = x.astype(BF16)
    r = x - hi.astype(F32)
    mid = r.astype(BF16)
    lo = (r - mid.astype(F32)).astype(BF16)
    return hi, mid, lo


def _sigmoid(x):
    return 1.0 / (1.0 + jnp.exp(-x))


def _rms(x, n=None):
    n = x.shape[-1] if n is None else n
    return x * lax.rsqrt(jnp.sum(x * x, axis=-1, keepdims=True) * (1.0 / n) + EPS)


def _rms_heads128(x, n_real):
    parts = [_rms(x[:, i * LANES:(i + 1) * LANES], n_real) for i in range(x.shape[1] // LANES)]
    return jnp.concatenate(parts, axis=-1)


def _rms_segments(x, bd, seg):
    sq = x * x
    hi = sq.astype(BF16)
    lo = (sq - hi.astype(F32)).astype(BF16)
    ssq = _mm(hi, bd) + _mm(lo, bd)
    return x * lax.rsqrt(ssq * (1.0 / seg) + EPS)


def _rope(x, cos, sin_signed):
    lane = lax.broadcasted_iota(jnp.int32, (1, LANES), 1)
    lo = (lane % 16) < 8
    parts = []
    for i in range(x.shape[1] // LANES):
        xb = x[:, i * LANES:(i + 1) * LANES]
        parts.append(jnp.where(lo, pltpu.roll(xb, LANES - 8, 1), pltpu.roll(xb, 8, 1)))
    return x * cos + jnp.concatenate(parts, axis=-1) * sin_signed


def _mod_kernel(c_ref, w_ref, b_ref, o_ref):
    cv = c_ref[...]
    s = cv * _sigmoid(cv)
    o_ref[0] = _mm(s.astype(BF16), w_ref[0].astype(BF16)) + b_ref[0]


def _mod_call(cvec, w_ada, b_ada):
    tn = 1536
    n_out = N_MOD * D_MODEL
    return pl.pallas_call(
        _mod_kernel,
        grid=(DEPTH, n_out // tn),
        in_specs=[pl.BlockSpec((8, D_MODEL), lambda l, j: (0, 0)),
                  pl.BlockSpec((1, D_MODEL, tn), lambda l, j: (l, 0, j)),
                  pl.BlockSpec((1, 1, tn), lambda l, j: (l, 0, j))],
        out_specs=pl.BlockSpec((1, 8, tn), lambda l, j: (l, 0, j)),
        out_shape=jax.ShapeDtypeStruct((DEPTH, 8, n_out), F32),
        compiler_params=pltpu.CompilerParams(dimension_semantics=("arbitrary", "arbitrary")),
        name="mod",
    )(cvec, w_ada, b_ada.reshape(DEPTH, 1, n_out))


def _tile_rope_key(kr):
    return jnp.concatenate([pltpu.roll(kr, NOPE_A, 1)] * H_A, axis=-1)


def _cache_kernel(ckv_ref, kr_ref, wuk_ref, wuv_ref, gka_ref, k_ref, v_ref):
    ckv = ckv_ref[0, 0].astype(BF16)
    kpre = _mm(ckv, wuk_ref[0]) + _tile_rope_key(kr_ref[0, 0])
    k_ref[0, 0] = (_rms_heads128(kpre, QK_A) * gka_ref[0]).astype(BF16)
    v_ref[0, 0] = _mm(ckv, wuv_ref[0]).astype(BF16)


def _cache_call(cache_ckv, cache_kr_pad, P):
    bl = lambda b, l: (b, l, 0, 0)
    lay = lambda b, l: (l, 0, 0)
    return pl.pallas_call(
        _cache_kernel,
        grid=(DEC_BATCH, DEPTH),
        in_specs=[pl.BlockSpec((1, 1, PAST_LEN, KV_RANK), bl),
                  pl.BlockSpec((1, 1, PAST_LEN, LANES), bl),
                  pl.BlockSpec((1, KV_RANK, WA), lay),
                  pl.BlockSpec((1, KV_RANK, WVA), lay),
                  pl.BlockSpec((1, 1, WA), lay)],
        out_specs=[pl.BlockSpec((1, 1, PAST_LEN, WA), bl),
                   pl.BlockSpec((1, 1, PAST_LEN, WVA), bl)],
        out_shape=[jax.ShapeDtypeStruct((DEC_BATCH, DEPTH, PAST_LEN, WA), BF16),
                   jax.ShapeDtypeStruct((DEC_BATCH, DEPTH, PAST_LEN, WVA), BF16)],
        compiler_params=pltpu.CompilerParams(dimension_semantics=("arbitrary", "arbitrary")),
        name="cache_expand",
    )(cache_ckv, cache_kr_pad, P["wuk"], P["wuv"], P["gka"])


def _pre_kernel(latent, *refs):
    (x_ref, mod_ref, g1_ref, wina_ref, winb_ref, winc_ref, gq_ref, gkv_ref, wuq_ref, wuk_ref, wuv_ref,
     gqa_ref, gka_ref, gdq_ref, gdk_ref, bd32_ref) = refs[:16]
    refs = refs[16:]
    if latent:
        cosa_ref, sina_ref, cosd_ref, sind_ref = refs[:4]
        refs = refs[4:]
    qa_ref, ka_ref, va_ref, qd_ref, kd_ref, vd_ref, z_ref, xbc_ref, dt_ref = refs[:9]
    refs = refs[9:]
    if not latent:
        ckvn_ref, krope_ref, kd32_ref, vd32_ref = refs

    x = x_ref[...]
    mod = mod_ref[0, 0]
    shift1 = mod[:, 0:D_MODEL]
    scale1 = mod[:, D_MODEL:2 * D_MODEL]
    h = (_rms(x) * g1_ref[0] * (1.0 + scale1) + shift1).astype(BF16)

    def proj(c0, c1):
        if c1 <= C_DQ:
            return _mm(h, wina_ref[0, :, c0:c1])
        if c0 >= C_DT:
            return _mm(h, winc_ref[0])
        return _mm(h, winb_ref[0, :, c0 - C_DQ:c1 - C_DQ])

    cqn = (_rms(proj(C_CQ, C_CKV)) * gq_ref[0]).astype(BF16)
    q = _rms_heads128(_mm(cqn, wuq_ref[0]), QK_A) * gqa_ref[0]
    if latent:
        q = _rope(q, cosa_ref[...], sina_ref[...])
    qa_ref[...] = q.astype(BF16)

    ckvn = _rms(proj(C_CKV, C_KR)) * gkv_ref[0]
    kr = proj(C_KR, C_DQ)
    ckvb = ckvn.astype(BF16)
    k = _rms_heads128(_mm(ckvb, wuk_ref[0]) + _tile_rope_key(kr), QK_A) * gka_ref[0]
    if latent:
        k = _rope(k, cosa_ref[...], sina_ref[...])
    ka_ref[...] = k.astype(BF16)
    va_ref[...] = _mm(ckvb, wuv_ref[0]).astype(BF16)

    bd32 = bd32_ref[...]
    qd = _rms_segments(proj(C_DQ, C_DK), bd32, DH_B) * gdq_ref[0]
    kd = _rms_segments(proj(C_DK, C_DV), bd32, DH_B) * gdk_ref[0]
    vd = proj(C_DV, C_Z)
    if latent:
        qd = _rope(qd, cosd_ref[...], sind_ref[...])
        kd = _rope(kd, cosd_ref[...], sind_ref[...])
    qd_ref[...] = qd.astype(BF16)
    kd_ref[...] = kd.astype(BF16)
    vd_ref[...] = vd.astype(BF16)

    z_ref[...] = proj(C_Z, C_XBC)
    xbc_ref[...] = proj(C_XBC, C_DT)
    dt_ref[...] = proj(C_DT, NP_IN)

    if not latent:
        ckvn_ref[...] = ckvn
        krope_ref[...] = kr[:, 0:ROPE_A]
        kd32_ref[...] = kd
        vd32_ref[...] = vd


def _pre_call(x2d, mod4, P, l, latent):
    T = x2d.shape[0]
    tm = TM_TOK
    tpb = DEC_SEQ // tm
    row = lambda i: (i, 0)
    lay = lambda i: (l, 0, 0)
    if latent:
        mod_map = lambda i: (l, i // tpb, 0, 0)
    else:
        mod_map = lambda i: (l, DEC_BATCH, 0, 0)
    full = lambda i: (0, 0)
    in_specs = [pl.BlockSpec((tm, D_MODEL), row),
                pl.BlockSpec((1, 1, 1, N_MOD * D_MODEL), mod_map),
                pl.BlockSpec((1, 1, D_MODEL), lay),
                pl.BlockSpec((1, D_MODEL, C_DQ), lay),
                pl.BlockSpec((1, D_MODEL, C_DT - C_DQ), lay),
                pl.BlockSpec((1, D_MODEL, NP_IN - C_DT), lay),
                pl.BlockSpec((1, 1, Q_RANK), lay),
                pl.BlockSpec((1, 1, KV_RANK), lay),
                pl.BlockSpec((1, Q_RANK, WA), lay),
                pl.BlockSpec((1, KV_RANK, WA), lay),
                pl.BlockSpec((1, KV_RANK, WVA), lay),
                pl.BlockSpec((1, 1, WA), lay),
                pl.BlockSpec((1, 1, WA), lay),
                pl.BlockSpec((1, 1, WD), lay),
                pl.BlockSpec((1, 1, WD), lay),
                pl.BlockSpec((WD, WD), full)]
    args = [x2d, mod4, P["g1"], P["win_a"], P["win_b"], P["win_c"], P["gq"], P["gkv"], P["wuq"], P["wuk"], P["wuv"],
            P["gqa"], P["gka"], P["gdq"], P["gdk"], P["bd32"]]
    if latent:
        pos = lambda i: (i % tpb, 0)
        in_specs += [pl.BlockSpec((tm, WA), pos), pl.BlockSpec((tm, WA), pos),
                     pl.BlockSpec((tm, WD), pos), pl.BlockSpec((tm, WD), pos)]
        args += [P["cosa"], P["sina"], P["cosd"], P["sind"]]
    widths = [(WA, BF16), (WA, BF16), (WVA, BF16), (WD, BF16), (WD, BF16), (WD, BF16),
              (D_INNER, F32), (CONV_DIM, F32), (LANES, F32)]
    if not latent:
        widths += [(KV_RANK, F32), (ROPE_A, F32), (WD, F32), (WD, F32)]
    out_specs = [pl.BlockSpec((tm, w), row) for w, _ in widths]
    out_shape = [jax.ShapeDtypeStruct((T, w), dt) for w, dt in widths]
    return pl.pallas_call(
        functools.partial(_pre_kernel, latent),
        grid=(T // tm,),
        in_specs=in_specs, out_specs=out_specs, out_shape=out_shape,
        compiler_params=pltpu.CompilerParams(dimension_semantics=("arbitrary",),
                                             vmem_limit_bytes=V7X_VMEM_LIMIT),
        name="pre_lat" if latent else "pre_ctx",
    )(*args)


def _softmax(q, key_refs, c0, c1):
    s = [_mm_nt(q, kr[:, c0:c1]) for kr in key_refs]
    m = s[0].max(axis=-1, keepdims=True)
    for si in s[1:]:
        m = jnp.maximum(m, si.max(axis=-1, keepdims=True))
    p = [jnp.exp(si - m) for si in s]
    den = p[0].sum(axis=-1, keepdims=True)
    for pi in p[1:]:
        den = den + pi.sum(axis=-1, keepdims=True)
    return p, den


def _pv(p, val_refs):
    pv = _mm(p[0].astype(BF16), val_refs[0][...])
    for pi, vr in zip(p[1:], val_refs[1:]):
        pv = pv + _mm(pi.astype(BF16), vr[...])
    return pv


def _attn_kernel(has_cache, lam_init, *refs):
    qa_ref, ka_ref, va_ref, qd_ref, kd_ref, vd_ref = refs[:6]
    refs = refs[6:]
    if has_cache:
        kca_ref, vca_ref, kcd_ref, vcd_ref = refs[:4]
        refs = refs[4:]
    lq1_ref, lk1_ref, lq2_ref, lk2_ref, gsub_ref, bd64_ref, o_ref = refs

    ka_refs = [ka_ref.at[0]]
    va_refs = [va_ref.at[0]]
    kd_refs = [kd_ref.at[0]]
    vd_refs = [vd_ref.at[0]]
    if has_cache:
        ka_refs = [kca_ref.at[0, 0]] + ka_refs
        va_refs = [vca_ref.at[0, 0]] + va_refs
        kd_refs = [kcd_ref.at[0, 0]] + kd_refs
        vd_refs = [vcd_ref.at[0, 0]] + vd_refs

    lane256 = lax.broadcasted_iota(jnp.int32, (1, WD), 1)
    lane128 = lax.broadcasted_iota(jnp.int32, (1, LANES), 1)

    qa = qa_ref[0]
    oa = None
    for hh in range(H_A):
        p, den = _softmax(qa[:, hh * LANES:(hh + 1) * LANES], ka_refs, hh * LANES, (hh + 1) * LANES)
        contrib = jnp.where(lane256 // V_A == hh, _pv(p, va_refs) * (1.0 / den), 0.0)
        oa = contrib if oa is None else oa + contrib

    lam = (jnp.exp(jnp.sum(lq1_ref[0] * lk1_ref[0], axis=-1, keepdims=True))
           - jnp.exp(jnp.sum(lq2_ref[0] * lk2_ref[0], axis=-1, keepdims=True)) + lam_init)
    qd = qd_ref[0]
    od = None
    for hh in range(H_B):
        blk = hh // 2
        qblk = qd[:, blk * LANES:(blk + 1) * LANES]
        maps = []
        for mm_i in range(2):
            seg = (hh % 2) * 2 + mm_i
            qm = jnp.where(lane128 // DH_B == seg, qblk, jnp.zeros_like(qblk))
            p, den = _softmax(qm, kd_refs, blk * LANES, (blk + 1) * LANES)
            maps.append(_pv(p, vd_refs) * (1.0 / den))
        contrib = jnp.where(lane256 // (2 * DH_B) == hh, maps[0] - lam * maps[1], 0.0)
        od = contrib if od is None else od + contrib
    od = _rms_segments(od, bd64_ref[...], 2 * DH_B) * (gsub_ref[0] * (1.0 - lam_init))
    o_ref[0] = jnp.concatenate([oa, od], axis=-1).astype(BF16)


def _attn_call(qa, ka, va, qd, kd, vd, caches, P, l):
    B, L = qa.shape[0], qa.shape[1]
    has_cache = caches is not None
    lam_init = 0.8 - 0.6 * math.exp(-0.3 * l)
    qmap = lambda b, i: (b, i, 0)
    kmap = lambda b, i: (b, 0, 0)
    lay = lambda b, i: (l, 0, 0)
    tq = min(TQ, L)
    in_specs = [pl.BlockSpec((1, tq, WA), qmap), pl.BlockSpec((1, L, WA), kmap),
                pl.BlockSpec((1, L, WVA), kmap), pl.BlockSpec((1, tq, WD), qmap),
                pl.BlockSpec((1, L, WD), kmap), pl.BlockSpec((1, L, WD), kmap)]
    args = [qa, ka, va, qd, kd, vd]
    if has_cache:
        cmap = lambda b, i: (b, l, 0, 0)
        in_specs += [pl.BlockSpec((1, 1, PAST_LEN, WA), cmap), pl.BlockSpec((1, 1, PAST_LEN, WVA), cmap),
                     pl.BlockSpec((1, 1, PAST_LEN, WD), cmap), pl.BlockSpec((1, 1, PAST_LEN, WD), cmap)]
        args += list(caches)
    in_specs += [pl.BlockSpec((1, 1, DH_B), lay)] * 4
    in_specs += [pl.BlockSpec((1, 1, WD), lay), pl.BlockSpec((WD, WD), lambda b, i: (0, 0))]
    args += [P["lq1"], P["lk1"], P["lq2"], P["lk2"], P["gsub"], P["bd64"]]
    return pl.pallas_call(
        functools.partial(_attn_kernel, has_cache, lam_init),
        grid=(B, L // tq),
        in_specs=in_specs,
        out_specs=pl.BlockSpec((1, tq, WVA + WD), qmap),
        out_shape=jax.ShapeDtypeStruct((B, L, WVA + WD), BF16),
        compiler_params=pltpu.CompilerParams(dimension_semantics=("arbitrary", "arbitrary"),
                                             vmem_limit_bytes=V7X_VMEM_LIMIT),
        name="attn_lat" if has_cache else "attn_ctx",
    )(*args)


def _ssd_kernel(L, has_h0, *refs):
    z_ref, xbc_ref, dt_ref = refs[:3]
    refs = refs[3:]
    if has_h0:
        s0_ref = refs[0]
        refs = refs[1:]
    cw_ref, cb_ref, alog_ref, dtb_ref, dexp_ref, g_ref, tri_ref = refs[:7]
    refs = refs[7:]
    o_ref = refs[0]
    refs = refs[1:]
    if not has_h0:
        st_ref = refs[0]
        refs = refs[1:]
    xpad, xc, yacc, cum_s, bt_s, cumt_s, dtt_s, wt_s, el_s, s_scr = refs
    nc = L // CHUNK
    halo = 8
    nd = 2 * H_C

    xpad[0:halo, :] = jnp.zeros((halo, CONV_DIM), F32)
    xpad[L + halo:L + 2 * halo, :] = jnp.zeros((halo, CONV_DIM), F32)
    xpad[halo:L + halo, :] = xbc_ref[0]
    cw = cw_ref[0]
    cb = cb_ref[0]
    dtb = dtb_ref[0]
    a_neg = -jnp.exp(alog_ref[0])
    lane128 = lax.broadcasted_iota(jnp.int32, (1, LANES), 1)
    fwd_lane = lane128 < H_C
    tri_f = tri_ref[0]
    tri_b = tri_ref[1]

    static_prep = nc <= 2

    def prep_body(c, carry):
        base = c * CHUNK if static_prep else pl.multiple_of(c * CHUNK, CHUNK)
        accs = []
        for g0 in range(0, CONV_DIM, CONV_COLS):
            a = cb[:, g0:g0 + CONV_COLS]
            if not static_prep:
                win = xpad[pl.ds(base, CHUNK + 2 * halo), g0:g0 + CONV_COLS]
            for k in range(CONV_K):
                off = halo - CONV_K // 2 + k
                if static_prep:
                    tap = xpad[base + off:base + off + CHUNK, g0:g0 + CONV_COLS]
                else:
                    tap = win[off:off + CHUNK, :]
                a = a + tap * cw[k:k + 1, g0:g0 + CONV_COLS]
            accs.append(a)
        acc = jnp.concatenate(accs, axis=-1)
        act = acc * _sigmoid(acc)
        xc[pl.ds(base, CHUNK), :] = act
        bt_s[pl.ds(base, CHUNK), :] = act[:, D_INNER:D_INNER + LANES].T
        dtr = dt_ref[0, pl.ds(base, CHUNK), :] + dtb
        dtc = jnp.maximum(dtr, 0.0) + jnp.log(1.0 + jnp.exp(-jnp.abs(dtr)))
        hi, mid, lo = _split3(dtc * a_neg)
        cum_f = _mm(tri_f, hi) + _mm(tri_f, mid) + _mm(tri_f, lo)
        cum_b = _mm(tri_b, hi) + _mm(tri_b, mid) + _mm(tri_b, lo)
        cum = jnp.where(fwd_lane, cum_f, cum_b)
        last = jnp.where(fwd_lane, cum[CHUNK - 1:CHUNK, :], cum[0:1, :])
        cum_s[pl.ds(base, CHUNK), :] = cum
        rows = pl.ds(c * nd, nd) if static_prep else pl.ds(pl.multiple_of(c * nd, nd), nd)
        cumt_s[rows, :] = cum.T[0:nd, :]
        dtt_s[rows, :] = dtc.T[0:nd, :]
        wt_s[rows, :] = (jnp.exp(last - cum) * dtc).T[0:nd, :]
        el_rows = pl.ds(c * 8, 8) if static_prep else pl.ds(pl.multiple_of(c * 8, 8), 8)
        el_s[el_rows, :] = jnp.broadcast_to(jnp.exp(last), (8, LANES))
        return carry

    if static_prep:
        for c_static in range(nc):
            prep_body(c_static, 0)
    else:
        lax.fori_loop(0, nc, prep_body, 0, unroll=2)

    row_i = lax.broadcasted_iota(jnp.int32, (CHUNK, CHUNK), 0)
    col_j = lax.broadcasted_iota(jnp.int32, (CHUNK, CHUNK), 1)
    blockmask = (lax.broadcasted_iota(jnp.int32, (2 * N_C, D_INNER), 0) // N_C
                 == lax.broadcasted_iota(jnp.int32, (2 * N_C, D_INNER), 1) // (D_INNER // G_C))
    dexp = dexp_ref[0]
    gnorm = g_ref[0]

    for d in range(2):
        if has_h0:
            s_scr[...] = s0_ref[0, 0, d]
        else:
            s_scr[...] = jnp.zeros((2 * N_C, D_INNER), F32)
        causal = (row_i >= col_j) if d == 0 else (col_j >= row_i)

        def per_head_lanes(cols, d=d):
            parts = []
            for pair in range(H_C // 2):
                i0 = d * H_C + 2 * pair
                parts.append(jnp.where(lane128 < P_C, cols[:, i0:i0 + 1], cols[:, i0 + 1:i0 + 2]))
            return jnp.concatenate(parts, axis=-1)

        def chunk_body(step, carry, d=d, causal=causal, per_head_lanes=per_head_lanes):
            c = step if d == 0 else nc - 1 - step
            base = pl.multiple_of(c * CHUNK, CHUNK)
            rows = pl.ds(pl.multiple_of(c * nd, nd), nd)
            xs = xc[pl.ds(base, CHUNK), 0:D_INNER]
            c_c = xc[pl.ds(base, CHUNK), D_INNER + LANES:D_INNER + 2 * LANES]
            cum = cum_s[pl.ds(base, CHUNK), :]
            cum_t = cumt_s[rows, :]
            dt_t = dtt_s[rows, :]
            w_t = wt_s[rows, :]
            b_t = bt_s[pl.ds(base, CHUNK), :]
            cb16 = c_c.astype(BF16)
            bt16 = b_t.astype(BF16)
            cbs = [_mm(jnp.where(lane128 // N_C == g, cb16, jnp.zeros_like(cb16)), bt16)
                   for g in range(G_C)]
            sv = s_scr[...]

            def blockdiag(v):
                return jnp.concatenate([jnp.where(lane128 < P_C, v, 0.0),
                                        jnp.where(lane128 < P_C, 0.0, v)], axis=0).astype(BF16)

            ys, upds = [], []
            for pair in range(H_C // 2):
                xbd = blockdiag(xs[:, pair * LANES:(pair + 1) * LANES])
                sbd = blockdiag(sv[:, pair * LANES:(pair + 1) * LANES])
                ms, es, ws = [], [], []
                for hh in (2 * pair, 2 * pair + 1):
                    idx = d * H_C + hh
                    col = jnp.broadcast_to(cum[:, idx:idx + 1], (CHUNK, CHUNK))
                    dec = jnp.where(causal, jnp.exp(col - cum_t[idx:idx + 1, :]), 0.0)
                    ms.append((cbs[hh // (H_C // G_C)] * dec * dt_t[idx:idx + 1, :]).astype(BF16))
                    es.append((jnp.exp(col) * c_c).astype(BF16))
                    ws.append((b_t * w_t[idx:idx + 1, :]).astype(BF16))
                ys.append(_mm(jnp.concatenate(ms + es, axis=-1), jnp.concatenate([xbd, sbd], axis=0)))
                upds.append(_mm(jnp.concatenate(ws, axis=-1), xbd))
            y = jnp.concatenate(ys, axis=-1)
            cd = per_head_lanes(el_s[pl.ds(pl.multiple_of(c * 8, 8), 1), :])
            s_scr[...] = sv * cd + jnp.where(blockmask, jnp.concatenate(upds, axis=-1), 0.0)
            if d == 0:
                yacc[pl.ds(base, CHUNK), :] = y
            else:
                zc = z_ref[0, pl.ds(base, CHUNK), :]
                yt = (yacc[pl.ds(base, CHUNK), :] + y + dexp * xs) * (zc * _sigmoid(zc))
                o_ref[0, pl.ds(base, CHUNK), :] = (_rms(yt) * gnorm).astype(BF16)
            return carry

        lax.fori_loop(0, nc, chunk_body, 0, unroll=2)
        if not has_h0:
            st = s_scr[...].T
            st_ref[0, d] = (st + pltpu.roll(st, N_C, 1))[:, 0:N_C]


def _ssd_call(z, xbc, dt, s0, P, l):
    B, L = z.shape[0], z.shape[1]
    has_h0 = s0 is not None
    bmap = lambda b: (b, 0, 0)
    lay = lambda b: (l, 0, 0)
    in_specs = [pl.BlockSpec((1, L, D_INNER), bmap), pl.BlockSpec((1, L, CONV_DIM), bmap),
                pl.BlockSpec((1, L, LANES), bmap)]
    args = [z, xbc, dt]
    if has_h0:
        in_specs.append(pl.BlockSpec((1, 1, 2, 2 * N_C, D_INNER), lambda b: (b, l, 0, 0, 0)))
        args.append(s0)
    in_specs += [pl.BlockSpec((1, 8, CONV_DIM), lay), pl.BlockSpec((1, 1, CONV_DIM), lay),
                 pl.BlockSpec((1, 1, LANES), lay), pl.BlockSpec((1, 1, LANES), lay),
                 pl.BlockSpec((1, 1, D_INNER), lay), pl.BlockSpec((1, 1, D_INNER), lay),
                 pl.BlockSpec((2, CHUNK, CHUNK), lambda b: (0, 0, 0))]
    args += [P["cw"], P["cb"], P["alog"], P["dtb"], P["dexp"], P["gssm"], P["tri"]]
    out_specs = [pl.BlockSpec((1, L, D_INNER), bmap)]
    out_shape = [jax.ShapeDtypeStruct((B, L, D_INNER), BF16)]
    if not has_h0:
        out_specs.append(pl.BlockSpec((1, 2, D_INNER, N_C), lambda b: (b, 0, 0, 0)))
        out_shape.append(jax.ShapeDtypeStruct((B, 2, D_INNER, N_C), F32))
    return pl.pallas_call(
        functools.partial(_ssd_kernel, L, has_h0),
        grid=(B,),
        in_specs=in_specs, out_specs=out_specs, out_shape=out_shape,
        scratch_shapes=[pltpu.VMEM((L + 16, CONV_DIM), F32), pltpu.VMEM((L, CONV_DIM), F32),
                        pltpu.VMEM((L, D_INNER), F32), pltpu.VMEM((L, LANES), F32),
                        pltpu.VMEM((L, LANES), F32)]
        + [pltpu.VMEM((L // CHUNK * 2 * H_C, LANES), F32)] * 3
        + [pltpu.VMEM((L // CHUNK * 8, LANES), F32), pltpu.VMEM((2 * N_C, D_INNER), F32)],
        compiler_params=pltpu.CompilerParams(dimension_semantics=("arbitrary",),
                                             vmem_limit_bytes=V7X_VMEM_LIMIT),
        name="ssd_lat" if has_h0 else "ssd_ctx",
    )(*args)


def _post_kernel(x_ref, oat_ref, oc_ref, mod_ref, wo_ref, g2_ref, w1_ref, w2_ref, o_ref):
    mod = mod_ref[0, 0]
    gate1 = mod[:, 2 * D_MODEL:3 * D_MODEL]
    shift2 = mod[:, 3 * D_MODEL:4 * D_MODEL]
    scale2 = mod[:, 4 * D_MODEL:5 * D_MODEL]
    gate2 = mod[:, 5 * D_MODEL:6 * D_MODEL]
    n_att = WVA + WD
    mix = _mm(oat_ref[...], wo_ref[0, 0:n_att, :]) + _mm(oc_ref[...], wo_ref[0, n_att:n_att + D_INNER, :])
    x1 = x_ref[...] + gate1 * mix
    h2 = (_rms(x1) * g2_ref[0] * (1.0 + scale2) + shift2).astype(BF16)
    ff = None
    for c in range(D_FF // FF_CHUNK):
        u = jnp.maximum(_mm(h2, w1_ref[0, :, c * FF_CHUNK:(c + 1) * FF_CHUNK]), 0.0)
        part = _mm((u * u).astype(BF16), w2_ref[0, c * FF_CHUNK:(c + 1) * FF_CHUNK, :])
        ff = part if ff is None else ff + part
    o_ref[...] = x1 + gate2 * ff


def _post_call(x2d, oat, oc, mod4, P, l, latent):
    T = x2d.shape[0]
    tm = TM_TOK
    tpb = DEC_SEQ // tm
    row = lambda i: (i, 0)
    lay = lambda i: (l, 0, 0)
    if latent:
        mod_map = lambda i: (l, i // tpb, 0, 0)
    else:
        mod_map = lambda i: (l, DEC_BATCH, 0, 0)
    single = pl.Buffered(1)
    return pl.pallas_call(
        _post_kernel,
        grid=(T // tm,),
        in_specs=[pl.BlockSpec((tm, D_MODEL), row),
                  pl.BlockSpec((tm, WVA + WD), row),
                  pl.BlockSpec((tm, D_INNER), row),
                  pl.BlockSpec((1, 1, 1, N_MOD * D_MODEL), mod_map),
                  pl.BlockSpec((1, D_MODEL, D_MODEL), lay, pipeline_mode=single),
                  pl.BlockSpec((1, 1, D_MODEL), lay),
                  pl.BlockSpec((1, D_MODEL, D_FF), lay, pipeline_mode=single),
                  pl.BlockSpec((1, D_FF, D_MODEL), lay, pipeline_mode=single)],
        out_specs=pl.BlockSpec((tm, D_MODEL), row),
        out_shape=jax.ShapeDtypeStruct((T, D_MODEL), F32),
        compiler_params=pltpu.CompilerParams(dimension_semantics=("arbitrary",),
                                             vmem_limit_bytes=V7X_VMEM_LIMIT),
        name="post_lat" if latent else "post_ctx",
    )(x2d, oat, oc, mod4, P["wout"], P["g2"], P["wff1"], P["wff2"])


def _rope_tables():
    t = np.arange(DEC_SEQ)
    half = ROPE_A // 2
    freqs = ROPE_BASE ** (-np.arange(0, half, 2, dtype=np.float64) / half)
    ang_r = (t // GRID_W)[:, None] * freqs
    ang_c = (t % GRID_W)[:, None] * freqs
    cos32 = np.concatenate([np.cos(ang_r), np.cos(ang_r), np.cos(ang_c), np.cos(ang_c)], axis=-1)
    sin32 = np.concatenate([-np.sin(ang_r), np.sin(ang_r), -np.sin(ang_c), np.sin(ang_c)], axis=-1)
    ones = np.ones((DEC_SEQ, NOPE_A))
    zeros = np.zeros((DEC_SEQ, NOPE_A))
    pad1 = np.ones((DEC_SEQ, LANES - QK_A))
    pad0 = np.zeros((DEC_SEQ, LANES - QK_A))
    cosa = np.tile(np.concatenate([ones, cos32, pad1], axis=-1), (1, H_A))
    sina = np.tile(np.concatenate([zeros, sin32, pad0], axis=-1), (1, H_A))
    cosd = np.tile(cos32, (1, WD // ROPE_A))
    sind = np.tile(sin32, (1, WD // ROPE_A))
    return tuple(jnp.asarray(a, F32) for a in (cosa, sina, cosd, sind))


def _constants():
    lane = np.arange(WD)
    bd32 = (lane[:, None] // DH_B == lane[None, :] // DH_B).astype(np.float32)
    bd64 = (lane[:, None] // (2 * DH_B) == lane[None, :] // (2 * DH_B)).astype(np.float32)
    i = np.arange(CHUNK)
    tri = np.stack([(i[None, :] <= i[:, None]), (i[None, :] >= i[:, None])]).astype(np.float32)
    return dict(bd32=jnp.asarray(bd32, BF16), bd64=jnp.asarray(bd64, BF16),
                tri=jnp.asarray(tri, BF16))


def _pad_last(a, n):
    return jnp.pad(a, [(0, 0)] * (a.ndim - 1) + [(0, n - a.shape[-1])])


def _prep_params(norm1_g, norm2_g, w_in, w_out, mla_q_norm_g, mla_kv_norm_g, w_uq, w_ukv, mla_qk_norm_q,
                 mla_qk_norm_k, diff_q_norm_g, diff_k_norm_g, diff_lq1, diff_lk1, diff_lq2, diff_lk2,
                 diff_subln_g, ssm_conv_w, ssm_conv_b, ssm_A_log, ssm_dt_bias, ssm_D, ssm_norm_g, w_ff1, w_ff2):
    P = _constants()
    offs = np.cumsum((Q_RANK, KV_RANK, ROPE_A, WD, WD, WD, D_INNER, CONV_DIM, 2 * H_C))
    o = [0] + offs.tolist()
    w_in16 = w_in.astype(BF16)
    sl = lambda i: w_in16[:, :, o[i]:o[i + 1]]
    P["win_a"] = jnp.concatenate([sl(0), sl(1), _pad_last(sl(2), LANES)], axis=-1)
    P["win_b"] = w_in16[:, :, o[3]:o[8]]
    P["win_c"] = _pad_last(sl(8), LANES)
    P["wuq"] = _pad_last(w_uq.reshape(DEPTH, Q_RANK, H_A, QK_A), LANES).reshape(DEPTH, Q_RANK, WA).astype(BF16)
    wkv = w_ukv.reshape(DEPTH, KV_RANK, H_A, NOPE_A + V_A)
    P["wuk"] = _pad_last(wkv[..., :NOPE_A], LANES).reshape(DEPTH, KV_RANK, WA).astype(BF16)
    P["wuv"] = wkv[..., NOPE_A:].reshape(DEPTH, KV_RANK, WVA).astype(BF16)
    P["wout"] = w_out.astype(BF16)
    P["wff1"] = w_ff1.astype(BF16)
    P["wff2"] = w_ff2.astype(BF16)
    vec = lambda a: a.reshape(DEPTH, 1, -1)
    P["g1"] = vec(norm1_g)
    P["g2"] = vec(norm2_g)
    P["gq"] = vec(mla_q_norm_g)
    P["gkv"] = vec(mla_kv_norm_g)
    P["gqa"] = vec(jnp.tile(_pad_last(mla_qk_norm_q, LANES), (1, H_A))) * (QK_A ** -0.5)
    P["gka"] = vec(jnp.tile(_pad_last(mla_qk_norm_k, LANES), (1, H_A)))
    P["gdq"] = vec(jnp.tile(diff_q_norm_g, (1, WD // DH_B))) * (DH_B ** -0.5)
    P["gdk"] = vec(jnp.tile(diff_k_norm_g, (1, WD // DH_B)))
    P["gsub"] = vec(jnp.tile(diff_subln_g, (1, H_B)))
    P["lq1"], P["lk1"], P["lq2"], P["lk2"] = vec(diff_lq1), vec(diff_lk1), vec(diff_lq2), vec(diff_lk2)
    P["cw"] = jnp.pad(ssm_conv_w, ((0, 0), (0, 8 - CONV_K), (0, 0)))
    P["cb"] = vec(ssm_conv_b)
    P["alog"] = vec(_pad_last(ssm_A_log.reshape(DEPTH, 2 * H_C), LANES))
    P["dtb"] = vec(_pad_last(ssm_dt_bias.reshape(DEPTH, 2 * H_C), LANES))
    P["dexp"] = vec(jnp.repeat(ssm_D, P_C, axis=-1))
    P["gssm"] = vec(ssm_norm_g)
    P["cosa"], P["sina"], P["cosd"], P["sind"] = _rope_tables()
    return P


def kernel(x_prompt, x_sample, cache_mla_ckv, cache_mla_krope, cache_diff_k, cache_diff_v, state_ssm, c, c_ctx, norm1_g, norm2_g, w_ada, b_ada, w_in, w_out, mla_q_norm_g, mla_kv_norm_g, w_uq, w_ukv, mla_qk_norm_q, mla_qk_norm_k, diff_q_norm_g, diff_k_norm_g, diff_lq1, diff_lk1, diff_lq2, diff_lk2, diff_subln_g, ssm_conv_w, ssm_conv_b, ssm_A_log, ssm_dt_bias, ssm_D, ssm_norm_g, w_ff1, w_ff2):
    P = _prep_params(norm1_g, norm2_g, w_in, w_out, mla_q_norm_g, mla_kv_norm_g, w_uq, w_ukv, mla_qk_norm_q,
                     mla_qk_norm_k, diff_q_norm_g, diff_k_norm_g, diff_lq1, diff_lk1, diff_lq2, diff_lk2,
                     diff_subln_g, ssm_conv_w, ssm_conv_b, ssm_A_log, ssm_dt_bias, ssm_D, ssm_norm_g,
                     w_ff1, w_ff2)

    cvec = jnp.concatenate([c, c_ctx[None, :], jnp.zeros((8 - DEC_BATCH - 1, D_MODEL), F32)], axis=0)
    mod4 = _mod_call(cvec, w_ada, b_ada).reshape(DEPTH, 8, 1, N_MOD * D_MODEL)

    kca, vca = _cache_call(cache_mla_ckv, _pad_last(cache_mla_krope, LANES), P)
    kcd = cache_diff_k.reshape(DEC_BATCH, DEPTH, PAST_LEN, WD).astype(BF16)
    vcd = cache_diff_v.reshape(DEC_BATCH, DEPTH, PAST_LEN, WD).astype(BF16)
    st = jnp.transpose(state_ssm, (0, 1, 2, 5, 3, 4)).reshape(DEC_BATCH, DEPTH, 2, N_C, D_INNER)
    half = D_INNER // G_C
    zero = jnp.zeros_like(st[..., :half])
    s0 = jnp.concatenate([jnp.concatenate([st[..., :half], zero], axis=-1),
                          jnp.concatenate([zero, st[..., half:]], axis=-1)], axis=-2)

    xp = x_prompt.reshape(BATCH * SEQ, D_MODEL)
    xs = x_sample.reshape(DEC_BATCH * DEC_SEQ, D_MODEL)
    ckv_l, krope_l, kd_l, vd_l, st_l = [], [], [], [], []
    for l in range(DEPTH):
        qa, ka, va, qd, kd, vd, z, xbc, dt, ckvn, krope, kd32, vd32 = _pre_call(xp, mod4, P, l, False)
        r3 = lambda a: a.reshape(BATCH, SEQ, a.shape[-1])
        oat = _attn_call(r3(qa), r3(ka), r3(va), r3(qd), r3(kd), r3(vd), None, P, l)
        oc, st_new = _ssd_call(r3(z), r3(xbc), r3(dt), None, P, l)
        xp = _post_call(xp, oat.reshape(BATCH * SEQ, -1), oc.reshape(BATCH * SEQ, -1), mod4, P, l, False)
        ckv_l.append(ckvn.reshape(BATCH, SEQ, KV_RANK))
        krope_l.append(krope.reshape(BATCH, SEQ, ROPE_A))
        kd_l.append(kd32.reshape(BATCH, SEQ, H_B, 2 * DH_B))
        vd_l.append(vd32.reshape(BATCH, SEQ, H_B, 2 * DH_B))
        st_l.append(st_new.reshape(BATCH, 2, H_C, P_C, N_C))
        qa, ka, va, qd, kd, vd, z, xbc, dt = _pre_call(xs, mod4, P, l, True)
        r3 = lambda a: a.reshape(DEC_BATCH, DEC_SEQ, a.shape[-1])
        oat = _attn_call(r3(qa), r3(ka), r3(va), r3(qd), r3(kd), r3(vd), (kca, vca, kcd, vcd), P, l)
        oc, = _ssd_call(r3(z), r3(xbc), r3(dt), s0, P, l)
        xs = _post_call(xs, oat.reshape(DEC_BATCH * DEC_SEQ, -1), oc.reshape(DEC_BATCH * DEC_SEQ, -1),
                        mod4, P, l, True)
    return (xp.reshape(BATCH, SEQ, D_MODEL), xs.reshape(DEC_BATCH, DEC_SEQ, D_MODEL),
            jnp.stack(ckv_l, axis=1), jnp.stack(krope_l, axis=1), jnp.stack(kd_l, axis=1),
            jnp.stack(vd_l, axis=1), jnp.stack(st_l, axis=1))
```

```python
import functools
import math

import numpy as np
import jax
import jax.numpy as jnp
from jax import lax
from jax.experimental import pallas as pl
from jax.experimental.pallas import tpu as pltpu

D_MODEL = 1024
BATCH = 16
SEQ = 256
DEPTH = 4
DEC_BATCH = 4
DEC_SEQ = 2048
PAST_LEN = 256
GRID_W = 64
ROPE_BASE = 10000.0
EPS = 1e-6
CHUNK = 128
D_FF = 4 * D_MODEL
N_MOD = 6
H_A = 4
NOPE_A = 64
ROPE_A = 32
V_A = 64
Q_RANK = 256
KV_RANK = 128
H_B = 4
DH_B = 32
H_C = 8
P_C = 64
N_C = 64
G_C = 2
CONV_K = 5
D_INNER = H_C * P_C
CONV_DIM = D_INNER + 2 * G_C * N_C

F32 = jnp.float32
BF16 = jnp.bfloat16

LANES = 128
QK_A = NOPE_A + ROPE_A
WA = H_A * LANES
WD = H_B * 2 * DH_B
WVA = H_A * V_A
C_CQ, C_CKV, C_KR, C_DQ, C_DK, C_DV, C_Z, C_XBC, C_DT = 0, 256, 384, 512, 768, 1024, 1280, 1792, 2560
NP_IN = 2688
V7X_VMEM_LIMIT = 56 * 1024 * 1024
TM_TOK = 512
TQ = 512
FF_CHUNK = 1024
CONV_COLS = 256


def _mm(a, b):
    return jnp.dot(a, b, preferred_element_type=F32)


def _mm_nt(a, b):
    return lax.dot_general(a, b, (((1,), (1,)), ((), ())), preferred_element_type=F32)


def _split3(x):
    hi = x.astype(BF16)
    r = x - hi.astype(F32)
    mid = r.astype(BF16)
    lo = (r - mid.astype(F32)).astype(BF16)
    return hi, mid, lo


def _sigmoid(x):
    return 1.0 / (1.0 + jnp.exp(-x))


def _rms(x, n=None):
    n = x.shape[-1] if n is None else n
    return x * lax.rsqrt(jnp.sum(x * x, axis=-1, keepdims=True) * (1.0 / n) + EPS)


def _rms_heads128(x, n_real):
    parts = [_rms(x[:, i * LANES:(i + 1) * LANES], n_real) for i in range(x.shape[1] // LANES)]
    return jnp.concatenate(parts, axis=-1)


def _rms_segments(x, bd, seg):
    sq = x * x
    hi = sq.astype(BF16)
    lo = (sq - hi.astype(F32)).astype(BF16)
    ssq = _mm(hi, bd) + _mm(lo, bd)
    return x * lax.rsqrt(ssq * (1.0 / seg) + EPS)


def _rope(x, cos, sin_signed):
    lane = lax.broadcasted_iota(jnp.int32, (1, LANES), 1)
    lo = (lane % 16) < 8
    parts = []
    for i in range(x.shape[1] // LANES):
        xb = x[:, i * LANES:(i + 1) * LANES]
        parts.append(jnp.where(lo, pltpu.roll(xb, LANES - 8, 1), pltpu.roll(xb, 8, 1)))
    return x * cos + jnp.concatenate(parts, axis=-1) * sin_signed


def _mod_kernel(c_ref, w_ref, b_ref, o_ref):
    cv = c_ref[...]
    s = cv * _sigmoid(cv)
    o_ref[0] = _mm(s.astype(BF16), w_ref[0].astype(BF16)) + b_ref[0]


def _mod_call(cvec, w_ada, b_ada):
    tn = 1536
    n_out = N_MOD * D_MODEL
    return pl.pallas_call(
        _mod_kernel,
        grid=(DEPTH, n_out // tn),
        in_specs=[pl.BlockSpec((8, D_MODEL), lambda l, j: (0, 0)),
                  pl.BlockSpec((1, D_MODEL, tn), lambda l, j: (l, 0, j)),
                  pl.BlockSpec((1, 1, tn), lambda l, j: (l, 0, j))],
        out_specs=pl.BlockSpec((1, 8, tn), lambda l, j: (l, 0, j)),
        out_shape=jax.ShapeDtypeStruct((DEPTH, 8, n_out), F32),
        compiler_params=pltpu.CompilerParams(dimension_semantics=("arbitrary", "arbitrary")),
        name="mod",
    )(cvec, w_ada, b_ada.reshape(DEPTH, 1, n_out))


def _tile_rope_key(kr):
    return jnp.concatenate([pltpu.roll(kr, NOPE_A, 1)] * H_A, axis=-1)


def _cache_kernel(ckv_ref, kr_ref, wuk_ref, wuv_ref, gka_ref, k_ref, v_ref):
    ckv = ckv_ref[0, 0].astype(BF16)
    kpre = _mm(ckv, wuk_ref[0]) + _tile_rope_key(kr_ref[0, 0])
    k_ref[0, 0] = (_rms_heads128(kpre, QK_A) * gka_ref[0]).astype(BF16)
    v_ref[0, 0] = _mm(ckv, wuv_ref[0]).astype(BF16)


def _cache_call(cache_ckv, cache_kr_pad, P):
    bl = lambda b, l: (b, l, 0, 0)
    lay = lambda b, l: (l, 0, 0)
    return pl.pallas_call(
        _cache_kernel,
        grid=(DEC_BATCH, DEPTH),
        in_specs=[pl.BlockSpec((1, 1, PAST_LEN, KV_RANK), bl),
                  pl.BlockSpec((1, 1, PAST_LEN, LANES), bl),
                  pl.BlockSpec((1, KV_RANK, WA), lay),
                  pl.BlockSpec((1, KV_RANK, WVA), lay),
                  pl.BlockSpec((1, 1, WA), lay)],
        out_specs=[pl.BlockSpec((1, 1, PAST_LEN, WA), bl),
                   pl.BlockSpec((1, 1, PAST_LEN, WVA), bl)],
        out_shape=[jax.ShapeDtypeStruct((DEC_BATCH, DEPTH, PAST_LEN, WA), BF16),
                   jax.ShapeDtypeStruct((DEC_BATCH, DEPTH, PAST_LEN, WVA), BF16)],
        compiler_params=pltpu.CompilerParams(dimension_semantics=("arbitrary", "arbitrary")),
        name="cache_expand",
    )(cache_ckv, cache_kr_pad, P["wuk"], P["wuv"], P["gka"])


def _pre_kernel(latent, *refs):
    (x_ref, mod_ref, g1_ref, wina_ref, winb_ref, winc_ref, gq_ref, gkv_ref, wuq_ref, wuk_ref, wuv_ref,
     gqa_ref, gka_ref, gdq_ref, gdk_ref, bd32_ref) = refs[:16]
    refs = refs[16:]
    if latent:
        cosa_ref, sina_ref, cosd_ref, sind_ref = refs[:4]
        refs = refs[4:]
    qa_ref, ka_ref, va_ref, qd_ref, kd_ref, vd_ref, z_ref, xbc_ref, dt_ref = refs[:9]
    refs = refs[9:]
    if not latent:
        ckvn_ref, krope_ref, kd32_ref, vd32_ref = refs

    x = x_ref[...]
    mod = mod_ref[0, 0]
    shift1 = mod[:, 0:D_MODEL]
    scale1 = mod[:, D_MODEL:2 * D_MODEL]
    h = (_rms(x) * g1_ref[0] * (1.0 + scale1) + shift1).astype(BF16)

    def proj(c0, c1):
        if c1 <= C_DQ:
            return _mm(h, wina_ref[0, :, c0:c1])
        if c0 >= C_DT:
            return _mm(h, winc_ref[0])
        return _mm_nt(h, winb_ref[0, c0 - C_DQ:c1 - C_DQ, :])

    cqn = (_rms(proj(C_CQ, C_CKV)) * gq_ref[0]).astype(BF16)
    q = _rms_heads128(_mm(cqn, wuq_ref[0]), QK_A) * gqa_ref[0]
    if latent:
        q = _rope(q, cosa_ref[...], sina_ref[...])
    qa_ref[...] = q.astype(BF16)

    ckvn = _rms(proj(C_CKV, C_KR)) * gkv_ref[0]
    kr = proj(C_KR, C_DQ)
    ckvb = ckvn.astype(BF16)
    k = _rms_heads128(_mm(ckvb, wuk_ref[0]) + _tile_rope_key(kr), QK_A) * gka_ref[0]
    if latent:
        k = _rope(k, cosa_ref[...], sina_ref[...])
    ka_ref[...] = k.astype(BF16)
    va_ref[...] = _mm(ckvb, wuv_ref[0]).astype(BF16)

    bd32 = bd32_ref[...]
    qd = _rms_segments(proj(C_DQ, C_DK), bd32, DH_B) * gdq_ref[0]
    kd = _rms_segments(proj(C_DK, C_DV), bd32, DH_B) * gdk_ref[0]
    vd = proj(C_DV, C_Z)
    if latent:
        qd = _rope(qd, cosd_ref[...], sind_ref[...])
        kd = _rope(kd, cosd_ref[...], sind_ref[...])
    qd_ref[...] = qd.astype(BF16)
    kd_ref[...] = kd.astype(BF16)
    vd_ref[...] = vd.astype(BF16)

    z_ref[...] = proj(C_Z, C_XBC)
    xbc_ref[...] = proj(C_XBC, C_DT)
    dt_ref[...] = proj(C_DT, NP_IN)

    if not latent:
        ckvn_ref[...] = ckvn
        krope_ref[...] = kr[:, 0:ROPE_A]
        kd32_ref[...] = kd.T
        vd32_ref[...] = vd.T


def _pre_call(x2d, mod4, P, l, latent):
    T = x2d.shape[0]
    tm = TM_TOK
    tpb = DEC_SEQ // tm
    row = lambda i: (i, 0)
    lay = lambda i: (l, 0, 0)
    if latent:
        mod_map = lambda i: (l, i // tpb, 0, 0)
    else:
        mod_map = lambda i: (l, DEC_BATCH, 0, 0)
    full = lambda i: (0, 0)
    in_specs = [pl.BlockSpec((tm, D_MODEL), row),
                pl.BlockSpec((1, 1, 1, N_MOD * D_MODEL), mod_map),
                pl.BlockSpec((1, 1, D_MODEL), lay),
                pl.BlockSpec((1, D_MODEL, C_DQ), lay),
                pl.BlockSpec((1, C_DT - C_DQ, D_MODEL), lay),
                pl.BlockSpec((1, D_MODEL, NP_IN - C_DT), lay),
                pl.BlockSpec((1, 1, Q_RANK), lay),
                pl.BlockSpec((1, 1, KV_RANK), lay),
                pl.BlockSpec((1, Q_RANK, WA), lay),
                pl.BlockSpec((1, KV_RANK, WA), lay),
                pl.BlockSpec((1, KV_RANK, WVA), lay),
                pl.BlockSpec((1, 1, WA), lay),
                pl.BlockSpec((1, 1, WA), lay),
                pl.BlockSpec((1, 1, WD), lay),
                pl.BlockSpec((1, 1, WD), lay),
                pl.BlockSpec((WD, WD), full)]
    args = [x2d, mod4, P["g1"], P["win_a"], P["win_b"], P["win_c"], P["gq"], P["gkv"], P["wuq"], P["wuk"], P["wuv"],
            P["gqa"], P["gka"], P["gdq"], P["gdk"], P["bd32"]]
    if latent:
        pos = lambda i: (i % tpb, 0)
        in_specs += [pl.BlockSpec((tm, WA), pos), pl.BlockSpec((tm, WA), pos),
                     pl.BlockSpec((tm, WD), pos), pl.BlockSpec((tm, WD), pos)]
        args += [P["cosa"], P["sina"], P["cosd"], P["sind"]]
    widths = [(WA, BF16), (WA, BF16), (WVA, BF16), (WD, BF16), (WD, BF16), (WD, BF16),
              (D_INNER, F32), (CONV_DIM, F32), (LANES, F32)]
    if not latent:
        widths += [(KV_RANK, F32), (ROPE_A, F32), (WD, F32), (WD, F32)]
    out_specs = [pl.BlockSpec((tm, w), row) for w, _ in widths]
    out_shape = [jax.ShapeDtypeStruct((T, w), dt) for w, dt in widths]
    if not latent:
        for i_t in (11, 12):
            out_specs[i_t] = pl.BlockSpec((WD, tm), lambda i: (0, i))
            out_shape[i_t] = jax.ShapeDtypeStruct((WD, T), F32)
    return pl.pallas_call(
        functools.partial(_pre_kernel, latent),
        grid=(T // tm,),
        in_specs=in_specs, out_specs=out_specs, out_shape=out_shape,
        compiler_params=pltpu.CompilerParams(dimension_semantics=("arbitrary",),
                                             vmem_limit_bytes=V7X_VMEM_LIMIT),
        name="pre_lat" if latent else "pre_ctx",
    )(*args)


def _softmax(q, key_refs, c0, c1):
    s = [_mm_nt(q, kr[:, c0:c1]) for kr in key_refs]
    m = s[0].max(axis=-1, keepdims=True)
    for si in s[1:]:
        m = jnp.maximum(m, si.max(axis=-1, keepdims=True))
    p = [jnp.exp(si - m) for si in s]
    den = p[0].sum(axis=-1, keepdims=True)
    for pi in p[1:]:
        den = den + pi.sum(axis=-1, keepdims=True)
    return p, den


def _pv(p, val_refs):
    pv = _mm(p[0].astype(BF16), val_refs[0][...])
    for pi, vr in zip(p[1:], val_refs[1:]):
        pv = pv + _mm(pi.astype(BF16), vr[...])
    return pv


def _attn_kernel(has_cache, lam_init, *refs):
    qa_ref, ka_ref, va_ref, qd_ref, kd_ref, vd_ref = refs[:6]
    refs = refs[6:]
    if has_cache:
        kca_ref, vca_ref, kcd_ref, vcd_ref = refs[:4]
        refs = refs[4:]
    lq1_ref, lk1_ref, lq2_ref, lk2_ref, gsub_ref, bd64_ref, o_ref = refs

    ka_refs = [ka_ref.at[0]]
    va_refs = [va_ref.at[0]]
    kd_refs = [kd_ref.at[0]]
    vd_refs = [vd_ref.at[0]]
    if has_cache:
        ka_refs = [kca_ref.at[0, 0]] + ka_refs
        va_refs = [vca_ref.at[0, 0]] + va_refs
        kd_refs = [kcd_ref.at[0, 0]] + kd_refs
        vd_refs = [vcd_ref.at[0, 0]] + vd_refs

    lane256 = lax.broadcasted_iota(jnp.int32, (1, WD), 1)
    lane128 = lax.broadcasted_iota(jnp.int32, (1, LANES), 1)

    qa = qa_ref[0]
    oa = None
    for hh in range(H_A):
        p, den = _softmax(qa[:, hh * LANES:(hh + 1) * LANES], ka_refs, hh * LANES, (hh + 1) * LANES)
        contrib = jnp.where(lane256 // V_A == hh, _pv(p, va_refs) * (1.0 / den), 0.0)
        oa = contrib if oa is None else oa + contrib

    lam = (jnp.exp(jnp.sum(lq1_ref[0] * lk1_ref[0], axis=-1, keepdims=True))
           - jnp.exp(jnp.sum(lq2_ref[0] * lk2_ref[0], axis=-1, keepdims=True)) + lam_init)
    qd = qd_ref[0]
    od = None
    for hh in range(H_B):
        blk = hh // 2
        qblk = qd[:, blk * LANES:(blk + 1) * LANES]
        maps = []
        for mm_i in range(2):
            seg = (hh % 2) * 2 + mm_i
            qm = jnp.where(lane128 // DH_B == seg, qblk, jnp.zeros_like(qblk))
            p, den = _softmax(qm, kd_refs, blk * LANES, (blk + 1) * LANES)
            maps.append(_pv(p, vd_refs) * (1.0 / den))
        contrib = jnp.where(lane256 // (2 * DH_B) == hh, maps[0] - lam * maps[1], 0.0)
        od = contrib if od is None else od + contrib
    od = _rms_segments(od, bd64_ref[...], 2 * DH_B) * (gsub_ref[0] * (1.0 - lam_init))
    o_ref[0] = jnp.concatenate([oa, od], axis=-1).astype(BF16)


def _attn_call(qa, ka, va, qd, kd, vd, caches, P, l):
    B, L = qa.shape[0], qa.shape[1]
    has_cache = caches is not None
    lam_init = 0.8 - 0.6 * math.exp(-0.3 * l)
    qmap = lambda b, i: (b, i, 0)
    kmap = lambda b, i: (b, 0, 0)
    lay = lambda b, i: (l, 0, 0)
    tq = min(TQ, L)
    in_specs = [pl.BlockSpec((1, tq, WA), qmap), pl.BlockSpec((1, L, WA), kmap),
                pl.BlockSpec((1, L, WVA), kmap), pl.BlockSpec((1, tq, WD), qmap),
                pl.BlockSpec((1, L, WD), kmap), pl.BlockSpec((1, L, WD), kmap)]
    args = [qa, ka, va, qd, kd, vd]
    if has_cache:
        cmap = lambda b, i: (b, l, 0, 0)
        in_specs += [pl.BlockSpec((1, 1, PAST_LEN, WA), cmap), pl.BlockSpec((1, 1, PAST_LEN, WVA), cmap),
                     pl.BlockSpec((1, 1, PAST_LEN, WD), cmap), pl.BlockSpec((1, 1, PAST_LEN, WD), cmap)]
        args += list(caches)
    in_specs += [pl.BlockSpec((1, 1, DH_B), lay)] * 4
    in_specs += [pl.BlockSpec((1, 1, WD), lay), pl.BlockSpec((WD, WD), lambda b, i: (0, 0))]
    args += [P["lq1"], P["lk1"], P["lq2"], P["lk2"], P["gsub"], P["bd64"]]
    return pl.pallas_call(
        functools.partial(_attn_kernel, has_cache, lam_init),
        grid=(B, L // tq),
        in_specs=in_specs,
        out_specs=pl.BlockSpec((1, tq, WVA + WD), qmap),
        out_shape=jax.ShapeDtypeStruct((B, L, WVA + WD), BF16),
        compiler_params=pltpu.CompilerParams(dimension_semantics=("arbitrary", "arbitrary"),
                                             vmem_limit_bytes=V7X_VMEM_LIMIT),
        name="attn_lat" if has_cache else "attn_ctx",
    )(*args)


def _ssd_kernel(L, has_h0, *refs):
    z_ref, xbc_ref, dt_ref = refs[:3]
    refs = refs[3:]
    if has_h0:
        s0_ref = refs[0]
        refs = refs[1:]
    cw_ref, cb_ref, alog_ref, dtb_ref, dexp_ref, g_ref, tri_ref = refs[:7]
    refs = refs[7:]
    o_ref = refs[0]
    refs = refs[1:]
    if not has_h0:
        st_ref = refs[0]
        refs = refs[1:]
    xpad, xc, yacc, cum_s, bt_s, cumt_s, dtt_s, wt_s, el_s, s_scr = refs
    nc = L // CHUNK
    halo = 8
    nd = 2 * H_C

    xpad[0:halo, :] = jnp.zeros((halo, CONV_DIM), F32)
    xpad[L + halo:L + 2 * halo, :] = jnp.zeros((halo, CONV_DIM), F32)
    xpad[halo:L + halo, :] = xbc_ref[0]
    cw = cw_ref[0]
    cb = cb_ref[0]
    dtb = dtb_ref[0]
    a_neg = -jnp.exp(alog_ref[0])
    lane128 = lax.broadcasted_iota(jnp.int32, (1, LANES), 1)
    fwd_lane = lane128 < H_C
    tri_f = tri_ref[0]
    tri_b = tri_ref[1]

    static_prep = nc <= 2

    def prep_body(c, carry):
        base = c * CHUNK if static_prep else pl.multiple_of(c * CHUNK, CHUNK)
        accs = []
        for g0 in range(0, CONV_DIM, CONV_COLS):
            a = cb[:, g0:g0 + CONV_COLS]
            if not static_prep:
                win = xpad[pl.ds(base, CHUNK + 2 * halo), g0:g0 + CONV_COLS]
            for k in range(CONV_K):
                off = halo - CONV_K // 2 + k
                if static_prep:
                    tap = xpad[base + off:base + off + CHUNK, g0:g0 + CONV_COLS]
                else:
                    tap = win[off:off + CHUNK, :]
                a = a + tap * cw[k:k + 1, g0:g0 + CONV_COLS]
            accs.append(a)
        acc = jnp.concatenate(accs, axis=-1)
        act = acc * _sigmoid(acc)
        xc[pl.ds(base, CHUNK), :] = act
        bt_s[pl.ds(base, CHUNK), :] = act[:, D_INNER:D_INNER + LANES].T
        dtr = dt_ref[0, pl.ds(base, CHUNK), :] + dtb
        dtc = jnp.maximum(dtr, 0.0) + jnp.log(1.0 + jnp.exp(-jnp.abs(dtr)))
        hi, mid, lo = _split3(dtc * a_neg)
        cum_f = _mm(tri_f, hi) + _mm(tri_f, mid) + _mm(tri_f, lo)
        cum_b = _mm(tri_b, hi) + _mm(tri_b, mid) + _mm(tri_b, lo)
        cum = jnp.where(fwd_lane, cum_f, cum_b)
        last = jnp.where(fwd_lane, cum[CHUNK - 1:CHUNK, :], cum[0:1, :])
        cum_s[pl.ds(base, CHUNK), :] = cum
        rows = pl.ds(c * nd, nd) if static_prep else pl.ds(pl.multiple_of(c * nd, nd), nd)
        cumt_s[rows, :] = cum.T[0:nd, :]
        dtt_s[rows, :] = dtc.T[0:nd, :]
        wt_s[rows, :] = (jnp.exp(last - cum) * dtc).T[0:nd, :]
        el_rows = pl.ds(c * 8, 8) if static_prep else pl.ds(pl.multiple_of(c * 8, 8), 8)
        el_s[el_rows, :] = jnp.broadcast_to(jnp.exp(last), (8, LANES))
        return carry

    if static_prep:
        for c_static in range(nc):
            prep_body(c_static, 0)
    else:
        lax.fori_loop(0, nc, prep_body, 0, unroll=2)

    row_i = lax.broadcasted_iota(jnp.int32, (CHUNK, CHUNK), 0)
    col_j = lax.broadcasted_iota(jnp.int32, (CHUNK, CHUNK), 1)
    blockmask = (lax.broadcasted_iota(jnp.int32, (2 * N_C, D_INNER), 0) // N_C
                 == lax.broadcasted_iota(jnp.int32, (2 * N_C, D_INNER), 1) // (D_INNER // G_C))
    dexp = dexp_ref[0]
    gnorm = g_ref[0]

    for d in range(2):
        if has_h0:
            s_scr[...] = s0_ref[0, 0, d]
        else:
            s_scr[...] = jnp.zeros((2 * N_C, D_INNER), F32)
        causal = (row_i >= col_j) if d == 0 else (col_j >= row_i)

        def per_head_lanes(cols, d=d):
            parts = []
            for pair in range(H_C // 2):
                i0 = d * H_C + 2 * pair
                parts.append(jnp.where(lane128 < P_C, cols[:, i0:i0 + 1], cols[:, i0 + 1:i0 + 2]))
            return jnp.concatenate(parts, axis=-1)

        def chunk_body(step, carry, d=d, causal=causal, per_head_lanes=per_head_lanes):
            c = step if d == 0 else nc - 1 - step
            base = pl.multiple_of(c * CHUNK, CHUNK)
            rows = pl.ds(pl.multiple_of(c * nd, nd), nd)
            xs = xc[pl.ds(base, CHUNK), 0:D_INNER]
            c_c = xc[pl.ds(base, CHUNK), D_INNER + LANES:D_INNER + 2 * LANES]
            cum = cum_s[pl.ds(base, CHUNK), :]
            cum_t = cumt_s[rows, :]
            dt_t = dtt_s[rows, :]
            w_t = wt_s[rows, :]
            b_t = bt_s[pl.ds(base, CHUNK), :]
            cb16 = c_c.astype(BF16)
            bt16 = b_t.astype(BF16)
            cbs = [_mm(jnp.where(lane128 // N_C == g, cb16, jnp.zeros_like(cb16)), bt16)
                   for g in range(G_C)]
            sv = s_scr[...]

            def blockdiag(v):
                return jnp.concatenate([jnp.where(lane128 < P_C, v, 0.0),
                                        jnp.where(lane128 < P_C, 0.0, v)], axis=0).astype(BF16)

            ys, upds = [], []
            for pair in range(H_C // 2):
                xbd = blockdiag(xs[:, pair * LANES:(pair + 1) * LANES])
                sbd = blockdiag(sv[:, pair * LANES:(pair + 1) * LANES])
                ms, es, ws = [], [], []
                for hh in (2 * pair, 2 * pair + 1):
                    idx = d * H_C + hh
                    col = jnp.broadcast_to(cum[:, idx:idx + 1], (CHUNK, CHUNK))
                    dec = jnp.where(causal, jnp.exp(col - cum_t[idx:idx + 1, :]), 0.0)
                    ms.append((cbs[hh // (H_C // G_C)] * dec * dt_t[idx:idx + 1, :]).astype(BF16))
                    es.append((jnp.exp(col) * c_c).astype(BF16))
                    ws.append((b_t * w_t[idx:idx + 1, :]).astype(BF16))
                ys.append(_mm(jnp.concatenate(ms + es, axis=-1), jnp.concatenate([xbd, sbd], axis=0)))
                upds.append(_mm(jnp.concatenate(ws, axis=-1), xbd))
            y = jnp.concatenate(ys, axis=-1)
            cd = per_head_lanes(el_s[pl.ds(pl.multiple_of(c * 8, 8), 1), :])
            s_scr[...] = sv * cd + jnp.where(blockmask, jnp.concatenate(upds, axis=-1), 0.0)
            if d == 0:
                yacc[pl.ds(base, CHUNK), :] = y
            else:
                zc = z_ref[0, pl.ds(base, CHUNK), :]
                yt = (yacc[pl.ds(base, CHUNK), :] + y + dexp * xs) * (zc * _sigmoid(zc))
                o_ref[0, pl.ds(base, CHUNK), :] = (_rms(yt) * gnorm).astype(BF16)
            return carry

        lax.fori_loop(0, nc, chunk_body, 0, unroll=2)
        if not has_h0:
            st = s_scr[...].T
            st_ref[0, d] = (st + pltpu.roll(st, N_C, 1))[:, 0:N_C]


def _ssd_call(z, xbc, dt, s0, P, l):
    B, L = z.shape[0], z.shape[1]
    has_h0 = s0 is not None
    bmap = lambda b: (b, 0, 0)
    lay = lambda b: (l, 0, 0)
    in_specs = [pl.BlockSpec((1, L, D_INNER), bmap), pl.BlockSpec((1, L, CONV_DIM), bmap),
                pl.BlockSpec((1, L, LANES), bmap)]
    args = [z, xbc, dt]
    if has_h0:
        in_specs.append(pl.BlockSpec((1, 1, 2, 2 * N_C, D_INNER), lambda b: (b, l, 0, 0, 0)))
        args.append(s0)
    in_specs += [pl.BlockSpec((1, 8, CONV_DIM), lay), pl.BlockSpec((1, 1, CONV_DIM), lay),
                 pl.BlockSpec((1, 1, LANES), lay), pl.BlockSpec((1, 1, LANES), lay),
                 pl.BlockSpec((1, 1, D_INNER), lay), pl.BlockSpec((1, 1, D_INNER), lay),
                 pl.BlockSpec((2, CHUNK, CHUNK), lambda b: (0, 0, 0))]
    args += [P["cw"], P["cb"], P["alog"], P["dtb"], P["dexp"], P["gssm"], P["tri"]]
    out_specs = [pl.BlockSpec((1, L, D_INNER), bmap)]
    out_shape = [jax.ShapeDtypeStruct((B, L, D_INNER), BF16)]
    if not has_h0:
        out_specs.append(pl.BlockSpec((1, 2, D_INNER, N_C), lambda b: (b, 0, 0, 0)))
        out_shape.append(jax.ShapeDtypeStruct((B, 2, D_INNER, N_C), F32))
    return pl.pallas_call(
        functools.partial(_ssd_kernel, L, has_h0),
        grid=(B,),
        in_specs=in_specs, out_specs=out_specs, out_shape=out_shape,
        scratch_shapes=[pltpu.VMEM((L + 16, CONV_DIM), F32), pltpu.VMEM((L, CONV_DIM), F32),
                        pltpu.VMEM((L, D_INNER), F32), pltpu.VMEM((L, LANES), F32),
                        pltpu.VMEM((L, LANES), F32)]
        + [pltpu.VMEM((L // CHUNK * 2 * H_C, LANES), F32)] * 3
        + [pltpu.VMEM((L // CHUNK * 8, LANES), F32), pltpu.VMEM((2 * N_C, D_INNER), F32)],
        compiler_params=pltpu.CompilerParams(dimension_semantics=("arbitrary",),
                                             vmem_limit_bytes=V7X_VMEM_LIMIT),
        name="ssd_lat" if has_h0 else "ssd_ctx",
    )(*args)


def _post_kernel(x_ref, oat_ref, oc_ref, mod_ref, wo_ref, g2_ref, w1_ref, w2_ref, o_ref):
    mod = mod_ref[0, 0]
    gate1 = mod[:, 2 * D_MODEL:3 * D_MODEL]
    shift2 = mod[:, 3 * D_MODEL:4 * D_MODEL]
    scale2 = mod[:, 4 * D_MODEL:5 * D_MODEL]
    gate2 = mod[:, 5 * D_MODEL:6 * D_MODEL]
    n_att = WVA + WD
    mix = _mm(oat_ref[...], wo_ref[0, 0:n_att, :]) + _mm(oc_ref[...], wo_ref[0, n_att:n_att + D_INNER, :])
    x1 = x_ref[...] + gate1 * mix
    h2 = (_rms(x1) * g2_ref[0] * (1.0 + scale2) + shift2).astype(BF16)
    ff = None
    for c in range(D_FF // FF_CHUNK):
        u = jnp.maximum(_mm(h2, w1_ref[0, :, c * FF_CHUNK:(c + 1) * FF_CHUNK]), 0.0)
        part = _mm((u * u).astype(BF16), w2_ref[0, c * FF_CHUNK:(c + 1) * FF_CHUNK, :])
        ff = part if ff is None else ff + part
    o_ref[...] = x1 + gate2 * ff


def _post_call(x2d, oat, oc, mod4, P, l, latent):
    T = x2d.shape[0]
    tm = TM_TOK
    tpb = DEC_SEQ // tm
    row = lambda i: (i, 0)
    lay = lambda i: (l, 0, 0)
    if latent:
        mod_map = lambda i: (l, i // tpb, 0, 0)
    else:
        mod_map = lambda i: (l, DEC_BATCH, 0, 0)
    single = pl.Buffered(1)
    return pl.pallas_call(
        _post_kernel,
        grid=(T // tm,),
        in_specs=[pl.BlockSpec((tm, D_MODEL), row),
                  pl.BlockSpec((tm, WVA + WD), row),
                  pl.BlockSpec((tm, D_INNER), row),
                  pl.BlockSpec((1, 1, 1, N_MOD * D_MODEL), mod_map),
                  pl.BlockSpec((1, D_MODEL, D_MODEL), lay, pipeline_mode=single),
                  pl.BlockSpec((1, 1, D_MODEL), lay),
                  pl.BlockSpec((1, D_MODEL, D_FF), lay, pipeline_mode=single),
                  pl.BlockSpec((1, D_FF, D_MODEL), lay, pipeline_mode=single)],
        out_specs=pl.BlockSpec((tm, D_MODEL), row),
        out_shape=jax.ShapeDtypeStruct((T, D_MODEL), F32),
        compiler_params=pltpu.CompilerParams(dimension_semantics=("arbitrary",),
                                             vmem_limit_bytes=V7X_VMEM_LIMIT),
        name="post_lat" if latent else "post_ctx",
    )(x2d, oat, oc, mod4, P["wout"], P["g2"], P["wff1"], P["wff2"])


def _rope_tables():
    t = np.arange(DEC_SEQ)
    half = ROPE_A // 2
    freqs = ROPE_BASE ** (-np.arange(0, half, 2, dtype=np.float64) / half)
    ang_r = (t // GRID_W)[:, None] * freqs
    ang_c = (t % GRID_W)[:, None] * freqs
    cos32 = np.concatenate([np.cos(ang_r), np.cos(ang_r), np.cos(ang_c), np.cos(ang_c)], axis=-1)
    sin32 = np.concatenate([-np.sin(ang_r), np.sin(ang_r), -np.sin(ang_c), np.sin(ang_c)], axis=-1)
    ones = np.ones((DEC_SEQ, NOPE_A))
    zeros = np.zeros((DEC_SEQ, NOPE_A))
    pad1 = np.ones((DEC_SEQ, LANES - QK_A))
    pad0 = np.zeros((DEC_SEQ, LANES - QK_A))
    cosa = np.tile(np.concatenate([ones, cos32, pad1], axis=-1), (1, H_A))
    sina = np.tile(np.concatenate([zeros, sin32, pad0], axis=-1), (1, H_A))
    cosd = np.tile(cos32, (1, WD // ROPE_A))
    sind = np.tile(sin32, (1, WD // ROPE_A))
    return tuple(jnp.asarray(a, F32) for a in (cosa, sina, cosd, sind))


def _constants():
    lane = np.arange(WD)
    bd32 = (lane[:, None] // DH_B == lane[None, :] // DH_B).astype(np.float32)
    bd64 = (lane[:, None] // (2 * DH_B) == lane[None, :] // (2 * DH_B)).astype(np.float32)
    i = np.arange(CHUNK)
    tri = np.stack([(i[None, :] <= i[:, None]), (i[None, :] >= i[:, None])]).astype(np.float32)
    return dict(bd32=jnp.asarray(bd32, BF16), bd64=jnp.asarray(bd64, BF16),
                tri=jnp.asarray(tri, BF16))


def _pad_last(a, n):
    return jnp.pad(a, [(0, 0)] * (a.ndim - 1) + [(0, n - a.shape[-1])])


def _prep_params(norm1_g, norm2_g, w_in, w_out, mla_q_norm_g, mla_kv_norm_g, w_uq, w_ukv, mla_qk_norm_q,
                 mla_qk_norm_k, diff_q_norm_g, diff_k_norm_g, diff_lq1, diff_lk1, diff_lq2, diff_lk2,
                 diff_subln_g, ssm_conv_w, ssm_conv_b, ssm_A_log, ssm_dt_bias, ssm_D, ssm_norm_g, w_ff1, w_ff2):
    P = _constants()
    offs = np.cumsum((Q_RANK, KV_RANK, ROPE_A, WD, WD, WD, D_INNER, CONV_DIM, 2 * H_C))
    o = [0] + offs.tolist()
    w_in16 = w_in.astype(BF16)
    sl = lambda i: w_in16[:, :, o[i]:o[i + 1]]
    P["win_a"] = jnp.concatenate([sl(0), sl(1), _pad_last(sl(2), LANES)], axis=-1)
    P["win_b"] = jnp.swapaxes(w_in16[:, :, o[3]:o[8]], 1, 2)
    P["win_c"] = _pad_last(sl(8), LANES)
    P["wuq"] = _pad_last(w_uq.reshape(DEPTH, Q_RANK, H_A, QK_A), LANES).reshape(DEPTH, Q_RANK, WA).astype(BF16)
    wkv = w_ukv.reshape(DEPTH, KV_RANK, H_A, NOPE_A + V_A)
    P["wuk"] = _pad_last(wkv[..., :NOPE_A], LANES).reshape(DEPTH, KV_RANK, WA).astype(BF16)
    P["wuv"] = wkv[..., NOPE_A:].reshape(DEPTH, KV_RANK, WVA).astype(BF16)
    P["wout"] = w_out.astype(BF16)
    P["wff1"] = w_ff1.astype(BF16)
    P["wff2"] = w_ff2.astype(BF16)
    vec = lambda a: a.reshape(DEPTH, 1, -1)
    P["g1"] = vec(norm1_g)
    P["g2"] = vec(norm2_g)
    P["gq"] = vec(mla_q_norm_g)
    P["gkv"] = vec(mla_kv_norm_g)
    P["gqa"] = vec(jnp.tile(_pad_last(mla_qk_norm_q, LANES), (1, H_A))) * (QK_A ** -0.5)
    P["gka"] = vec(jnp.tile(_pad_last(mla_qk_norm_k, LANES), (1, H_A)))
    P["gdq"] = vec(jnp.tile(diff_q_norm_g, (1, WD // DH_B))) * (DH_B ** -0.5)
    P["gdk"] = vec(jnp.tile(diff_k_norm_g, (1, WD // DH_B)))
    P["gsub"] = vec(jnp.tile(diff_subln_g, (1, H_B)))
    P["lq1"], P["lk1"], P["lq2"], P["lk2"] = vec(diff_lq1), vec(diff_lk1), vec(diff_lq2), vec(diff_lk2)
    P["cw"] = jnp.pad(ssm_conv_w, ((0, 0), (0, 8 - CONV_K), (0, 0)))
    P["cb"] = vec(ssm_conv_b)
    P["alog"] = vec(_pad_last(ssm_A_log.reshape(DEPTH, 2 * H_C), LANES))
    P["dtb"] = vec(_pad_last(ssm_dt_bias.reshape(DEPTH, 2 * H_C), LANES))
    P["dexp"] = vec(jnp.repeat(ssm_D, P_C, axis=-1))
    P["gssm"] = vec(ssm_norm_g)
    P["cosa"], P["sina"], P["cosd"], P["sind"] = _rope_tables()
    return P


def kernel(x_prompt, x_sample, cache_mla_ckv, cache_mla_krope, cache_diff_k, cache_diff_v, state_ssm, c, c_ctx, norm1_g, norm2_g, w_ada, b_ada, w_in, w_out, mla_q_norm_g, mla_kv_norm_g, w_uq, w_ukv, mla_qk_norm_q, mla_qk_norm_k, diff_q_norm_g, diff_k_norm_g, diff_lq1, diff_lk1, diff_lq2, diff_lk2, diff_subln_g, ssm_conv_w, ssm_conv_b, ssm_A_log, ssm_dt_bias, ssm_D, ssm_norm_g, w_ff1, w_ff2):
    P = _prep_params(norm1_g, norm2_g, w_in, w_out, mla_q_norm_g, mla_kv_norm_g, w_uq, w_ukv, mla_qk_norm_q,
                     mla_qk_norm_k, diff_q_norm_g, diff_k_norm_g, diff_lq1, diff_lk1, diff_lq2, diff_lk2,
                     diff_subln_g, ssm_conv_w, ssm_conv_b, ssm_A_log, ssm_dt_bias, ssm_D, ssm_norm_g,
                     w_ff1, w_ff2)

    cvec = jnp.concatenate([c, c_ctx[None, :], jnp.zeros((8 - DEC_BATCH - 1, D_MODEL), F32)], axis=0)
    mod4 = _mod_call(cvec, w_ada, b_ada).reshape(DEPTH, 8, 1, N_MOD * D_MODEL)

    kca, vca = _cache_call(cache_mla_ckv, _pad_last(cache_mla_krope, LANES), P)
    kcd = cache_diff_k.reshape(DEC_BATCH, DEPTH, PAST_LEN, WD).astype(BF16)
    vcd = cache_diff_v.reshape(DEC_BATCH, DEPTH, PAST_LEN, WD).astype(BF16)
    st = jnp.transpose(state_ssm, (0, 1, 2, 5, 3, 4)).reshape(DEC_BATCH, DEPTH, 2, N_C, D_INNER)
    half = D_INNER // G_C
    zero = jnp.zeros_like(st[..., :half])
    s0 = jnp.concatenate([jnp.concatenate([st[..., :half], zero], axis=-1),
                          jnp.concatenate([zero, st[..., half:]], axis=-1)], axis=-2)

    xp = x_prompt.reshape(BATCH * SEQ, D_MODEL)
    xs = x_sample.reshape(DEC_BATCH * DEC_SEQ, D_MODEL)
    ckv_l, krope_l, kd_l, vd_l, st_l = [], [], [], [], []
    for l in range(DEPTH):
        qa, ka, va, qd, kd, vd, z, xbc, dt, ckvn, krope, kd32, vd32 = _pre_call(xp, mod4, P, l, False)
        r3 = lambda a: a.reshape(BATCH, SEQ, a.shape[-1])
        oat = _attn_call(r3(qa), r3(ka), r3(va), r3(qd), r3(kd), r3(vd), None, P, l)
        oc, st_new = _ssd_call(r3(z), r3(xbc), r3(dt), None, P, l)
        xp = _post_call(xp, oat.reshape(BATCH * SEQ, -1), oc.reshape(BATCH * SEQ, -1), mod4, P, l, False)
        ckv_l.append(ckvn.reshape(BATCH, SEQ, KV_RANK))
        krope_l.append(krope.reshape(BATCH, SEQ, ROPE_A))
        kd_l.append(kd32.T.reshape(BATCH, SEQ, H_B, 2 * DH_B))
        vd_l.append(vd32.T.reshape(BATCH, SEQ, H_B, 2 * DH_B))
        st_l.append(st_new.reshape(BATCH, 2, H_C, P_C, N_C))
        qa, ka, va, qd, kd, vd, z, xbc, dt = _pre_call(xs, mod4, P, l, True)
        r3 = lambda a: a.reshape(DEC_BATCH, DEC_SEQ, a.shape[-1])
        oat = _attn_call(r3(qa), r3(ka), r3(va), r3(qd), r3(kd), r3(vd), (kca, vca, kcd, vcd), P, l)
        oc, = _ssd_call(r3(z), r3(xbc), r3(dt), s0, P, l)
        xs = _post_call(xs, oat.reshape(DEC_BATCH * DEC_SEQ, -1), oc.reshape(DEC_BATCH * DEC_SEQ, -1),
                        mod4, P, l, True)
    return (xp.reshape(BATCH, SEQ, D_MODEL), xs.reshape(DEC_BATCH, DEC_SEQ, D_MODEL),
            jnp.stack(ckv_l, axis=1), jnp.stack(krope_l, axis=1), jnp.stack(kd_l, axis=1),
            jnp.stack(vd_l, axis=1), jnp.stack(st_l, axis=1))
```

```python
import functools
import math

import numpy as np
import jax
import jax.numpy as jnp
from jax import lax
from jax.experimental import pallas as pl
from jax.experimental.pallas import tpu as pltpu

D_MODEL = 1024
BATCH = 16
SEQ = 256
DEPTH = 4
DEC_BATCH = 4
DEC_SEQ = 2048
PAST_LEN = 256
GRID_W = 64
ROPE_BASE = 10000.0
EPS = 1e-6
CHUNK = 128
D_FF = 4 * D_MODEL
N_MOD = 6
H_A = 4
NOPE_A = 64
ROPE_A = 32
V_A = 64
Q_RANK = 256
KV_RANK = 128
H_B = 4
DH_B = 32
H_C = 8
P_C = 64
N_C = 64
G_C = 2
CONV_K = 5
D_INNER = H_C * P_C
CONV_DIM = D_INNER + 2 * G_C * N_C

F32 = jnp.float32
BF16 = jnp.bfloat16

LANES = 128
QK_A = NOPE_A + ROPE_A
WA = H_A * LANES
WD = H_B * 2 * DH_B
WVA = H_A * V_A
R_CQ, R_CKV, R_KR, R_DQ, R_DK, R_DV, R_Z, R_XBC, R_DT, D_IN_PROJ = (
    0, 256, 384, 416, 672, 928, 1184, 1696, 2464, 2480)
V7X_VMEM_LIMIT = 56 * 1024 * 1024
TM_TOK = 512
TM_PRE = 1024
TQ = 512
FF_CHUNK = 1024
CONV_COLS = 256


def _mm(a, b):
    return jnp.dot(a, b, preferred_element_type=F32)


def _mm_nt(a, b):
    return lax.dot_general(a, b, (((1,), (1,)), ((), ())), preferred_element_type=F32)


def _split3(x):
    hi = x.astype(BF16)
    r = x - hi.astype(F32)
    mid = r.astype(BF16)
    lo = (r - mid.astype(F32)).astype(BF16)
    return hi, mid, lo


def _sigmoid(x):
    return 1.0 / (1.0 + jnp.exp(-x))


def _rms(x, n=None):
    n = x.shape[-1] if n is None else n
    return x * lax.rsqrt(jnp.sum(x * x, axis=-1, keepdims=True) * (1.0 / n) + EPS)


def _rms_heads128(x, n_real):
    parts = [_rms(x[:, i * LANES:(i + 1) * LANES], n_real) for i in range(x.shape[1] // LANES)]
    return jnp.concatenate(parts, axis=-1)


def _rms_segments(x, bd, seg):
    sq = x * x
    hi = sq.astype(BF16)
    lo = (sq - hi.astype(F32)).astype(BF16)
    ssq = _mm(hi, bd) + _mm(lo, bd)
    return x * lax.rsqrt(ssq * (1.0 / seg) + EPS)


def _rope(x, cos, sin_signed):
    lane = lax.broadcasted_iota(jnp.int32, (1, LANES), 1)
    lo = (lane % 16) < 8
    parts = []
    for i in range(x.shape[1] // LANES):
        xb = x[:, i * LANES:(i + 1) * LANES]
        parts.append(jnp.where(lo, pltpu.roll(xb, LANES - 8, 1), pltpu.roll(xb, 8, 1)))
    return x * cos + jnp.concatenate(parts, axis=-1) * sin_signed


def _mod_kernel(c_ref, w_ref, b_ref, o_ref):
    cv = c_ref[...]
    s = cv * _sigmoid(cv)
    o_ref[0] = _mm(s.astype(BF16), w_ref[0].astype(BF16)) + b_ref[0]


def _mod_call(cvec, w_ada, b_ada):
    tn = 1536
    n_out = N_MOD * D_MODEL
    return pl.pallas_call(
        _mod_kernel,
        grid=(DEPTH, n_out // tn),
        in_specs=[pl.BlockSpec((8, D_MODEL), lambda l, j: (0, 0)),
                  pl.BlockSpec((1, D_MODEL, tn), lambda l, j: (l, 0, j)),
                  pl.BlockSpec((1, 1, tn), lambda l, j: (l, 0, j))],
        out_specs=pl.BlockSpec((1, 8, tn), lambda l, j: (l, 0, j)),
        out_shape=jax.ShapeDtypeStruct((DEPTH, 8, n_out), F32),
        compiler_params=pltpu.CompilerParams(dimension_semantics=("arbitrary", "arbitrary")),
        name="mod",
    )(cvec, w_ada, b_ada.reshape(DEPTH, 1, n_out))


def _tile_rope_key(kr):
    return jnp.concatenate([pltpu.roll(kr, NOPE_A, 1)] * H_A, axis=-1)


def _cache_kernel(ckv_ref, kr_ref, wuk_ref, wuv_ref, gka_ref, k_ref, v_ref):
    ckv = ckv_ref[0, 0].astype(BF16)
    kpre = _mm(ckv, wuk_ref[0]) + _tile_rope_key(kr_ref[0, 0])
    k_ref[0, 0] = (_rms_heads128(kpre, QK_A) * gka_ref[0]).astype(BF16)
    v_ref[0, 0] = _mm(ckv, wuv_ref[0]).astype(BF16)


def _cache_call(cache_ckv, cache_kr_pad, P):
    bl = lambda b, l: (b, l, 0, 0)
    lay = lambda b, l: (l, 0, 0)
    return pl.pallas_call(
        _cache_kernel,
        grid=(DEC_BATCH, DEPTH),
        in_specs=[pl.BlockSpec((1, 1, PAST_LEN, KV_RANK), bl),
                  pl.BlockSpec((1, 1, PAST_LEN, LANES), bl),
                  pl.BlockSpec((1, KV_RANK, WA), lay),
                  pl.BlockSpec((1, KV_RANK, WVA), lay),
                  pl.BlockSpec((1, 1, WA), lay)],
        out_specs=[pl.BlockSpec((1, 1, PAST_LEN, WA), bl),
                   pl.BlockSpec((1, 1, PAST_LEN, WVA), bl)],
        out_shape=[jax.ShapeDtypeStruct((DEC_BATCH, DEPTH, PAST_LEN, WA), BF16),
                   jax.ShapeDtypeStruct((DEC_BATCH, DEPTH, PAST_LEN, WVA), BF16)],
        compiler_params=pltpu.CompilerParams(dimension_semantics=("arbitrary", "arbitrary")),
        name="cache_expand",
    )(cache_ckv, cache_kr_pad, P["wuk"], P["wuv"], P["gka"])


def _pre_kernel(latent, *refs):
    (x_ref, mod_ref, g1_ref, wint_ref, gq_ref, gkv_ref, wuq_ref, wuk_ref, wuv_ref,
     gqa_ref, gka_ref, gdq_ref, gdk_ref, bd32_ref) = refs[:14]
    refs = refs[14:]
    if latent:
        cosa_ref, sina_ref, cosd_ref, sind_ref = refs[:4]
        refs = refs[4:]
    qa_ref, ka_ref, va_ref, qd_ref, kd_ref, vd_ref, z_ref, xbc_ref, dt_ref = refs[:9]
    refs = refs[9:]
    if not latent:
        ckvn_ref, krope_ref, kd32_ref, vd32_ref = refs

    x = x_ref[...]
    mod = mod_ref[0, 0]
    shift1 = mod[:, 0:D_MODEL]
    scale1 = mod[:, D_MODEL:2 * D_MODEL]
    h = (_rms(x) * g1_ref[0] * (1.0 + scale1) + shift1).astype(BF16)

    def proj(r0, r1):
        return _mm_nt(h, wint_ref[0, r0:r1, :])

    lane128 = lax.broadcasted_iota(jnp.int32, (1, LANES), 1)

    cqn = (_rms(proj(R_CQ, R_CKV)) * gq_ref[0]).astype(BF16)
    q = _rms_heads128(_mm(cqn, wuq_ref[0]), QK_A) * gqa_ref[0]
    if latent:
        q = _rope(q, cosa_ref[...], sina_ref[...])
    qa_ref[...] = q.astype(BF16)

    ckvn = _rms(proj(R_CKV, R_KR)) * gkv_ref[0]
    kr = jnp.where(lane128 < ROPE_A, proj(R_KR, R_KR + LANES), 0.0)
    ckvb = ckvn.astype(BF16)
    k = _rms_heads128(_mm(ckvb, wuk_ref[0]) + _tile_rope_key(kr), QK_A) * gka_ref[0]
    if latent:
        k = _rope(k, cosa_ref[...], sina_ref[...])
    ka_ref[...] = k.astype(BF16)
    va_ref[...] = _mm(ckvb, wuv_ref[0]).astype(BF16)

    bd32 = bd32_ref[...]
    qd = _rms_segments(proj(R_DQ, R_DK), bd32, DH_B) * gdq_ref[0]
    kd = _rms_segments(proj(R_DK, R_DV), bd32, DH_B) * gdk_ref[0]
    vd = proj(R_DV, R_Z)
    if latent:
        qd = _rope(qd, cosd_ref[...], sind_ref[...])
        kd = _rope(kd, cosd_ref[...], sind_ref[...])
    qd_ref[...] = qd.astype(BF16)
    kd_ref[...] = kd.astype(BF16)
    vd_ref[...] = vd.astype(BF16)

    z_ref[...] = proj(R_Z, R_XBC)
    xbc_ref[...] = proj(R_XBC, R_DT)
    dt_tail = pltpu.roll(proj(D_IN_PROJ - LANES, D_IN_PROJ), 2 * H_C, 1)
    dt_ref[...] = jnp.where(lane128 < 2 * H_C, dt_tail, 0.0)

    if not latent:
        ckvn_ref[...] = ckvn
        krope_ref[...] = kr[:, 0:ROPE_A]
        kd32_ref[...] = kd.T
        vd32_ref[...] = vd.T


def _pre_call(x2d, mod4, P, l, latent):
    T = x2d.shape[0]
    tm = TM_PRE
    tpb = DEC_SEQ // tm
    row = lambda i: (i, 0)
    lay = lambda i: (l, 0, 0)
    if latent:
        mod_map = lambda i: (l, i // tpb, 0, 0)
    else:
        mod_map = lambda i: (l, DEC_BATCH, 0, 0)
    full = lambda i: (0, 0)
    in_specs = [pl.BlockSpec((tm, D_MODEL), row),
                pl.BlockSpec((1, 1, 1, N_MOD * D_MODEL), mod_map),
                pl.BlockSpec((1, 1, D_MODEL), lay),
                pl.BlockSpec((1, D_IN_PROJ, D_MODEL), lay),
                pl.BlockSpec((1, 1, Q_RANK), lay),
                pl.BlockSpec((1, 1, KV_RANK), lay),
                pl.BlockSpec((1, Q_RANK, WA), lay),
                pl.BlockSpec((1, KV_RANK, WA), lay),
                pl.BlockSpec((1, KV_RANK, WVA), lay),
                pl.BlockSpec((1, 1, WA), lay),
                pl.BlockSpec((1, 1, WA), lay),
                pl.BlockSpec((1, 1, WD), lay),
                pl.BlockSpec((1, 1, WD), lay),
                pl.BlockSpec((WD, WD), full)]
    args = [x2d, mod4, P["g1"], P["win_t"], P["gq"], P["gkv"], P["wuq"], P["wuk"], P["wuv"],
            P["gqa"], P["gka"], P["gdq"], P["gdk"], P["bd32"]]
    if latent:
        pos = lambda i: (i % tpb, 0)
        in_specs += [pl.BlockSpec((tm, WA), pos), pl.BlockSpec((tm, WA), pos),
                     pl.BlockSpec((tm, WD), pos), pl.BlockSpec((tm, WD), pos)]
        args += [P["cosa"], P["sina"], P["cosd"], P["sind"]]
    widths = [(WA, BF16), (WA, BF16), (WVA, BF16), (WD, BF16), (WD, BF16), (WD, BF16),
              (D_INNER, F32), (CONV_DIM, F32), (LANES, F32)]
    if not latent:
        widths += [(KV_RANK, F32), (ROPE_A, F32), (WD, F32), (WD, F32)]
    out_specs = [pl.BlockSpec((tm, w), row) for w, _ in widths]
    out_shape = [jax.ShapeDtypeStruct((T, w), dt) for w, dt in widths]
    if not latent:
        for i_t in (11, 12):
            out_specs[i_t] = pl.BlockSpec((WD, tm), lambda i: (0, i))
            out_shape[i_t] = jax.ShapeDtypeStruct((WD, T), F32)
    return pl.pallas_call(
        functools.partial(_pre_kernel, latent),
        grid=(T // tm,),
        in_specs=in_specs, out_specs=out_specs, out_shape=out_shape,
        compiler_params=pltpu.CompilerParams(dimension_semantics=("arbitrary",),
                                             vmem_limit_bytes=V7X_VMEM_LIMIT),
        name="pre_lat" if latent else "pre_ctx",
    )(*args)


def _softmax(q, key_refs, c0, c1):
    s = [_mm_nt(q, kr[:, c0:c1]) for kr in key_refs]
    m = s[0].max(axis=-1, keepdims=True)
    for si in s[1:]:
        m = jnp.maximum(m, si.max(axis=-1, keepdims=True))
    p = [jnp.exp(si - m) for si in s]
    den = p[0].sum(axis=-1, keepdims=True)
    for pi in p[1:]:
        den = den + pi.sum(axis=-1, keepdims=True)
    return p, den


def _pv(p, val_refs):
    pv = _mm(p[0].astype(BF16), val_refs[0][...])
    for pi, vr in zip(p[1:], val_refs[1:]):
        pv = pv + _mm(pi.astype(BF16), vr[...])
    return pv


def _attn_kernel(has_cache, lam_init, *refs):
    qa_ref, ka_ref, va_ref, qd_ref, kd_ref, vd_ref = refs[:6]
    refs = refs[6:]
    if has_cache:
        kca_ref, vca_ref, kcd_ref, vcd_ref = refs[:4]
        refs = refs[4:]
    lq1_ref, lk1_ref, lq2_ref, lk2_ref, gsub_ref, bd64_ref, o_ref = refs

    ka_refs = [ka_ref.at[0]]
    va_refs = [va_ref.at[0]]
    kd_refs = [kd_ref.at[0]]
    vd_refs = [vd_ref.at[0]]
    if has_cache:
        ka_refs = [kca_ref.at[0, 0]] + ka_refs
        va_refs = [vca_ref.at[0, 0]] + va_refs
        kd_refs = [kcd_ref.at[0, 0]] + kd_refs
        vd_refs = [vcd_ref.at[0, 0]] + vd_refs

    lane256 = lax.broadcasted_iota(jnp.int32, (1, WD), 1)
    lane128 = lax.broadcasted_iota(jnp.int32, (1, LANES), 1)

    qa = qa_ref[0]
    oa = None
    for hh in range(H_A):
        p, den = _softmax(qa[:, hh * LANES:(hh + 1) * LANES], ka_refs, hh * LANES, (hh + 1) * LANES)
        contrib = jnp.where(lane256 // V_A == hh, _pv(p, va_refs) * (1.0 / den), 0.0)
        oa = contrib if oa is None else oa + contrib

    lam = (jnp.exp(jnp.sum(lq1_ref[0] * lk1_ref[0], axis=-1, keepdims=True))
           - jnp.exp(jnp.sum(lq2_ref[0] * lk2_ref[0], axis=-1, keepdims=True)) + lam_init)
    qd = qd_ref[0]
    od = None
    for hh in range(H_B):
        blk = hh // 2
        qblk = qd[:, blk * LANES:(blk + 1) * LANES]
        maps = []
        for mm_i in range(2):
            seg = (hh % 2) * 2 + mm_i
            qm = jnp.where(lane128 // DH_B == seg, qblk, jnp.zeros_like(qblk))
            p, den = _softmax(qm, kd_refs, blk * LANES, (blk + 1) * LANES)
            maps.append(_pv(p, vd_refs) * (1.0 / den))
        contrib = jnp.where(lane256 // (2 * DH_B) == hh, maps[0] - lam * maps[1], 0.0)
        od = contrib if od is None else od + contrib
    od = _rms_segments(od, bd64_ref[...], 2 * DH_B) * (gsub_ref[0] * (1.0 - lam_init))
    o_ref[0] = jnp.concatenate([oa, od], axis=-1).astype(BF16)


def _attn_call(qa, ka, va, qd, kd, vd, caches, P, l):
    B, L = qa.shape[0], qa.shape[1]
    has_cache = caches is not None
    lam_init = 0.8 - 0.6 * math.exp(-0.3 * l)
    qmap = lambda b, i: (b, i, 0)
    kmap = lambda b, i: (b, 0, 0)
    lay = lambda b, i: (l, 0, 0)
    tq = min(TQ, L)
    in_specs = [pl.BlockSpec((1, tq, WA), qmap), pl.BlockSpec((1, L, WA), kmap),
                pl.BlockSpec((1, L, WVA), kmap), pl.BlockSpec((1, tq, WD), qmap),
                pl.BlockSpec((1, L, WD), kmap), pl.BlockSpec((1, L, WD), kmap)]
    args = [qa, ka, va, qd, kd, vd]
    if has_cache:
        cmap = lambda b, i: (b, l, 0, 0)
        in_specs += [pl.BlockSpec((1, 1, PAST_LEN, WA), cmap), pl.BlockSpec((1, 1, PAST_LEN, WVA), cmap),
                     pl.BlockSpec((1, 1, PAST_LEN, WD), cmap), pl.BlockSpec((1, 1, PAST_LEN, WD), cmap)]
        args += list(caches)
    in_specs += [pl.BlockSpec((1, 1, DH_B), lay)] * 4
    in_specs += [pl.BlockSpec((1, 1, WD), lay), pl.BlockSpec((WD, WD), lambda b, i: (0, 0))]
    args += [P["lq1"], P["lk1"], P["lq2"], P["lk2"], P["gsub"], P["bd64"]]
    return pl.pallas_call(
        functools.partial(_attn_kernel, has_cache, lam_init),
        grid=(B, L // tq),
        in_specs=in_specs,
        out_specs=pl.BlockSpec((1, tq, WVA + WD), qmap),
        out_shape=jax.ShapeDtypeStruct((B, L, WVA + WD), BF16),
        compiler_params=pltpu.CompilerParams(dimension_semantics=("arbitrary", "arbitrary"),
                                             vmem_limit_bytes=V7X_VMEM_LIMIT),
        name="attn_lat" if has_cache else "attn_ctx",
    )(*args)


def _ssd_kernel(L, has_h0, *refs):
    z_ref, xbc_ref, dt_ref = refs[:3]
    refs = refs[3:]
    if has_h0:
        s0_ref = refs[0]
        refs = refs[1:]
    cw_ref, cb_ref, alog_ref, dtb_ref, dexp_ref, g_ref, tri_ref = refs[:7]
    refs = refs[7:]
    o_ref = refs[0]
    refs = refs[1:]
    if not has_h0:
        st_ref = refs[0]
        refs = refs[1:]
    xpad, xc, yacc, cum_s, bt_s, cumt_s, dtt_s, wt_s, el_s, s_scr = refs
    nc = L // CHUNK
    halo = 8
    nd = 2 * H_C

    xpad[0:halo, :] = jnp.zeros((halo, CONV_DIM), F32)
    xpad[L + halo:L + 2 * halo, :] = jnp.zeros((halo, CONV_DIM), F32)
    xpad[halo:L + halo, :] = xbc_ref[0]
    cw = cw_ref[0]
    cb = cb_ref[0]
    dtb = dtb_ref[0]
    a_neg = -jnp.exp(alog_ref[0])
    lane128 = lax.broadcasted_iota(jnp.int32, (1, LANES), 1)
    fwd_lane = lane128 < H_C
    tri_f = tri_ref[0]
    tri_b = tri_ref[1]

    static_prep = nc <= 2

    def prep_body(c, carry):
        base = c * CHUNK if static_prep else pl.multiple_of(c * CHUNK, CHUNK)
        accs = []
        for g0 in range(0, CONV_DIM, CONV_COLS):
            a = cb[:, g0:g0 + CONV_COLS]
            if not static_prep:
                win = xpad[pl.ds(base, CHUNK + 2 * halo), g0:g0 + CONV_COLS]
            for k in range(CONV_K):
                off = halo - CONV_K // 2 + k
                if static_prep:
                    tap = xpad[base + off:base + off + CHUNK, g0:g0 + CONV_COLS]
                else:
                    tap = win[off:off + CHUNK, :]
                a = a + tap * cw[k:k + 1, g0:g0 + CONV_COLS]
            accs.append(a)
        acc = jnp.concatenate(accs, axis=-1)
        act = acc * _sigmoid(acc)
        xc[pl.ds(base, CHUNK), :] = act
        bt_s[pl.ds(base, CHUNK), :] = act[:, D_INNER:D_INNER + LANES].T
        dtr = dt_ref[0, pl.ds(base, CHUNK), :] + dtb
        dtc = jnp.maximum(dtr, 0.0) + jnp.log(1.0 + jnp.exp(-jnp.abs(dtr)))
        hi, mid, lo = _split3(dtc * a_neg)
        cum_f = _mm(tri_f, hi) + _mm(tri_f, mid) + _mm(tri_f, lo)
        cum_b = _mm(tri_b, hi) + _mm(tri_b, mid) + _mm(tri_b, lo)
        cum = jnp.where(fwd_lane, cum_f, cum_b)
        last = jnp.where(fwd_lane, cum[CHUNK - 1:CHUNK, :], cum[0:1, :])
        cum_s[pl.ds(base, CHUNK), :] = cum
        rows = pl.ds(c * nd, nd) if static_prep else pl.ds(pl.multiple_of(c * nd, nd), nd)
        cumt_s[rows, :] = cum.T[0:nd, :]
        dtt_s[rows, :] = dtc.T[0:nd, :]
        wt_s[rows, :] = (jnp.exp(last - cum) * dtc).T[0:nd, :]
        el_rows = pl.ds(c * 8, 8) if static_prep else pl.ds(pl.multiple_of(c * 8, 8), 8)
        el_s[el_rows, :] = jnp.broadcast_to(jnp.exp(last), (8, LANES))
        return carry

    if static_prep:
        for c_static in range(nc):
            prep_body(c_static, 0)
    else:
        lax.fori_loop(0, nc, prep_body, 0, unroll=2)

    row_i = lax.broadcasted_iota(jnp.int32, (CHUNK, CHUNK), 0)
    col_j = lax.broadcasted_iota(jnp.int32, (CHUNK, CHUNK), 1)
    blockmask = (lax.broadcasted_iota(jnp.int32, (2 * N_C, D_INNER), 0) // N_C
                 == lax.broadcasted_iota(jnp.int32, (2 * N_C, D_INNER), 1) // (D_INNER // G_C))
    dexp = dexp_ref[0]
    gnorm = g_ref[0]

    for d in range(2):
        if has_h0:
            s_scr[...] = s0_ref[0, 0, d]
        else:
            s_scr[...] = jnp.zeros((2 * N_C, D_INNER), F32)
        causal = (row_i >= col_j) if d == 0 else (col_j >= row_i)

        def per_head_lanes(cols, d=d):
            parts = []
            for pair in range(H_C // 2):
                i0 = d * H_C + 2 * pair
                parts.append(jnp.where(lane128 < P_C, cols[:, i0:i0 + 1], cols[:, i0 + 1:i0 + 2]))
            return jnp.concatenate(parts, axis=-1)

        def chunk_body(step, carry, d=d, causal=causal, per_head_lanes=per_head_lanes):
            c = step if d == 0 else nc - 1 - step
            base = pl.multiple_of(c * CHUNK, CHUNK)
            rows = pl.ds(pl.multiple_of(c * nd, nd), nd)
            xs = xc[pl.ds(base, CHUNK), 0:D_INNER]
            c_c = xc[pl.ds(base, CHUNK), D_INNER + LANES:D_INNER + 2 * LANES]
            cum = cum_s[pl.ds(base, CHUNK), :]
            cum_t = cumt_s[rows, :]
            dt_t = dtt_s[rows, :]
            w_t = wt_s[rows, :]
            b_t = bt_s[pl.ds(base, CHUNK), :]
            cb16 = c_c.astype(BF16)
            bt16 = b_t.astype(BF16)
            cbs = [_mm(jnp.where(lane128 // N_C == g, cb16, jnp.zeros_like(cb16)), bt16)
                   for g in range(G_C)]
            sv = s_scr[...]

            def blockdiag(v):
                return jnp.concatenate([jnp.where(lane128 < P_C, v, 0.0),
                                        jnp.where(lane128 < P_C, 0.0, v)], axis=0).astype(BF16)

            ys, upds = [], []
            for pair in range(H_C // 2):
                xbd = blockdiag(xs[:, pair * LANES:(pair + 1) * LANES])
                sbd = blockdiag(sv[:, pair * LANES:(pair + 1) * LANES])
                ms, es, ws = [], [], []
                for hh in (2 * pair, 2 * pair + 1):
                    idx = d * H_C + hh
                    col = jnp.broadcast_to(cum[:, idx:idx + 1], (CHUNK, CHUNK))
                    dec = jnp.where(causal, jnp.exp(col - cum_t[idx:idx + 1, :]), 0.0)
                    ms.append((cbs[hh // (H_C // G_C)] * dec * dt_t[idx:idx + 1, :]).astype(BF16))
                    es.append((jnp.exp(col) * c_c).astype(BF16))
                    ws.append((b_t * w_t[idx:idx + 1, :]).astype(BF16))
                ys.append(_mm(jnp.concatenate(ms + es, axis=-1), jnp.concatenate([xbd, sbd], axis=0)))
                upds.append(_mm(jnp.concatenate(ws, axis=-1), xbd))
            y = jnp.concatenate(ys, axis=-1)
            cd = per_head_lanes(el_s[pl.ds(pl.multiple_of(c * 8, 8), 1), :])
            s_scr[...] = sv * cd + jnp.where(blockmask, jnp.concatenate(upds, axis=-1), 0.0)
            if d == 0:
                yacc[pl.ds(base, CHUNK), :] = y
            else:
                zc = z_ref[0, pl.ds(base, CHUNK), :]
                yt = (yacc[pl.ds(base, CHUNK), :] + y + dexp * xs) * (zc * _sigmoid(zc))
                o_ref[0, pl.ds(base, CHUNK), :] = (_rms(yt) * gnorm).astype(BF16)
            return carry

        lax.fori_loop(0, nc, chunk_body, 0, unroll=2)
        if not has_h0:
            st = s_scr[...].T
            st_ref[0, d] = (st + pltpu.roll(st, N_C, 1))[:, 0:N_C]


def _ssd_call(z, xbc, dt, s0, P, l):
    B, L = z.shape[0], z.shape[1]
    has_h0 = s0 is not None
    bmap = lambda b: (b, 0, 0)
    lay = lambda b: (l, 0, 0)
    in_specs = [pl.BlockSpec((1, L, D_INNER), bmap), pl.BlockSpec((1, L, CONV_DIM), bmap),
                pl.BlockSpec((1, L, LANES), bmap)]
    args = [z, xbc, dt]
    if has_h0:
        in_specs.append(pl.BlockSpec((1, 1, 2, 2 * N_C, D_INNER), lambda b: (b, l, 0, 0, 0)))
        args.append(s0)
    in_specs += [pl.BlockSpec((1, 8, CONV_DIM), lay), pl.BlockSpec((1, 1, CONV_DIM), lay),
                 pl.BlockSpec((1, 1, LANES), lay), pl.BlockSpec((1, 1, LANES), lay),
                 pl.BlockSpec((1, 1, D_INNER), lay), pl.BlockSpec((1, 1, D_INNER), lay),
                 pl.BlockSpec((2, CHUNK, CHUNK), lambda b: (0, 0, 0))]
    args += [P["cw"], P["cb"], P["alog"], P["dtb"], P["dexp"], P["gssm"], P["tri"]]
    out_specs = [pl.BlockSpec((1, L, D_INNER), bmap)]
    out_shape = [jax.ShapeDtypeStruct((B, L, D_INNER), BF16)]
    if not has_h0:
        out_specs.append(pl.BlockSpec((1, 2, D_INNER, N_C), lambda b: (b, 0, 0, 0)))
        out_shape.append(jax.ShapeDtypeStruct((B, 2, D_INNER, N_C), F32))
    return pl.pallas_call(
        functools.partial(_ssd_kernel, L, has_h0),
        grid=(B,),
        in_specs=in_specs, out_specs=out_specs, out_shape=out_shape,
        scratch_shapes=[pltpu.VMEM((L + 16, CONV_DIM), F32), pltpu.VMEM((L, CONV_DIM), F32),
                        pltpu.VMEM((L, D_INNER), F32), pltpu.VMEM((L, LANES), F32),
                        pltpu.VMEM((L, LANES), F32)]
        + [pltpu.VMEM((L // CHUNK * 2 * H_C, LANES), F32)] * 3
        + [pltpu.VMEM((L // CHUNK * 8, LANES), F32), pltpu.VMEM((2 * N_C, D_INNER), F32)],
        compiler_params=pltpu.CompilerParams(dimension_semantics=("arbitrary",),
                                             vmem_limit_bytes=V7X_VMEM_LIMIT),
        name="ssd_lat" if has_h0 else "ssd_ctx",
    )(*args)


def _post_kernel(x_ref, oat_ref, oc_ref, mod_ref, wo_ref, g2_ref, w1_ref, w2_ref, o_ref):
    mod = mod_ref[0, 0]
    gate1 = mod[:, 2 * D_MODEL:3 * D_MODEL]
    shift2 = mod[:, 3 * D_MODEL:4 * D_MODEL]
    scale2 = mod[:, 4 * D_MODEL:5 * D_MODEL]
    gate2 = mod[:, 5 * D_MODEL:6 * D_MODEL]
    n_att = WVA + WD
    mix = _mm(oat_ref[...], wo_ref[0, 0:n_att, :]) + _mm(oc_ref[...], wo_ref[0, n_att:n_att + D_INNER, :])
    x1 = x_ref[...] + gate1 * mix
    h2 = (_rms(x1) * g2_ref[0] * (1.0 + scale2) + shift2).astype(BF16)
    ff = None
    for c in range(D_FF // FF_CHUNK):
        u = jnp.maximum(_mm(h2, w1_ref[0, :, c * FF_CHUNK:(c + 1) * FF_CHUNK]), 0.0)
        part = _mm((u * u).astype(BF16), w2_ref[0, c * FF_CHUNK:(c + 1) * FF_CHUNK, :])
        ff = part if ff is None else ff + part
    o_ref[...] = x1 + gate2 * ff


def _post_call(x2d, oat, oc, mod4, P, l, latent):
    T = x2d.shape[0]
    tm = TM_TOK
    tpb = DEC_SEQ // tm
    row = lambda i: (i, 0)
    lay = lambda i: (l, 0, 0)
    if latent:
        mod_map = lambda i: (l, i // tpb, 0, 0)
    else:
        mod_map = lambda i: (l, DEC_BATCH, 0, 0)
    single = pl.Buffered(1)
    return pl.pallas_call(
        _post_kernel,
        grid=(T // tm,),
        in_specs=[pl.BlockSpec((tm, D_MODEL), row),
                  pl.BlockSpec((tm, WVA + WD), row),
                  pl.BlockSpec((tm, D_INNER), row),
                  pl.BlockSpec((1, 1, 1, N_MOD * D_MODEL), mod_map),
                  pl.BlockSpec((1, D_MODEL, D_MODEL), lay, pipeline_mode=single),
                  pl.BlockSpec((1, 1, D_MODEL), lay),
                  pl.BlockSpec((1, D_MODEL, D_FF), lay, pipeline_mode=single),
                  pl.BlockSpec((1, D_FF, D_MODEL), lay, pipeline_mode=single)],
        out_specs=pl.BlockSpec((tm, D_MODEL), row),
        out_shape=jax.ShapeDtypeStruct((T, D_MODEL), F32),
        compiler_params=pltpu.CompilerParams(dimension_semantics=("arbitrary",),
                                             vmem_limit_bytes=V7X_VMEM_LIMIT),
        name="post_lat" if latent else "post_ctx",
    )(x2d, oat, oc, mod4, P["wout"], P["g2"], P["wff1"], P["wff2"])


def _rope_tables():
    t = np.arange(DEC_SEQ)
    half = ROPE_A // 2
    freqs = ROPE_BASE ** (-np.arange(0, half, 2, dtype=np.float64) / half)
    ang_r = (t // GRID_W)[:, None] * freqs
    ang_c = (t % GRID_W)[:, None] * freqs
    cos32 = np.concatenate([np.cos(ang_r), np.cos(ang_r), np.cos(ang_c), np.cos(ang_c)], axis=-1)
    sin32 = np.concatenate([-np.sin(ang_r), np.sin(ang_r), -np.sin(ang_c), np.sin(ang_c)], axis=-1)
    ones = np.ones((DEC_SEQ, NOPE_A))
    zeros = np.zeros((DEC_SEQ, NOPE_A))
    pad1 = np.ones((DEC_SEQ, LANES - QK_A))
    pad0 = np.zeros((DEC_SEQ, LANES - QK_A))
    cosa = np.tile(np.concatenate([ones, cos32, pad1], axis=-1), (1, H_A))
    sina = np.tile(np.concatenate([zeros, sin32, pad0], axis=-1), (1, H_A))
    cosd = np.tile(cos32, (1, WD // ROPE_A))
    sind = np.tile(sin32, (1, WD // ROPE_A))
    return tuple(jnp.asarray(a, F32) for a in (cosa, sina, cosd, sind))


def _constants():
    lane = np.arange(WD)
    bd32 = (lane[:, None] // DH_B == lane[None, :] // DH_B).astype(np.float32)
    bd64 = (lane[:, None] // (2 * DH_B) == lane[None, :] // (2 * DH_B)).astype(np.float32)
    i = np.arange(CHUNK)
    tri = np.stack([(i[None, :] <= i[:, None]), (i[None, :] >= i[:, None])]).astype(np.float32)
    return dict(bd32=jnp.asarray(bd32, BF16), bd64=jnp.asarray(bd64, BF16),
                tri=jnp.asarray(tri, BF16))


def _pad_last(a, n):
    return jnp.pad(a, [(0, 0)] * (a.ndim - 1) + [(0, n - a.shape[-1])])


def _prep_params(norm1_g, norm2_g, w_in, w_out, mla_q_norm_g, mla_kv_norm_g, w_uq, w_ukv, mla_qk_norm_q,
                 mla_qk_norm_k, diff_q_norm_g, diff_k_norm_g, diff_lq1, diff_lk1, diff_lq2, diff_lk2,
                 diff_subln_g, ssm_conv_w, ssm_conv_b, ssm_A_log, ssm_dt_bias, ssm_D, ssm_norm_g, w_ff1, w_ff2):
    P = _constants()
    P["win_t"] = jnp.swapaxes(w_in, 1, 2).astype(BF16)
    P["wuq"] = _pad_last(w_uq.reshape(DEPTH, Q_RANK, H_A, QK_A), LANES).reshape(DEPTH, Q_RANK, WA).astype(BF16)
    wkv = w_ukv.reshape(DEPTH, KV_RANK, H_A, NOPE_A + V_A)
    P["wuk"] = _pad_last(wkv[..., :NOPE_A], LANES).reshape(DEPTH, KV_RANK, WA).astype(BF16)
    P["wuv"] = wkv[..., NOPE_A:].reshape(DEPTH, KV_RANK, WVA).astype(BF16)
    P["wout"] = w_out.astype(BF16)
    P["wff1"] = w_ff1.astype(BF16)
    P["wff2"] = w_ff2.astype(BF16)
    vec = lambda a: a.reshape(DEPTH, 1, -1)
    P["g1"] = vec(norm1_g)
    P["g2"] = vec(norm2_g)
    P["gq"] = vec(mla_q_norm_g)
    P["gkv"] = vec(mla_kv_norm_g)
    P["gqa"] = vec(jnp.tile(_pad_last(mla_qk_norm_q, LANES), (1, H_A))) * (QK_A ** -0.5)
    P["gka"] = vec(jnp.tile(_pad_last(mla_qk_norm_k, LANES), (1, H_A)))
    P["gdq"] = vec(jnp.tile(diff_q_norm_g, (1, WD // DH_B))) * (DH_B ** -0.5)
    P["gdk"] = vec(jnp.tile(diff_k_norm_g, (1, WD // DH_B)))
    P["gsub"] = vec(jnp.tile(diff_subln_g, (1, H_B)))
    P["lq1"], P["lk1"], P["lq2"], P["lk2"] = vec(diff_lq1), vec(diff_lk1), vec(diff_lq2), vec(diff_lk2)
    P["cw"] = jnp.pad(ssm_conv_w, ((0, 0), (0, 8 - CONV_K), (0, 0)))
    P["cb"] = vec(ssm_conv_b)
    P["alog"] = vec(_pad_last(ssm_A_log.reshape(DEPTH, 2 * H_C), LANES))
    P["dtb"] = vec(_pad_last(ssm_dt_bias.reshape(DEPTH, 2 * H_C), LANES))
    P["dexp"] = vec(jnp.repeat(ssm_D, P_C, axis=-1))
    P["gssm"] = vec(ssm_norm_g)
    P["cosa"], P["sina"], P["cosd"], P["sind"] = _rope_tables()
    return P


def kernel(x_prompt, x_sample, cache_mla_ckv, cache_mla_krope, cache_diff_k, cache_diff_v, state_ssm, c, c_ctx, norm1_g, norm2_g, w_ada, b_ada, w_in, w_out, mla_q_norm_g, mla_kv_norm_g, w_uq, w_ukv, mla_qk_norm_q, mla_qk_norm_k, diff_q_norm_g, diff_k_norm_g, diff_lq1, diff_lk1, diff_lq2, diff_lk2, diff_subln_g, ssm_conv_w, ssm_conv_b, ssm_A_log, ssm_dt_bias, ssm_D, ssm_norm_g, w_ff1, w_ff2):
    P = _prep_params(norm1_g, norm2_g, w_in, w_out, mla_q_norm_g, mla_kv_norm_g, w_uq, w_ukv, mla_qk_norm_q,
                     mla_qk_norm_k, diff_q_norm_g, diff_k_norm_g, diff_lq1, diff_lk1, diff_lq2, diff_lk2,
                     diff_subln_g, ssm_conv_w, ssm_conv_b, ssm_A_log, ssm_dt_bias, ssm_D, ssm_norm_g,
                     w_ff1, w_ff2)

    cvec = jnp.concatenate([c, c_ctx[None, :], jnp.zeros((8 - DEC_BATCH - 1, D_MODEL), F32)], axis=0)
    mod4 = _mod_call(cvec, w_ada, b_ada).reshape(DEPTH, 8, 1, N_MOD * D_MODEL)

    kca, vca = _cache_call(cache_mla_ckv, _pad_last(cache_mla_krope, LANES), P)
    kcd = cache_diff_k.reshape(DEC_BATCH, DEPTH, PAST_LEN, WD).astype(BF16)
    vcd = cache_diff_v.reshape(DEC_BATCH, DEPTH, PAST_LEN, WD).astype(BF16)
    st = jnp.transpose(state_ssm, (0, 1, 2, 5, 3, 4)).reshape(DEC_BATCH, DEPTH, 2, N_C, D_INNER)
    half = D_INNER // G_C
    zero = jnp.zeros_like(st[..., :half])
    s0 = jnp.concatenate([jnp.concatenate([st[..., :half], zero], axis=-1),
                          jnp.concatenate([zero, st[..., half:]], axis=-1)], axis=-2)

    xp = x_prompt.reshape(BATCH * SEQ, D_MODEL)
    xs = x_sample.reshape(DEC_BATCH * DEC_SEQ, D_MODEL)
    ckv_l, krope_l, kd_l, vd_l, st_l = [], [], [], [], []
    for l in range(DEPTH):
        qa, ka, va, qd, kd, vd, z, xbc, dt, ckvn, krope, kd32, vd32 = _pre_call(xp, mod4, P, l, False)
        r3 = lambda a: a.reshape(BATCH, SEQ, a.shape[-1])
        oat = _attn_call(r3(qa), r3(ka), r3(va), r3(qd), r3(kd), r3(vd), None, P, l)
        oc, st_new = _ssd_call(r3(z), r3(xbc), r3(dt), None, P, l)
        xp = _post_call(xp, oat.reshape(BATCH * SEQ, -1), oc.reshape(BATCH * SEQ, -1), mod4, P, l, False)
        ckv_l.append(ckvn.reshape(BATCH, SEQ, KV_RANK))
        krope_l.append(krope.reshape(BATCH, SEQ, ROPE_A))
        kd_l.append(kd32.T.reshape(BATCH, SEQ, H_B, 2 * DH_B))
        vd_l.append(vd32.T.reshape(BATCH, SEQ, H_B, 2 * DH_B))
        st_l.append(st_new.reshape(BATCH, 2, H_C, P_C, N_C))
        qa, ka, va, qd, kd, vd, z, xbc, dt = _pre_call(xs, mod4, P, l, True)
        r3 = lambda a: a.reshape(DEC_BATCH, DEC_SEQ, a.shape[-1])
        oat = _attn_call(r3(qa), r3(ka), r3(va), r3(qd), r3(kd), r3(vd), (kca, vca, kcd, vcd), P, l)
        oc, = _ssd_call(r3(z), r3(xbc), r3(dt), s0, P, l)
        xs = _post_call(xs, oat.reshape(DEC_BATCH * DEC_SEQ, -1), oc.reshape(DEC_BATCH * DEC_SEQ, -1),
                        mod4, P, l, True)
    return (xp.reshape(BATCH, SEQ, D_MODEL), xs.reshape(DEC_BATCH, DEC_SEQ, D_MODEL),
            jnp.stack(ckv_l, axis=1), jnp.stack(krope_l, axis=1), jnp.stack(kd_l, axis=1),
            jnp.stack(vd_l, axis=1), jnp.stack(st_l, axis=1))
```

```python
import functools
import math

import numpy as np
import jax
import jax.numpy as jnp
from jax import lax
from jax.experimental import pallas as pl
from jax.experimental.pallas import tpu as pltpu

D_MODEL = 1024
BATCH = 16
SEQ = 256
DEPTH = 4
DEC_BATCH = 4
DEC_SEQ = 2048
PAST_LEN = 256
GRID_W = 64
ROPE_BASE = 10000.0
EPS = 1e-6
CHUNK = 128
D_FF = 4 * D_MODEL
N_MOD = 6
H_A = 4
NOPE_A = 64
ROPE_A = 32
V_A = 64
Q_RANK = 256
KV_RANK = 128
H_B = 4
DH_B = 32
H_C = 8
P_C = 64
N_C = 64
G_C = 2
CONV_K = 5
D_INNER = H_C * P_C
CONV_DIM = D_INNER + 2 * G_C * N_C

F32 = jnp.float32
BF16 = jnp.bfloat16

LANES = 128
QK_A = NOPE_A + ROPE_A
WA = H_A * LANES
WD = H_B * 2 * DH_B
WVA = H_A * V_A
R_CQ, R_CKV, R_KR, R_DQ, R_DK, R_DV, R_Z, R_XBC, R_DT, D_IN_PROJ = (
    0, 256, 384, 416, 672, 928, 1184, 1696, 2464, 2480)
V7X_VMEM_LIMIT = 56 * 1024 * 1024
TM_TOK = 512
TM_PRE = 1024
TQ = 512
FF_CHUNK = 1024
CONV_COLS = 256


def _mm(a, b):
    return jnp.dot(a, b, preferred_element_type=F32)


def _mm_nt(a, b):
    return lax.dot_general(a, b, (((1,), (1,)), ((), ())), preferred_element_type=F32)


def _split3(x):
    hi = x.astype(BF16)
    r = x - hi.astype(F32)
    mid = r.astype(BF16)
    lo = (r - mid.astype(F32)).astype(BF16)
    return hi, mid, lo


def _sigmoid(x):
    return 1.0 / (1.0 + jnp.exp(-x))


def _rms(x, n=None):
    n = x.shape[-1] if n is None else n
    return x * lax.rsqrt(jnp.sum(x * x, axis=-1, keepdims=True) * (1.0 / n) + EPS)


def _rms_heads128(x, n_real):
    parts = [_rms(x[:, i * LANES:(i + 1) * LANES], n_real) for i in range(x.shape[1] // LANES)]
    return jnp.concatenate(parts, axis=-1)


def _rms_segments(x, bd, seg):
    sq = x * x
    hi = sq.astype(BF16)
    lo = (sq - hi.astype(F32)).astype(BF16)
    ssq = _mm(hi, bd) + _mm(lo, bd)
    return x * lax.rsqrt(ssq * (1.0 / seg) + EPS)


def _rope(x, cos, sin_signed):
    lane = lax.broadcasted_iota(jnp.int32, (1, LANES), 1)
    lo = (lane % 16) < 8
    parts = []
    for i in range(x.shape[1] // LANES):
        xb = x[:, i * LANES:(i + 1) * LANES]
        parts.append(jnp.where(lo, pltpu.roll(xb, LANES - 8, 1), pltpu.roll(xb, 8, 1)))
    return x * cos + jnp.concatenate(parts, axis=-1) * sin_signed


def _mod_kernel(c_ref, w_ref, b_ref, o_ref):
    cv = c_ref[...]
    s = cv * _sigmoid(cv)
    o_ref[0] = _mm(s.astype(BF16), w_ref[0].astype(BF16)) + b_ref[0]


def _mod_call(cvec, w_ada, b_ada):
    tn = 1536
    n_out = N_MOD * D_MODEL
    return pl.pallas_call(
        _mod_kernel,
        grid=(DEPTH, n_out // tn),
        in_specs=[pl.BlockSpec((8, D_MODEL), lambda l, j: (0, 0)),
                  pl.BlockSpec((1, D_MODEL, tn), lambda l, j: (l, 0, j)),
                  pl.BlockSpec((1, 1, tn), lambda l, j: (l, 0, j))],
        out_specs=pl.BlockSpec((1, 8, tn), lambda l, j: (l, 0, j)),
        out_shape=jax.ShapeDtypeStruct((DEPTH, 8, n_out), F32),
        compiler_params=pltpu.CompilerParams(dimension_semantics=("arbitrary", "arbitrary")),
        name="mod",
    )(cvec, w_ada, b_ada.reshape(DEPTH, 1, n_out))


def _tile_rope_key(kr):
    return jnp.concatenate([pltpu.roll(kr, NOPE_A, 1)] * H_A, axis=-1)


def _cache_kernel(ckv_ref, kr_ref, wuk_ref, wuv_ref, gka_ref, k_ref, v_ref):
    ckv = ckv_ref[0, 0].astype(BF16)
    kpre = _mm(ckv, wuk_ref[0]) + _tile_rope_key(kr_ref[0, 0])
    k_ref[0, 0] = (_rms_heads128(kpre, QK_A) * gka_ref[0]).astype(BF16)
    v_ref[0, 0] = _mm(ckv, wuv_ref[0]).astype(BF16)


def _cache_call(cache_ckv, cache_kr_pad, P):
    bl = lambda b, l: (b, l, 0, 0)
    lay = lambda b, l: (l, 0, 0)
    return pl.pallas_call(
        _cache_kernel,
        grid=(DEC_BATCH, DEPTH),
        in_specs=[pl.BlockSpec((1, 1, PAST_LEN, KV_RANK), bl),
                  pl.BlockSpec((1, 1, PAST_LEN, LANES), bl),
                  pl.BlockSpec((1, KV_RANK, WA), lay),
                  pl.BlockSpec((1, KV_RANK, WVA), lay),
                  pl.BlockSpec((1, 1, WA), lay)],
        out_specs=[pl.BlockSpec((1, 1, PAST_LEN, WA), bl),
                   pl.BlockSpec((1, 1, PAST_LEN, WVA), bl)],
        out_shape=[jax.ShapeDtypeStruct((DEC_BATCH, DEPTH, PAST_LEN, WA), BF16),
                   jax.ShapeDtypeStruct((DEC_BATCH, DEPTH, PAST_LEN, WVA), BF16)],
        compiler_params=pltpu.CompilerParams(dimension_semantics=("arbitrary", "arbitrary")),
        name="cache_expand",
    )(cache_ckv, cache_kr_pad, P["wuk"], P["wuv"], P["gka"])


def _pre_kernel(latent, *refs):
    (x_ref, mod_ref, g1_ref, wint_ref, gq_ref, gkv_ref, wuq_ref, wuk_ref, wuv_ref,
     gqa_ref, gka_ref, gdq_ref, gdk_ref, bd32_ref) = refs[:14]
    refs = refs[14:]
    if latent:
        cosa_ref, sina_ref, cosd_ref, sind_ref = refs[:4]
        refs = refs[4:]
    qa_ref, ka_ref, va_ref, qd_ref, kd_ref, vd_ref, z_ref, xbc_ref, dt_ref = refs[:9]
    refs = refs[9:]
    if not latent:
        ckvn_ref, krope_ref, kd32_ref, vd32_ref = refs

    x = x_ref[...]
    mod = mod_ref[0, 0]
    shift1 = mod[:, 0:D_MODEL]
    scale1 = mod[:, D_MODEL:2 * D_MODEL]
    h = (_rms(x) * g1_ref[0] * (1.0 + scale1) + shift1).astype(BF16)

    def proj(r0, r1):
        return _mm_nt(h, wint_ref[0, r0:r1, :])

    lane128 = lax.broadcasted_iota(jnp.int32, (1, LANES), 1)

    cqn = (_rms(proj(R_CQ, R_CKV)) * gq_ref[0]).astype(BF16)
    q = _rms_heads128(_mm(cqn, wuq_ref[0]), QK_A) * gqa_ref[0]
    if latent:
        q = _rope(q, cosa_ref[...], sina_ref[...])
    qa_ref[...] = q.astype(BF16)

    ckvn = _rms(proj(R_CKV, R_KR)) * gkv_ref[0]
    kr = jnp.where(lane128 < ROPE_A, proj(R_KR, R_KR + LANES), 0.0)
    ckvb = ckvn.astype(BF16)
    k = _rms_heads128(_mm(ckvb, wuk_ref[0]) + _tile_rope_key(kr), QK_A) * gka_ref[0]
    if latent:
        k = _rope(k, cosa_ref[...], sina_ref[...])
    ka_ref[...] = k.astype(BF16)
    va_ref[...] = _mm(ckvb, wuv_ref[0]).astype(BF16)

    bd32 = bd32_ref[...]
    qd = _rms_segments(proj(R_DQ, R_DK), bd32, DH_B) * gdq_ref[0]
    kd = _rms_segments(proj(R_DK, R_DV), bd32, DH_B) * gdk_ref[0]
    vd = proj(R_DV, R_Z)
    if latent:
        qd = _rope(qd, cosd_ref[...], sind_ref[...])
        kd = _rope(kd, cosd_ref[...], sind_ref[...])
    qd_ref[...] = qd.astype(BF16)
    kd_ref[...] = kd.astype(BF16)
    vd_ref[...] = vd.astype(BF16)

    z_ref[...] = proj(R_Z, R_XBC)
    xbc_ref[...] = proj(R_XBC, R_DT)
    dt_tail = pltpu.roll(proj(D_IN_PROJ - LANES, D_IN_PROJ), 2 * H_C, 1)
    dt_ref[...] = jnp.where(lane128 < 2 * H_C, dt_tail, 0.0)

    if not latent:
        ckvn_ref[...] = ckvn
        krope_ref[...] = kr[:, 0:ROPE_A]
        for bb in range(x_ref.shape[0] // SEQ):
            kd32_ref[bb] = kd[bb * SEQ:(bb + 1) * SEQ, :].T
            vd32_ref[bb] = vd[bb * SEQ:(bb + 1) * SEQ, :].T


def _pre_call(x2d, mod4, P, l, latent):
    T = x2d.shape[0]
    tm = TM_PRE
    tpb = DEC_SEQ // tm
    row = lambda i: (i, 0)
    lay = lambda i: (l, 0, 0)
    if latent:
        mod_map = lambda i: (l, i // tpb, 0, 0)
    else:
        mod_map = lambda i: (l, DEC_BATCH, 0, 0)
    full = lambda i: (0, 0)
    in_specs = [pl.BlockSpec((tm, D_MODEL), row),
                pl.BlockSpec((1, 1, 1, N_MOD * D_MODEL), mod_map),
                pl.BlockSpec((1, 1, D_MODEL), lay),
                pl.BlockSpec((1, D_IN_PROJ, D_MODEL), lay),
                pl.BlockSpec((1, 1, Q_RANK), lay),
                pl.BlockSpec((1, 1, KV_RANK), lay),
                pl.BlockSpec((1, Q_RANK, WA), lay),
                pl.BlockSpec((1, KV_RANK, WA), lay),
                pl.BlockSpec((1, KV_RANK, WVA), lay),
                pl.BlockSpec((1, 1, WA), lay),
                pl.BlockSpec((1, 1, WA), lay),
                pl.BlockSpec((1, 1, WD), lay),
                pl.BlockSpec((1, 1, WD), lay),
                pl.BlockSpec((WD, WD), full)]
    args = [x2d, mod4, P["g1"], P["win_t"], P["gq"], P["gkv"], P["wuq"], P["wuk"], P["wuv"],
            P["gqa"], P["gka"], P["gdq"], P["gdk"], P["bd32"]]
    if latent:
        pos = lambda i: (i % tpb, 0)
        in_specs += [pl.BlockSpec((tm, WA), pos), pl.BlockSpec((tm, WA), pos),
                     pl.BlockSpec((tm, WD), pos), pl.BlockSpec((tm, WD), pos)]
        args += [P["cosa"], P["sina"], P["cosd"], P["sind"]]
    widths = [(WA, BF16), (WA, BF16), (WVA, BF16), (WD, BF16), (WD, BF16), (WD, BF16),
              (D_INNER, F32), (CONV_DIM, F32), (LANES, F32)]
    if not latent:
        widths += [(KV_RANK, F32), (ROPE_A, F32), (WD, F32), (WD, F32)]
    out_specs = [pl.BlockSpec((tm, w), row) for w, _ in widths]
    out_shape = [jax.ShapeDtypeStruct((T, w), dt) for w, dt in widths]
    if not latent:
        for i_t in (11, 12):
            out_specs[i_t] = pl.BlockSpec((tm // SEQ, WD, SEQ), lambda i: (i, 0, 0))
            out_shape[i_t] = jax.ShapeDtypeStruct((T // SEQ, WD, SEQ), F32)
    return pl.pallas_call(
        functools.partial(_pre_kernel, latent),
        grid=(T // tm,),
        in_specs=in_specs, out_specs=out_specs, out_shape=out_shape,
        compiler_params=pltpu.CompilerParams(dimension_semantics=("arbitrary",),
                                             vmem_limit_bytes=V7X_VMEM_LIMIT),
        name="pre_lat" if latent else "pre_ctx",
    )(*args)


def _softmax(q, key_refs, c0, c1):
    s = [_mm_nt(q, kr[:, c0:c1]) for kr in key_refs]
    m = s[0].max(axis=-1, keepdims=True)
    for si in s[1:]:
        m = jnp.maximum(m, si.max(axis=-1, keepdims=True))
    p = [jnp.exp(si - m) for si in s]
    den = p[0].sum(axis=-1, keepdims=True)
    for pi in p[1:]:
        den = den + pi.sum(axis=-1, keepdims=True)
    return p, den


def _pv(p, val_refs):
    pv = _mm(p[0].astype(BF16), val_refs[0][...])
    for pi, vr in zip(p[1:], val_refs[1:]):
        pv = pv + _mm(pi.astype(BF16), vr[...])
    return pv


def _attn_kernel(has_cache, lam_init, *refs):
    qa_ref, ka_ref, va_ref, qd_ref, kd_ref, vd_ref = refs[:6]
    refs = refs[6:]
    if has_cache:
        kca_ref, vca_ref, kcd_ref, vcd_ref = refs[:4]
        refs = refs[4:]
    lq1_ref, lk1_ref, lq2_ref, lk2_ref, gsub_ref, bd64_ref, o_ref = refs

    ka_refs = [ka_ref.at[0]]
    va_refs = [va_ref.at[0]]
    kd_refs = [kd_ref.at[0]]
    vd_refs = [vd_ref.at[0]]
    if has_cache:
        ka_refs = [kca_ref.at[0, 0]] + ka_refs
        va_refs = [vca_ref.at[0, 0]] + va_refs
        kd_refs = [kcd_ref.at[0, 0]] + kd_refs
        vd_refs = [vcd_ref.at[0, 0]] + vd_refs

    lane256 = lax.broadcasted_iota(jnp.int32, (1, WD), 1)
    lane128 = lax.broadcasted_iota(jnp.int32, (1, LANES), 1)

    qa = qa_ref[0]
    oa = None
    for hh in range(H_A):
        p, den = _softmax(qa[:, hh * LANES:(hh + 1) * LANES], ka_refs, hh * LANES, (hh + 1) * LANES)
        contrib = jnp.where(lane256 // V_A == hh, _pv(p, va_refs) * (1.0 / den), 0.0)
        oa = contrib if oa is None else oa + contrib

    lam = (jnp.exp(jnp.sum(lq1_ref[0] * lk1_ref[0], axis=-1, keepdims=True))
           - jnp.exp(jnp.sum(lq2_ref[0] * lk2_ref[0], axis=-1, keepdims=True)) + lam_init)
    qd = qd_ref[0]
    od = None
    for hh in range(H_B):
        blk = hh // 2
        qblk = qd[:, blk * LANES:(blk + 1) * LANES]
        maps = []
        for mm_i in range(2):
            seg = (hh % 2) * 2 + mm_i
            qm = jnp.where(lane128 // DH_B == seg, qblk, jnp.zeros_like(qblk))
            p, den = _softmax(qm, kd_refs, blk * LANES, (blk + 1) * LANES)
            maps.append(_pv(p, vd_refs) * (1.0 / den))
        contrib = jnp.where(lane256 // (2 * DH_B) == hh, maps[0] - lam * maps[1], 0.0)
        od = contrib if od is None else od + contrib
    od = _rms_segments(od, bd64_ref[...], 2 * DH_B) * (gsub_ref[0] * (1.0 - lam_init))
    o_ref[0] = jnp.concatenate([oa, od], axis=-1).astype(BF16)


def _attn_call(qa, ka, va, qd, kd, vd, caches, P, l):
    B, L = qa.shape[0], qa.shape[1]
    has_cache = caches is not None
    lam_init = 0.8 - 0.6 * math.exp(-0.3 * l)
    qmap = lambda b, i: (b, i, 0)
    kmap = lambda b, i: (b, 0, 0)
    lay = lambda b, i: (l, 0, 0)
    tq = min(TQ, L)
    in_specs = [pl.BlockSpec((1, tq, WA), qmap), pl.BlockSpec((1, L, WA), kmap),
                pl.BlockSpec((1, L, WVA), kmap), pl.BlockSpec((1, tq, WD), qmap),
                pl.BlockSpec((1, L, WD), kmap), pl.BlockSpec((1, L, WD), kmap)]
    args = [qa, ka, va, qd, kd, vd]
    if has_cache:
        cmap = lambda b, i: (b, l, 0, 0)
        in_specs += [pl.BlockSpec((1, 1, PAST_LEN, WA), cmap), pl.BlockSpec((1, 1, PAST_LEN, WVA), cmap),
                     pl.BlockSpec((1, 1, PAST_LEN, WD), cmap), pl.BlockSpec((1, 1, PAST_LEN, WD), cmap)]
        args += list(caches)
    in_specs += [pl.BlockSpec((1, 1, DH_B), lay)] * 4
    in_specs += [pl.BlockSpec((1, 1, WD), lay), pl.BlockSpec((WD, WD), lambda b, i: (0, 0))]
    args += [P["lq1"], P["lk1"], P["lq2"], P["lk2"], P["gsub"], P["bd64"]]
    return pl.pallas_call(
        functools.partial(_attn_kernel, has_cache, lam_init),
        grid=(B, L // tq),
        in_specs=in_specs,
        out_specs=pl.BlockSpec((1, tq, WVA + WD), qmap),
        out_shape=jax.ShapeDtypeStruct((B, L, WVA + WD), BF16),
        compiler_params=pltpu.CompilerParams(dimension_semantics=("arbitrary", "arbitrary"),
                                             vmem_limit_bytes=V7X_VMEM_LIMIT),
        name="attn_lat" if has_cache else "attn_ctx",
    )(*args)


def _ssd_kernel(L, has_h0, *refs):
    z_ref, xbc_ref, dt_ref = refs[:3]
    refs = refs[3:]
    if has_h0:
        s0_ref = refs[0]
        refs = refs[1:]
    cw_ref, cb_ref, alog_ref, dtb_ref, dexp_ref, g_ref, tri_ref = refs[:7]
    refs = refs[7:]
    o_ref = refs[0]
    refs = refs[1:]
    if not has_h0:
        st_ref = refs[0]
        refs = refs[1:]
    xpad, xc, yacc, cum_s, bt_s, cumt_s, dtt_s, wt_s, el_s, s_scr = refs
    nc = L // CHUNK
    halo = 8
    nd = 2 * H_C

    xpad[0:halo, :] = jnp.zeros((halo, CONV_DIM), F32)
    xpad[L + halo:L + 2 * halo, :] = jnp.zeros((halo, CONV_DIM), F32)
    xpad[halo:L + halo, :] = xbc_ref[0]
    cw = cw_ref[0]
    cb = cb_ref[0]
    dtb = dtb_ref[0]
    a_neg = -jnp.exp(alog_ref[0])
    lane128 = lax.broadcasted_iota(jnp.int32, (1, LANES), 1)
    fwd_lane = lane128 < H_C
    tri_f = tri_ref[0]
    tri_b = tri_ref[1]

    static_prep = nc <= 2

    def prep_body(c, carry):
        base = c * CHUNK if static_prep else pl.multiple_of(c * CHUNK, CHUNK)
        accs = []
        for g0 in range(0, CONV_DIM, CONV_COLS):
            a = cb[:, g0:g0 + CONV_COLS]
            if not static_prep:
                win = xpad[pl.ds(base, CHUNK + 2 * halo), g0:g0 + CONV_COLS]
            for k in range(CONV_K):
                off = halo - CONV_K // 2 + k
                if static_prep:
                    tap = xpad[base + off:base + off + CHUNK, g0:g0 + CONV_COLS]
                else:
                    tap = win[off:off + CHUNK, :]
                a = a + tap * cw[k:k + 1, g0:g0 + CONV_COLS]
            accs.append(a)
        acc = jnp.concatenate(accs, axis=-1)
        act = acc * _sigmoid(acc)
        xc[pl.ds(base, CHUNK), :] = act
        bt_s[pl.ds(base, CHUNK), :] = act[:, D_INNER:D_INNER + LANES].T
        dtr = dt_ref[0, pl.ds(base, CHUNK), :] + dtb
        dtc = jnp.maximum(dtr, 0.0) + jnp.log(1.0 + jnp.exp(-jnp.abs(dtr)))
        hi, mid, lo = _split3(dtc * a_neg)
        cum_f = _mm(tri_f, hi) + _mm(tri_f, mid) + _mm(tri_f, lo)
        cum_b = _mm(tri_b, hi) + _mm(tri_b, mid) + _mm(tri_b, lo)
        cum = jnp.where(fwd_lane, cum_f, cum_b)
        last = jnp.where(fwd_lane, cum[CHUNK - 1:CHUNK, :], cum[0:1, :])
        cum_s[pl.ds(base, CHUNK), :] = cum
        rows = pl.ds(c * nd, nd) if static_prep else pl.ds(pl.multiple_of(c * nd, nd), nd)
        cumt_s[rows, :] = cum.T[0:nd, :]
        dtt_s[rows, :] = dtc.T[0:nd, :]
        wt_s[rows, :] = (jnp.exp(last - cum) * dtc).T[0:nd, :]
        el_rows = pl.ds(c * 8, 8) if static_prep else pl.ds(pl.multiple_of(c * 8, 8), 8)
        el_s[el_rows, :] = jnp.broadcast_to(jnp.exp(last), (8, LANES))
        return carry

    if static_prep:
        for c_static in range(nc):
            prep_body(c_static, 0)
    else:
        lax.fori_loop(0, nc, prep_body, 0, unroll=2)

    row_i = lax.broadcasted_iota(jnp.int32, (CHUNK, CHUNK), 0)
    col_j = lax.broadcasted_iota(jnp.int32, (CHUNK, CHUNK), 1)
    blockmask = (lax.broadcasted_iota(jnp.int32, (2 * N_C, D_INNER), 0) // N_C
                 == lax.broadcasted_iota(jnp.int32, (2 * N_C, D_INNER), 1) // (D_INNER // G_C))
    dexp = dexp_ref[0]
    gnorm = g_ref[0]

    for d in range(2):
        if has_h0:
            s_scr[...] = s0_ref[0, 0, d]
        else:
            s_scr[...] = jnp.zeros((2 * N_C, D_INNER), F32)
        causal = (row_i >= col_j) if d == 0 else (col_j >= row_i)

        def per_head_lanes(cols, d=d):
            parts = []
            for pair in range(H_C // 2):
                i0 = d * H_C + 2 * pair
                parts.append(jnp.where(lane128 < P_C, cols[:, i0:i0 + 1], cols[:, i0 + 1:i0 + 2]))
            return jnp.concatenate(parts, axis=-1)

        def chunk_body(step, carry, d=d, causal=causal, per_head_lanes=per_head_lanes):
            c = step if d == 0 else nc - 1 - step
            base = pl.multiple_of(c * CHUNK, CHUNK)
            rows = pl.ds(pl.multiple_of(c * nd, nd), nd)
            xs = xc[pl.ds(base, CHUNK), 0:D_INNER]
            c_c = xc[pl.ds(base, CHUNK), D_INNER + LANES:D_INNER + 2 * LANES]
            cum = cum_s[pl.ds(base, CHUNK), :]
            cum_t = cumt_s[rows, :]
            dt_t = dtt_s[rows, :]
            w_t = wt_s[rows, :]
            b_t = bt_s[pl.ds(base, CHUNK), :]
            cb16 = c_c.astype(BF16)
            bt16 = b_t.astype(BF16)
            cbs = [_mm(jnp.where(lane128 // N_C == g, cb16, jnp.zeros_like(cb16)), bt16)
                   for g in range(G_C)]
            sv = s_scr[...]

            def blockdiag(v):
                return jnp.concatenate([jnp.where(lane128 < P_C, v, 0.0),
                                        jnp.where(lane128 < P_C, 0.0, v)], axis=0).astype(BF16)

            ys, upds = [], []
            for pair in range(H_C // 2):
                xbd = blockdiag(xs[:, pair * LANES:(pair + 1) * LANES])
                sbd = blockdiag(sv[:, pair * LANES:(pair + 1) * LANES])
                ms, es, ws = [], [], []
                for hh in (2 * pair, 2 * pair + 1):
                    idx = d * H_C + hh
                    col = jnp.broadcast_to(cum[:, idx:idx + 1], (CHUNK, CHUNK))
                    dec = jnp.where(causal, jnp.exp(col - cum_t[idx:idx + 1, :]), 0.0)
                    ms.append((cbs[hh // (H_C // G_C)] * dec * dt_t[idx:idx + 1, :]).astype(BF16))
                    es.append((jnp.exp(col) * c_c).astype(BF16))
                    ws.append((b_t * w_t[idx:idx + 1, :]).astype(BF16))
                ys.append(_mm(jnp.concatenate(ms + es, axis=-1), jnp.concatenate([xbd, sbd], axis=0)))
                upds.append(_mm(jnp.concatenate(ws, axis=-1), xbd))
            y = jnp.concatenate(ys, axis=-1)
            cd = per_head_lanes(el_s[pl.ds(pl.multiple_of(c * 8, 8), 1), :])
            s_scr[...] = sv * cd + jnp.where(blockmask, jnp.concatenate(upds, axis=-1), 0.0)
            if d == 0:
                yacc[pl.ds(base, CHUNK), :] = y
            else:
                zc = z_ref[0, pl.ds(base, CHUNK), :]
                yt = (yacc[pl.ds(base, CHUNK), :] + y + dexp * xs) * (zc * _sigmoid(zc))
                o_ref[0, pl.ds(base, CHUNK), :] = (_rms(yt) * gnorm).astype(BF16)
            return carry

        lax.fori_loop(0, nc, chunk_body, 0, unroll=2)
        if not has_h0:
            st = s_scr[...].T
            st_ref[0, d] = (st + pltpu.roll(st, N_C, 1))[:, 0:N_C]


def _ssd_call(z, xbc, dt, s0, P, l):
    B, L = z.shape[0], z.shape[1]
    has_h0 = s0 is not None
    bmap = lambda b: (b, 0, 0)
    lay = lambda b: (l, 0, 0)
    in_specs = [pl.BlockSpec((1, L, D_INNER), bmap), pl.BlockSpec((1, L, CONV_DIM), bmap),
                pl.BlockSpec((1, L, LANES), bmap)]
    args = [z, xbc, dt]
    if has_h0:
        in_specs.append(pl.BlockSpec((1, 1, 2, 2 * N_C, D_INNER), lambda b: (b, l, 0, 0, 0)))
        args.append(s0)
    in_specs += [pl.BlockSpec((1, 8, CONV_DIM), lay), pl.BlockSpec((1, 1, CONV_DIM), lay),
                 pl.BlockSpec((1, 1, LANES), lay), pl.BlockSpec((1, 1, LANES), lay),
                 pl.BlockSpec((1, 1, D_INNER), lay), pl.BlockSpec((1, 1, D_INNER), lay),
                 pl.BlockSpec((2, CHUNK, CHUNK), lambda b: (0, 0, 0))]
    args += [P["cw"], P["cb"], P["alog"], P["dtb"], P["dexp"], P["gssm"], P["tri"]]
    out_specs = [pl.BlockSpec((1, L, D_INNER), bmap)]
    out_shape = [jax.ShapeDtypeStruct((B, L, D_INNER), BF16)]
    if not has_h0:
        out_specs.append(pl.BlockSpec((1, 2, D_INNER, N_C), lambda b: (b, 0, 0, 0)))
        out_shape.append(jax.ShapeDtypeStruct((B, 2, D_INNER, N_C), F32))
    return pl.pallas_call(
        functools.partial(_ssd_kernel, L, has_h0),
        grid=(B,),
        in_specs=in_specs, out_specs=out_specs, out_shape=out_shape,
        scratch_shapes=[pltpu.VMEM((L + 16, CONV_DIM), F32), pltpu.VMEM((L, CONV_DIM), F32),
                        pltpu.VMEM((L, D_INNER), F32), pltpu.VMEM((L, LANES), F32),
                        pltpu.VMEM((L, LANES), F32)]
        + [pltpu.VMEM((L // CHUNK * 2 * H_C, LANES), F32)] * 3
        + [pltpu.VMEM((L // CHUNK * 8, LANES), F32), pltpu.VMEM((2 * N_C, D_INNER), F32)],
        compiler_params=pltpu.CompilerParams(dimension_semantics=("arbitrary",),
                                             vmem_limit_bytes=V7X_VMEM_LIMIT),
        name="ssd_lat" if has_h0 else "ssd_ctx",
    )(*args)


def _post_kernel(x_ref, oat_ref, oc_ref, mod_ref, wo_ref, g2_ref, w1_ref, w2_ref, o_ref):
    mod = mod_ref[0, 0]
    gate1 = mod[:, 2 * D_MODEL:3 * D_MODEL]
    shift2 = mod[:, 3 * D_MODEL:4 * D_MODEL]
    scale2 = mod[:, 4 * D_MODEL:5 * D_MODEL]
    gate2 = mod[:, 5 * D_MODEL:6 * D_MODEL]
    n_att = WVA + WD
    mix = _mm(oat_ref[...], wo_ref[0, 0:n_att, :]) + _mm(oc_ref[...], wo_ref[0, n_att:n_att + D_INNER, :])
    x1 = x_ref[...] + gate1 * mix
    h2 = (_rms(x1) * g2_ref[0] * (1.0 + scale2) + shift2).astype(BF16)
    ff = None
    for c in range(D_FF // FF_CHUNK):
        u = jnp.maximum(_mm(h2, w1_ref[0, :, c * FF_CHUNK:(c + 1) * FF_CHUNK]), 0.0)
        part = _mm((u * u).astype(BF16), w2_ref[0, c * FF_CHUNK:(c + 1) * FF_CHUNK, :])
        ff = part if ff is None else ff + part
    o_ref[...] = x1 + gate2 * ff


def _post_call(x2d, oat, oc, mod4, P, l, latent):
    T = x2d.shape[0]
    tm = TM_TOK
    tpb = DEC_SEQ // tm
    row = lambda i: (i, 0)
    lay = lambda i: (l, 0, 0)
    if latent:
        mod_map = lambda i: (l, i // tpb, 0, 0)
    else:
        mod_map = lambda i: (l, DEC_BATCH, 0, 0)
    single = pl.Buffered(1)
    return pl.pallas_call(
        _post_kernel,
        grid=(T // tm,),
        in_specs=[pl.BlockSpec((tm, D_MODEL), row),
                  pl.BlockSpec((tm, WVA + WD), row),
                  pl.BlockSpec((tm, D_INNER), row),
                  pl.BlockSpec((1, 1, 1, N_MOD * D_MODEL), mod_map),
                  pl.BlockSpec((1, D_MODEL, D_MODEL), lay, pipeline_mode=single),
                  pl.BlockSpec((1, 1, D_MODEL), lay),
                  pl.BlockSpec((1, D_MODEL, D_FF), lay, pipeline_mode=single),
                  pl.BlockSpec((1, D_FF, D_MODEL), lay, pipeline_mode=single)],
        out_specs=pl.BlockSpec((tm, D_MODEL), row),
        out_shape=jax.ShapeDtypeStruct((T, D_MODEL), F32),
        compiler_params=pltpu.CompilerParams(dimension_semantics=("arbitrary",),
                                             vmem_limit_bytes=V7X_VMEM_LIMIT),
        name="post_lat" if latent else "post_ctx",
    )(x2d, oat, oc, mod4, P["wout"], P["g2"], P["wff1"], P["wff2"])


def _rope_tables():
    t = np.arange(DEC_SEQ)
    half = ROPE_A // 2
    freqs = ROPE_BASE ** (-np.arange(0, half, 2, dtype=np.float64) / half)
    ang_r = (t // GRID_W)[:, None] * freqs
    ang_c = (t % GRID_W)[:, None] * freqs
    cos32 = np.concatenate([np.cos(ang_r), np.cos(ang_r), np.cos(ang_c), np.cos(ang_c)], axis=-1)
    sin32 = np.concatenate([-np.sin(ang_r), np.sin(ang_r), -np.sin(ang_c), np.sin(ang_c)], axis=-1)
    ones = np.ones((DEC_SEQ, NOPE_A))
    zeros = np.zeros((DEC_SEQ, NOPE_A))
    pad1 = np.ones((DEC_SEQ, LANES - QK_A))
    pad0 = np.zeros((DEC_SEQ, LANES - QK_A))
    cosa = np.tile(np.concatenate([ones, cos32, pad1], axis=-1), (1, H_A))
    sina = np.tile(np.concatenate([zeros, sin32, pad0], axis=-1), (1, H_A))
    cosd = np.tile(cos32, (1, WD // ROPE_A))
    sind = np.tile(sin32, (1, WD // ROPE_A))
    return tuple(jnp.asarray(a, F32) for a in (cosa, sina, cosd, sind))


def _constants():
    lane = np.arange(WD)
    bd32 = (lane[:, None] // DH_B == lane[None, :] // DH_B).astype(np.float32)
    bd64 = (lane[:, None] // (2 * DH_B) == lane[None, :] // (2 * DH_B)).astype(np.float32)
    i = np.arange(CHUNK)
    tri = np.stack([(i[None, :] <= i[:, None]), (i[None, :] >= i[:, None])]).astype(np.float32)
    return dict(bd32=jnp.asarray(bd32, BF16), bd64=jnp.asarray(bd64, BF16),
                tri=jnp.asarray(tri, BF16))


def _pad_last(a, n):
    return jnp.pad(a, [(0, 0)] * (a.ndim - 1) + [(0, n - a.shape[-1])])


def _prep_params(norm1_g, norm2_g, w_in, w_out, mla_q_norm_g, mla_kv_norm_g, w_uq, w_ukv, mla_qk_norm_q,
                 mla_qk_norm_k, diff_q_norm_g, diff_k_norm_g, diff_lq1, diff_lk1, diff_lq2, diff_lk2,
                 diff_subln_g, ssm_conv_w, ssm_conv_b, ssm_A_log, ssm_dt_bias, ssm_D, ssm_norm_g, w_ff1, w_ff2):
    P = _constants()
    P["win_t"] = jnp.swapaxes(w_in, 1, 2).astype(BF16)
    P["wuq"] = _pad_last(w_uq.reshape(DEPTH, Q_RANK, H_A, QK_A), LANES).reshape(DEPTH, Q_RANK, WA).astype(BF16)
    wkv = w_ukv.reshape(DEPTH, KV_RANK, H_A, NOPE_A + V_A)
    P["wuk"] = _pad_last(wkv[..., :NOPE_A], LANES).reshape(DEPTH, KV_RANK, WA).astype(BF16)
    P["wuv"] = wkv[..., NOPE_A:].reshape(DEPTH, KV_RANK, WVA).astype(BF16)
    P["wout"] = w_out.astype(BF16)
    P["wff1"] = w_ff1.astype(BF16)
    P["wff2"] = w_ff2.astype(BF16)
    vec = lambda a: a.reshape(DEPTH, 1, -1)
    P["g1"] = vec(norm1_g)
    P["g2"] = vec(norm2_g)
    P["gq"] = vec(mla_q_norm_g)
    P["gkv"] = vec(mla_kv_norm_g)
    P["gqa"] = vec(jnp.tile(_pad_last(mla_qk_norm_q, LANES), (1, H_A))) * (QK_A ** -0.5)
    P["gka"] = vec(jnp.tile(_pad_last(mla_qk_norm_k, LANES), (1, H_A)))
    P["gdq"] = vec(jnp.tile(diff_q_norm_g, (1, WD // DH_B))) * (DH_B ** -0.5)
    P["gdk"] = vec(jnp.tile(diff_k_norm_g, (1, WD // DH_B)))
    P["gsub"] = vec(jnp.tile(diff_subln_g, (1, H_B)))
    P["lq1"], P["lk1"], P["lq2"], P["lk2"] = vec(diff_lq1), vec(diff_lk1), vec(diff_lq2), vec(diff_lk2)
    P["cw"] = jnp.pad(ssm_conv_w, ((0, 0), (0, 8 - CONV_K), (0, 0)))
    P["cb"] = vec(ssm_conv_b)
    P["alog"] = vec(_pad_last(ssm_A_log.reshape(DEPTH, 2 * H_C), LANES))
    P["dtb"] = vec(_pad_last(ssm_dt_bias.reshape(DEPTH, 2 * H_C), LANES))
    P["dexp"] = vec(jnp.repeat(ssm_D, P_C, axis=-1))
    P["gssm"] = vec(ssm_norm_g)
    P["cosa"], P["sina"], P["cosd"], P["sind"] = _rope_tables()
    return P


def kernel(x_prompt, x_sample, cache_mla_ckv, cache_mla_krope, cache_diff_k, cache_diff_v, state_ssm, c, c_ctx, norm1_g, norm2_g, w_ada, b_ada, w_in, w_out, mla_q_norm_g, mla_kv_norm_g, w_uq, w_ukv, mla_qk_norm_q, mla_qk_norm_k, diff_q_norm_g, diff_k_norm_g, diff_lq1, diff_lk1, diff_lq2, diff_lk2, diff_subln_g, ssm_conv_w, ssm_conv_b, ssm_A_log, ssm_dt_bias, ssm_D, ssm_norm_g, w_ff1, w_ff2):
    P = _prep_params(norm1_g, norm2_g, w_in, w_out, mla_q_norm_g, mla_kv_norm_g, w_uq, w_ukv, mla_qk_norm_q,
                     mla_qk_norm_k, diff_q_norm_g, diff_k_norm_g, diff_lq1, diff_lk1, diff_lq2, diff_lk2,
                     diff_subln_g, ssm_conv_w, ssm_conv_b, ssm_A_log, ssm_dt_bias, ssm_D, ssm_norm_g,
                     w_ff1, w_ff2)

    cvec = jnp.concatenate([c, c_ctx[None, :], jnp.zeros((8 - DEC_BATCH - 1, D_MODEL), F32)], axis=0)
    mod4 = _mod_call(cvec, w_ada, b_ada).reshape(DEPTH, 8, 1, N_MOD * D_MODEL)

    kca, vca = _cache_call(cache_mla_ckv, _pad_last(cache_mla_krope, LANES), P)
    kcd = cache_diff_k.reshape(DEC_BATCH, DEPTH, PAST_LEN, WD).astype(BF16)
    vcd = cache_diff_v.reshape(DEC_BATCH, DEPTH, PAST_LEN, WD).astype(BF16)
    st = jnp.transpose(state_ssm, (0, 1, 2, 5, 3, 4)).reshape(DEC_BATCH, DEPTH, 2, N_C, D_INNER)
    half = D_INNER // G_C
    zero = jnp.zeros_like(st[..., :half])
    s0 = jnp.concatenate([jnp.concatenate([st[..., :half], zero], axis=-1),
                          jnp.concatenate([zero, st[..., half:]], axis=-1)], axis=-2)

    xp = x_prompt.reshape(BATCH * SEQ, D_MODEL)
    xs = x_sample.reshape(DEC_BATCH * DEC_SEQ, D_MODEL)
    ckv_l, krope_l, kd_l, vd_l, st_l = [], [], [], [], []
    for l in range(DEPTH):
        qa, ka, va, qd, kd, vd, z, xbc, dt, ckvn, krope, kd32, vd32 = _pre_call(xp, mod4, P, l, False)
        r3 = lambda a: a.reshape(BATCH, SEQ, a.shape[-1])
        oat = _attn_call(r3(qa), r3(ka), r3(va), r3(qd), r3(kd), r3(vd), None, P, l)
        oc, st_new = _ssd_call(r3(z), r3(xbc), r3(dt), None, P, l)
        xp = _post_call(xp, oat.reshape(BATCH * SEQ, -1), oc.reshape(BATCH * SEQ, -1), mod4, P, l, False)
        ckv_l.append(ckvn.reshape(BATCH, SEQ, KV_RANK))
        krope_l.append(krope.reshape(BATCH, SEQ, ROPE_A))
        kd_l.append(jnp.swapaxes(kd32, 1, 2).reshape(BATCH, SEQ, H_B, 2 * DH_B))
        vd_l.append(jnp.swapaxes(vd32, 1, 2).reshape(BATCH, SEQ, H_B, 2 * DH_B))
        st_l.append(st_new.reshape(BATCH, 2, H_C, P_C, N_C))
        qa, ka, va, qd, kd, vd, z, xbc, dt = _pre_call(xs, mod4, P, l, True)
        r3 = lambda a: a.reshape(DEC_BATCH, DEC_SEQ, a.shape[-1])
        oat = _attn_call(r3(qa), r3(ka), r3(va), r3(qd), r3(kd), r3(vd), (kca, vca, kcd, vcd), P, l)
        oc, = _ssd_call(r3(z), r3(xbc), r3(dt), s0, P, l)
        xs = _post_call(xs, oat.reshape(DEC_BATCH * DEC_SEQ, -1), oc.reshape(DEC_BATCH * DEC_SEQ, -1),
                        mod4, P, l, True)
    return (xp.reshape(BATCH, SEQ, D_MODEL), xs.reshape(DEC_BATCH, DEC_SEQ, D_MODEL),
            jnp.stack(ckv_l, axis=1), jnp.stack(krope_l, axis=1), jnp.stack(kd_l, axis=1),
            jnp.stack(vd_l, axis=1), jnp.stack(st_l, axis=1))
```

```python
import functools
import math

import numpy as np
import jax
import jax.numpy as jnp
from jax import lax
from jax.experimental import pallas as pl
from jax.experimental.pallas import tpu as pltpu

D_MODEL = 1024
BATCH = 16
SEQ = 256
DEPTH = 4
DEC_BATCH = 4
DEC_SEQ = 2048
PAST_LEN = 256
GRID_W = 64
ROPE_BASE = 10000.0
EPS = 1e-6
CHUNK = 128
D_FF = 4 * D_MODEL
N_MOD = 6
H_A = 4
NOPE_A = 64
ROPE_A = 32
V_A = 64
Q_RANK = 256
KV_RANK = 128
H_B = 4
DH_B = 32
H_C = 8
P_C = 64
N_C = 64
G_C = 2
CONV_K = 5
D_INNER = H_C * P_C
CONV_DIM = D_INNER + 2 * G_C * N_C

F32 = jnp.float32
BF16 = jnp.bfloat16

LANES = 128
QK_A = NOPE_A + ROPE_A
WA = H_A * LANES
WD = H_B * 2 * DH_B
WVA = H_A * V_A
R_CQ, R_CKV, R_KR, R_DQ, R_DK, R_DV, R_Z, R_XBC, R_DT, D_IN_PROJ = (
    0, 256, 384, 416, 672, 928, 1184, 1696, 2464, 2480)
V7X_VMEM_LIMIT = 56 * 1024 * 1024
TM_TOK = 512
TM_PRE = 1024
TQ = 512
FF_CHUNK = 1024
CONV_COLS = 256


def _mm(a, b):
    return jnp.dot(a, b, preferred_element_type=F32)


def _mm_nt(a, b):
    return lax.dot_general(a, b, (((1,), (1,)), ((), ())), preferred_element_type=F32)


def _split3(x):
    hi = x.astype(BF16)
    r = x - hi.astype(F32)
    mid = r.astype(BF16)
    lo = (r - mid.astype(F32)).astype(BF16)
    return hi, mid, lo


def _sigmoid(x):
    return 1.0 / (1.0 + jnp.exp(-x))


def _rms(x, n=None):
    n = x.shape[-1] if n is None else n
    return x * lax.rsqrt(jnp.sum(x * x, axis=-1, keepdims=True) * (1.0 / n) + EPS)


def _rms_heads128(x, n_real):
    parts = [_rms(x[:, i * LANES:(i + 1) * LANES], n_real) for i in range(x.shape[1] // LANES)]
    return jnp.concatenate(parts, axis=-1)


def _rms_segments(x, bd, seg):
    sq = x * x
    hi = sq.astype(BF16)
    lo = (sq - hi.astype(F32)).astype(BF16)
    ssq = _mm(hi, bd) + _mm(lo, bd)
    return x * lax.rsqrt(ssq * (1.0 / seg) + EPS)


def _rope(x, cos, sin_signed):
    lane = lax.broadcasted_iota(jnp.int32, (1, LANES), 1)
    lo = (lane % 16) < 8
    parts = []
    for i in range(x.shape[1] // LANES):
        xb = x[:, i * LANES:(i + 1) * LANES]
        parts.append(jnp.where(lo, pltpu.roll(xb, LANES - 8, 1), pltpu.roll(xb, 8, 1)))
    return x * cos + jnp.concatenate(parts, axis=-1) * sin_signed


def _mod_kernel(c_ref, w_ref, b_ref, o_ref):
    cv = c_ref[...]
    s = cv * _sigmoid(cv)
    o_ref[0] = _mm(s.astype(BF16), w_ref[0].astype(BF16)) + b_ref[0]


def _mod_call(cvec, w_ada, b_ada):
    tn = 1536
    n_out = N_MOD * D_MODEL
    return pl.pallas_call(
        _mod_kernel,
        grid=(DEPTH, n_out // tn),
        in_specs=[pl.BlockSpec((8, D_MODEL), lambda l, j: (0, 0)),
                  pl.BlockSpec((1, D_MODEL, tn), lambda l, j: (l, 0, j)),
                  pl.BlockSpec((1, 1, tn), lambda l, j: (l, 0, j))],
        out_specs=pl.BlockSpec((1, 8, tn), lambda l, j: (l, 0, j)),
        out_shape=jax.ShapeDtypeStruct((DEPTH, 8, n_out), F32),
        compiler_params=pltpu.CompilerParams(dimension_semantics=("arbitrary", "arbitrary")),
        name="mod",
    )(cvec, w_ada, b_ada.reshape(DEPTH, 1, n_out))


def _tile_rope_key(kr):
    return jnp.concatenate([pltpu.roll(kr, NOPE_A, 1)] * H_A, axis=-1)


def _cache_kernel(ckv_ref, kr_ref, wuk_ref, wuv_ref, gka_ref, k_ref, v_ref):
    ckv = ckv_ref[0, 0].astype(BF16)
    kpre = _mm(ckv, wuk_ref[0]) + _tile_rope_key(kr_ref[0, 0])
    k_ref[0, 0] = (_rms_heads128(kpre, QK_A) * gka_ref[0]).astype(BF16)
    v_ref[0, 0] = _mm(ckv, wuv_ref[0]).astype(BF16)


def _cache_call(cache_ckv, cache_kr_pad, P):
    bl = lambda b, l: (b, l, 0, 0)
    lay = lambda b, l: (l, 0, 0)
    return pl.pallas_call(
        _cache_kernel,
        grid=(DEC_BATCH, DEPTH),
        in_specs=[pl.BlockSpec((1, 1, PAST_LEN, KV_RANK), bl),
                  pl.BlockSpec((1, 1, PAST_LEN, LANES), bl),
                  pl.BlockSpec((1, KV_RANK, WA), lay),
                  pl.BlockSpec((1, KV_RANK, WVA), lay),
                  pl.BlockSpec((1, 1, WA), lay)],
        out_specs=[pl.BlockSpec((1, 1, PAST_LEN, WA), bl),
                   pl.BlockSpec((1, 1, PAST_LEN, WVA), bl)],
        out_shape=[jax.ShapeDtypeStruct((DEC_BATCH, DEPTH, PAST_LEN, WA), BF16),
                   jax.ShapeDtypeStruct((DEC_BATCH, DEPTH, PAST_LEN, WVA), BF16)],
        compiler_params=pltpu.CompilerParams(dimension_semantics=("arbitrary", "arbitrary")),
        name="cache_expand",
    )(cache_ckv, cache_kr_pad, P["wuk"], P["wuv"], P["gka"])


def _pre_kernel(latent, *refs):
    (x_ref, mod_ref, g1_ref, wint_ref, gq_ref, gkv_ref, wuq_ref, wuk_ref, wuv_ref,
     gqa_ref, gka_ref, gdq_ref, gdk_ref, bd32_ref) = refs[:14]
    refs = refs[14:]
    if latent:
        cosa_ref, sina_ref, cosd_ref, sind_ref = refs[:4]
        refs = refs[4:]
    qa_ref, ka_ref, va_ref, qd_ref, kd_ref, vd_ref, z_ref, xbc_ref, dt_ref = refs[:9]
    refs = refs[9:]
    if not latent:
        ckvn_ref, krope_ref, kd32_ref, vd32_ref = refs

    x = x_ref[...]
    mod = mod_ref[0, 0]
    shift1 = mod[:, 0:D_MODEL]
    scale1 = mod[:, D_MODEL:2 * D_MODEL]
    h = (_rms(x) * g1_ref[0] * (1.0 + scale1) + shift1).astype(BF16)

    def proj(r0, r1):
        return _mm_nt(h, wint_ref[0, r0:r1, :])

    lane128 = lax.broadcasted_iota(jnp.int32, (1, LANES), 1)

    cqn = (_rms(proj(R_CQ, R_CKV)) * gq_ref[0]).astype(BF16)
    q = _rms_heads128(_mm(cqn, wuq_ref[0]), QK_A) * gqa_ref[0]
    if latent:
        q = _rope(q, cosa_ref[...], sina_ref[...])
    qa_ref[...] = q.astype(BF16)

    ckvn = _rms(proj(R_CKV, R_KR)) * gkv_ref[0]
    kr = jnp.where(lane128 < ROPE_A, proj(R_KR, R_KR + LANES), 0.0)
    ckvb = ckvn.astype(BF16)
    k = _rms_heads128(_mm(ckvb, wuk_ref[0]) + _tile_rope_key(kr), QK_A) * gka_ref[0]
    if latent:
        k = _rope(k, cosa_ref[...], sina_ref[...])
    ka_ref[...] = k.astype(BF16)
    va_ref[...] = _mm(ckvb, wuv_ref[0]).astype(BF16)

    bd32 = bd32_ref[...]
    qd = _rms_segments(proj(R_DQ, R_DK), bd32, DH_B) * gdq_ref[0]
    kd = _rms_segments(proj(R_DK, R_DV), bd32, DH_B) * gdk_ref[0]
    vd = proj(R_DV, R_Z)
    if latent:
        qd = _rope(qd, cosd_ref[...], sind_ref[...])
        kd = _rope(kd, cosd_ref[...], sind_ref[...])
    qd_ref[...] = qd.astype(BF16)
    kd_ref[...] = kd.astype(BF16)
    vd_ref[...] = vd.astype(BF16)

    z_ref[...] = proj(R_Z, R_XBC)
    xbc_ref[...] = proj(R_XBC, R_DT)
    dt_tail = pltpu.roll(proj(D_IN_PROJ - LANES, D_IN_PROJ), 2 * H_C, 1)
    dt_ref[...] = jnp.where(lane128 < 2 * H_C, dt_tail, 0.0)

    if not latent:
        ckvn_ref[...] = ckvn
        krope_ref[...] = kr[:, 0:ROPE_A]
        for bb in range(x_ref.shape[0] // SEQ):
            kd32_ref[bb] = kd[bb * SEQ:(bb + 1) * SEQ, :].T
            vd32_ref[bb] = vd[bb * SEQ:(bb + 1) * SEQ, :].T


def _pre_call(x2d, mod4, P, l, latent):
    T = x2d.shape[0]
    tm = TM_PRE
    tpb = DEC_SEQ // tm
    row = lambda i: (i, 0)
    lay = lambda i: (l, 0, 0)
    if latent:
        mod_map = lambda i: (l, i // tpb, 0, 0)
    else:
        mod_map = lambda i: (l, DEC_BATCH, 0, 0)
    full = lambda i: (0, 0)
    in_specs = [pl.BlockSpec((tm, D_MODEL), row),
                pl.BlockSpec((1, 1, 1, N_MOD * D_MODEL), mod_map),
                pl.BlockSpec((1, 1, D_MODEL), lay),
                pl.BlockSpec((1, D_IN_PROJ, D_MODEL), lay),
                pl.BlockSpec((1, 1, Q_RANK), lay),
                pl.BlockSpec((1, 1, KV_RANK), lay),
                pl.BlockSpec((1, Q_RANK, WA), lay),
                pl.BlockSpec((1, KV_RANK, WA), lay),
                pl.BlockSpec((1, KV_RANK, WVA), lay),
                pl.BlockSpec((1, 1, WA), lay),
                pl.BlockSpec((1, 1, WA), lay),
                pl.BlockSpec((1, 1, WD), lay),
                pl.BlockSpec((1, 1, WD), lay),
                pl.BlockSpec((WD, WD), full)]
    args = [x2d, mod4, P["g1"], P["win_t"], P["gq"], P["gkv"], P["wuq"], P["wuk"], P["wuv"],
            P["gqa"], P["gka"], P["gdq"], P["gdk"], P["bd32"]]
    if latent:
        pos = lambda i: (i % tpb, 0)
        in_specs += [pl.BlockSpec((tm, WA), pos), pl.BlockSpec((tm, WA), pos),
                     pl.BlockSpec((tm, WD), pos), pl.BlockSpec((tm, WD), pos)]
        args += [P["cosa"], P["sina"], P["cosd"], P["sind"]]
    widths = [(WA, BF16), (WA, BF16), (WVA, BF16), (WD, BF16), (WD, BF16), (WD, BF16),
              (D_INNER, F32), (CONV_DIM, F32), (LANES, F32)]
    if not latent:
        widths += [(KV_RANK, F32), (ROPE_A, F32), (WD, F32), (WD, F32)]
    out_specs = [pl.BlockSpec((tm, w), row) for w, _ in widths]
    out_shape = [jax.ShapeDtypeStruct((T, w), dt) for w, dt in widths]
    if not latent:
        for i_t in (11, 12):
            out_specs[i_t] = pl.BlockSpec((tm // SEQ, WD, SEQ), lambda i: (i, 0, 0))
            out_shape[i_t] = jax.ShapeDtypeStruct((T // SEQ, WD, SEQ), F32)
    return pl.pallas_call(
        functools.partial(_pre_kernel, latent),
        grid=(T // tm,),
        in_specs=in_specs, out_specs=out_specs, out_shape=out_shape,
        compiler_params=pltpu.CompilerParams(dimension_semantics=("arbitrary",),
                                             vmem_limit_bytes=V7X_VMEM_LIMIT),
        name="pre_lat" if latent else "pre_ctx",
    )(*args)


def _softmax(q, key_refs, c0, c1):
    s = [_mm_nt(q, kr[:, c0:c1]) for kr in key_refs]
    m = s[0].max(axis=-1, keepdims=True)
    for si in s[1:]:
        m = jnp.maximum(m, si.max(axis=-1, keepdims=True))
    p = [jnp.exp(si - m) for si in s]
    den = p[0].sum(axis=-1, keepdims=True)
    for pi in p[1:]:
        den = den + pi.sum(axis=-1, keepdims=True)
    return p, den


def _pv(p, val_refs):
    pv = _mm(p[0].astype(BF16), val_refs[0][...])
    for pi, vr in zip(p[1:], val_refs[1:]):
        pv = pv + _mm(pi.astype(BF16), vr[...])
    return pv


def _attn_kernel(has_cache, lam_init, *refs):
    qa_ref, ka_ref, va_ref, qd_ref, kd_ref, vd_ref = refs[:6]
    refs = refs[6:]
    if has_cache:
        kca_ref, vca_ref, kcd_ref, vcd_ref = refs[:4]
        refs = refs[4:]
    lq1_ref, lk1_ref, lq2_ref, lk2_ref, gsub_ref, bd64_ref, o_ref = refs

    ka_refs = [ka_ref.at[0]]
    va_refs = [va_ref.at[0]]
    kd_refs = [kd_ref.at[0]]
    vd_refs = [vd_ref.at[0]]
    if has_cache:
        ka_refs = [kca_ref.at[0, 0]] + ka_refs
        va_refs = [vca_ref.at[0, 0]] + va_refs
        kd_refs = [kcd_ref.at[0, 0]] + kd_refs
        vd_refs = [vcd_ref.at[0, 0]] + vd_refs

    lane256 = lax.broadcasted_iota(jnp.int32, (1, WD), 1)
    lane128 = lax.broadcasted_iota(jnp.int32, (1, LANES), 1)

    qa = qa_ref[0]
    oa = None
    for hh in range(H_A):
        p, den = _softmax(qa[:, hh * LANES:(hh + 1) * LANES], ka_refs, hh * LANES, (hh + 1) * LANES)
        contrib = jnp.where(lane256 // V_A == hh, _pv(p, va_refs) * (1.0 / den), 0.0)
        oa = contrib if oa is None else oa + contrib

    lam = (jnp.exp(jnp.sum(lq1_ref[0] * lk1_ref[0], axis=-1, keepdims=True))
           - jnp.exp(jnp.sum(lq2_ref[0] * lk2_ref[0], axis=-1, keepdims=True)) + lam_init)
    qd = qd_ref[0]
    od = None
    for hh in range(H_B):
        blk = hh // 2
        qblk = qd[:, blk * LANES:(blk + 1) * LANES]
        maps = []
        for mm_i in range(2):
            seg = (hh % 2) * 2 + mm_i
            qm = jnp.where(lane128 // DH_B == seg, qblk, jnp.zeros_like(qblk))
            p, den = _softmax(qm, kd_refs, blk * LANES, (blk + 1) * LANES)
            maps.append(_pv(p, vd_refs) * (1.0 / den))
        contrib = jnp.where(lane256 // (2 * DH_B) == hh, maps[0] - lam * maps[1], 0.0)
        od = contrib if od is None else od + contrib
    od = _rms_segments(od, bd64_ref[...], 2 * DH_B) * (gsub_ref[0] * (1.0 - lam_init))
    o_ref[0] = jnp.concatenate([oa, od], axis=-1).astype(BF16)


def _attn_call(qa, ka, va, qd, kd, vd, caches, P, l):
    B, L = qa.shape[0], qa.shape[1]
    has_cache = caches is not None
    lam_init = 0.8 - 0.6 * math.exp(-0.3 * l)
    qmap = lambda b, i: (b, i, 0)
    kmap = lambda b, i: (b, 0, 0)
    lay = lambda b, i: (l, 0, 0)
    tq = min(TQ, L)
    in_specs = [pl.BlockSpec((1, tq, WA), qmap), pl.BlockSpec((1, L, WA), kmap),
                pl.BlockSpec((1, L, WVA), kmap), pl.BlockSpec((1, tq, WD), qmap),
                pl.BlockSpec((1, L, WD), kmap), pl.BlockSpec((1, L, WD), kmap)]
    args = [qa, ka, va, qd, kd, vd]
    if has_cache:
        cmap = lambda b, i: (b, l, 0, 0)
        in_specs += [pl.BlockSpec((1, 1, PAST_LEN, WA), cmap), pl.BlockSpec((1, 1, PAST_LEN, WVA), cmap),
                     pl.BlockSpec((1, 1, PAST_LEN, WD), cmap), pl.BlockSpec((1, 1, PAST_LEN, WD), cmap)]
        args += list(caches)
    in_specs += [pl.BlockSpec((1, 1, DH_B), lay)] * 4
    in_specs += [pl.BlockSpec((1, 1, WD), lay), pl.BlockSpec((WD, WD), lambda b, i: (0, 0))]
    args += [P["lq1"], P["lk1"], P["lq2"], P["lk2"], P["gsub"], P["bd64"]]
    return pl.pallas_call(
        functools.partial(_attn_kernel, has_cache, lam_init),
        grid=(B, L // tq),
        in_specs=in_specs,
        out_specs=pl.BlockSpec((1, tq, WVA + WD), qmap),
        out_shape=jax.ShapeDtypeStruct((B, L, WVA + WD), BF16),
        compiler_params=pltpu.CompilerParams(dimension_semantics=("arbitrary", "arbitrary"),
                                             vmem_limit_bytes=V7X_VMEM_LIMIT),
        name="attn_lat" if has_cache else "attn_ctx",
    )(*args)


def _ssd_kernel(L, has_h0, *refs):
    z_ref, xbc_ref, dt_ref = refs[:3]
    refs = refs[3:]
    if has_h0:
        s0_ref = refs[0]
        refs = refs[1:]
    cw_ref, cb_ref, alog_ref, dtb_ref, dexp_ref, g_ref, tri_ref = refs[:7]
    refs = refs[7:]
    o_ref = refs[0]
    refs = refs[1:]
    if not has_h0:
        st_ref = refs[0]
        refs = refs[1:]
    xpad, xc, yacc, cum_s, bt_s, cumt_s, dtt_s, wt_s, el_s, s_scr = refs
    nc = L // CHUNK
    halo = 8
    nd = 2 * H_C

    xpad[0:halo, :] = jnp.zeros((halo, CONV_DIM), F32)
    xpad[L + halo:L + 2 * halo, :] = jnp.zeros((halo, CONV_DIM), F32)
    xpad[halo:L + halo, :] = xbc_ref[0]
    cw = cw_ref[0]
    cb = cb_ref[0]
    dtb = dtb_ref[0]
    a_neg = -jnp.exp(alog_ref[0])
    lane128 = lax.broadcasted_iota(jnp.int32, (1, LANES), 1)
    fwd_lane = lane128 < H_C
    tri_f = tri_ref[0]
    tri_b = tri_ref[1]

    static_prep = nc <= 2

    def prep_body(c, carry):
        base = c * CHUNK if static_prep else pl.multiple_of(c * CHUNK, CHUNK)
        accs = []
        for g0 in range(0, CONV_DIM, CONV_COLS):
            a = cb[:, g0:g0 + CONV_COLS]
            if not static_prep:
                win = xpad[pl.ds(base, CHUNK + 2 * halo), g0:g0 + CONV_COLS]
            for k in range(CONV_K):
                off = halo - CONV_K // 2 + k
                if static_prep:
                    tap = xpad[base + off:base + off + CHUNK, g0:g0 + CONV_COLS]
                else:
                    tap = win[off:off + CHUNK, :]
                a = a + tap * cw[k:k + 1, g0:g0 + CONV_COLS]
            accs.append(a)
        acc = jnp.concatenate(accs, axis=-1)
        act = acc * _sigmoid(acc)
        xc[pl.ds(base, CHUNK), :] = act
        bt_s[pl.ds(base, CHUNK), :] = act[:, D_INNER:D_INNER + LANES].T
        dtr = dt_ref[0, pl.ds(base, CHUNK), :] + dtb
        dtc = jnp.maximum(dtr, 0.0) + jnp.log(1.0 + jnp.exp(-jnp.abs(dtr)))
        hi, mid, lo = _split3(dtc * a_neg)
        cum_f = _mm(tri_f, hi) + _mm(tri_f, mid) + _mm(tri_f, lo)
        cum_b = _mm(tri_b, hi) + _mm(tri_b, mid) + _mm(tri_b, lo)
        cum = jnp.where(fwd_lane, cum_f, cum_b)
        last = jnp.where(fwd_lane, cum[CHUNK - 1:CHUNK, :], cum[0:1, :])
        cum_s[pl.ds(base, CHUNK), :] = cum
        rows = pl.ds(c * nd, nd) if static_prep else pl.ds(pl.multiple_of(c * nd, nd), nd)
        cumt_s[rows, :] = cum.T[0:nd, :]
        dtt_s[rows, :] = dtc.T[0:nd, :]
        wt_s[rows, :] = (jnp.exp(last - cum) * dtc).T[0:nd, :]
        el_rows = pl.ds(c * 8, 8) if static_prep else pl.ds(pl.multiple_of(c * 8, 8), 8)
        el_s[el_rows, :] = jnp.broadcast_to(jnp.exp(last), (8, LANES))
        return carry

    if static_prep:
        for c_static in range(nc):
            prep_body(c_static, 0)
    else:
        lax.fori_loop(0, nc, prep_body, 0, unroll=2)

    row_i = lax.broadcasted_iota(jnp.int32, (CHUNK, CHUNK), 0)
    col_j = lax.broadcasted_iota(jnp.int32, (CHUNK, CHUNK), 1)
    blockmask = (lax.broadcasted_iota(jnp.int32, (2 * N_C, D_INNER), 0) // N_C
                 == lax.broadcasted_iota(jnp.int32, (2 * N_C, D_INNER), 1) // (D_INNER // G_C))
    dexp = dexp_ref[0]
    gnorm = g_ref[0]

    for d in range(2):
        if has_h0:
            s_scr[...] = s0_ref[0, 0, d]
        else:
            s_scr[...] = jnp.zeros((2 * N_C, D_INNER), F32)
        causal = (row_i >= col_j) if d == 0 else (col_j >= row_i)

        def per_head_lanes(cols, d=d):
            parts = []
            for pair in range(H_C // 2):
                i0 = d * H_C + 2 * pair
                parts.append(jnp.where(lane128 < P_C, cols[:, i0:i0 + 1], cols[:, i0 + 1:i0 + 2]))
            return jnp.concatenate(parts, axis=-1)

        def chunk_body(step, carry, d=d, causal=causal, per_head_lanes=per_head_lanes):
            c = step if d == 0 else nc - 1 - step
            base = pl.multiple_of(c * CHUNK, CHUNK)
            rows = pl.ds(pl.multiple_of(c * nd, nd), nd)
            xs = xc[pl.ds(base, CHUNK), 0:D_INNER]
            c_c = xc[pl.ds(base, CHUNK), D_INNER + LANES:D_INNER + 2 * LANES]
            cum = cum_s[pl.ds(base, CHUNK), :]
            cum_t = cumt_s[rows, :]
            dt_t = dtt_s[rows, :]
            w_t = wt_s[rows, :]
            b_t = bt_s[pl.ds(base, CHUNK), :]
            cb16 = c_c.astype(BF16)
            bt16 = b_t.astype(BF16)
            cbs = [_mm(jnp.where(lane128 // N_C == g, cb16, jnp.zeros_like(cb16)), bt16)
                   for g in range(G_C)]
            sv = s_scr[...]

            def blockdiag(v):
                return jnp.concatenate([jnp.where(lane128 < P_C, v, 0.0),
                                        jnp.where(lane128 < P_C, 0.0, v)], axis=0).astype(BF16)

            ys, upds = [], []
            for pair in range(H_C // 2):
                xbd = blockdiag(xs[:, pair * LANES:(pair + 1) * LANES])
                sbd = blockdiag(sv[:, pair * LANES:(pair + 1) * LANES])
                ms, es, ws = [], [], []
                for hh in (2 * pair, 2 * pair + 1):
                    idx = d * H_C + hh
                    col = jnp.broadcast_to(cum[:, idx:idx + 1], (CHUNK, CHUNK))
                    dec = jnp.where(causal, jnp.exp(col - cum_t[idx:idx + 1, :]), 0.0)
                    ms.append((cbs[hh // (H_C // G_C)] * dec * dt_t[idx:idx + 1, :]).astype(BF16))
                    es.append((jnp.exp(col) * c_c).astype(BF16))
                    ws.append((b_t * w_t[idx:idx + 1, :]).astype(BF16))
                ys.append(_mm(jnp.concatenate(ms + es, axis=-1), jnp.concatenate([xbd, sbd], axis=0)))
                upds.append(_mm(jnp.concatenate(ws, axis=-1), xbd))
            y = jnp.concatenate(ys, axis=-1)
            cd = per_head_lanes(el_s[pl.ds(pl.multiple_of(c * 8, 8), 1), :])
            s_scr[...] = sv * cd + jnp.where(blockmask, jnp.concatenate(upds, axis=-1), 0.0)
            if d == 0:
                yacc[pl.ds(base, CHUNK), :] = y
            else:
                zc = z_ref[0, pl.ds(base, CHUNK), :]
                yt = (yacc[pl.ds(base, CHUNK), :] + y + dexp * xs) * (zc * _sigmoid(zc))
                o_ref[0, pl.ds(base, CHUNK), :] = (_rms(yt) * gnorm).astype(BF16)
            return carry

        lax.fori_loop(0, nc, chunk_body, 0, unroll=2)
        if not has_h0:
            st = s_scr[...].T
            st_ref[0, d] = (st + pltpu.roll(st, N_C, 1))[:, 0:N_C]


def _ssd_call(z, xbc, dt, s0, P, l):
    B, L = z.shape[0], z.shape[1]
    has_h0 = s0 is not None
    bmap = lambda b: (b, 0, 0)
    lay = lambda b: (l, 0, 0)
    in_specs = [pl.BlockSpec((1, L, D_INNER), bmap), pl.BlockSpec((1, L, CONV_DIM), bmap),
                pl.BlockSpec((1, L, LANES), bmap)]
    args = [z, xbc, dt]
    if has_h0:
        in_specs.append(pl.BlockSpec((1, 1, 2, 2 * N_C, D_INNER), lambda b: (b, l, 0, 0, 0)))
        args.append(s0)
    in_specs += [pl.BlockSpec((1, 8, CONV_DIM), lay), pl.BlockSpec((1, 1, CONV_DIM), lay),
                 pl.BlockSpec((1, 1, LANES), lay), pl.BlockSpec((1, 1, LANES), lay),
                 pl.BlockSpec((1, 1, D_INNER), lay), pl.BlockSpec((1, 1, D_INNER), lay),
                 pl.BlockSpec((2, CHUNK, CHUNK), lambda b: (0, 0, 0))]
    args += [P["cw"], P["cb"], P["alog"], P["dtb"], P["dexp"], P["gssm"], P["tri"]]
    out_specs = [pl.BlockSpec((1, L, D_INNER), bmap)]
    out_shape = [jax.ShapeDtypeStruct((B, L, D_INNER), BF16)]
    if not has_h0:
        out_specs.append(pl.BlockSpec((1, 2, D_INNER, N_C), lambda b: (b, 0, 0, 0)))
        out_shape.append(jax.ShapeDtypeStruct((B, 2, D_INNER, N_C), F32))
    return pl.pallas_call(
        functools.partial(_ssd_kernel, L, has_h0),
        grid=(B,),
        in_specs=in_specs, out_specs=out_specs, out_shape=out_shape,
        scratch_shapes=[pltpu.VMEM((L + 16, CONV_DIM), F32), pltpu.VMEM((L, CONV_DIM), F32),
                        pltpu.VMEM((L, D_INNER), F32), pltpu.VMEM((L, LANES), F32),
                        pltpu.VMEM((L, LANES), F32)]
        + [pltpu.VMEM((L // CHUNK * 2 * H_C, LANES), F32)] * 3
        + [pltpu.VMEM((L // CHUNK * 8, LANES), F32), pltpu.VMEM((2 * N_C, D_INNER), F32)],
        compiler_params=pltpu.CompilerParams(dimension_semantics=("arbitrary",),
                                             vmem_limit_bytes=V7X_VMEM_LIMIT),
        name="ssd_lat" if has_h0 else "ssd_ctx",
    )(*args)


def _post_weight_copies(l, wo_hbm, w1_hbm, w2_hbm, wo_s, w1_s, w2_s, sem):
    copies = [pltpu.make_async_copy(wo_hbm.at[l], wo_s, sem.at[0])]
    for c in range(D_FF // FF_CHUNK):
        cols = pl.ds(c * FF_CHUNK, FF_CHUNK)
        copies.append(pltpu.make_async_copy(w1_hbm.at[l, :, cols], w1_s.at[:, cols], sem.at[1 + 2 * c]))
        copies.append(pltpu.make_async_copy(w2_hbm.at[l, cols, :], w2_s.at[cols, :], sem.at[2 + 2 * c]))
    return copies


def _post_body(x_ref, oat_ref, oc_ref, mod_ref, g2_ref, wo_s, w1_s, w2_s, o_ref, copies):
    def ready(k):
        if copies is not None:
            copies[k].wait()

    mod = mod_ref[0, 0]
    gate1 = mod[:, 2 * D_MODEL:3 * D_MODEL]
    gate2 = mod[:, 5 * D_MODEL:6 * D_MODEL]
    n_att = WVA + WD
    ready(0)
    mix = _mm(oat_ref[...], wo_s[0:n_att, :]) + _mm(oc_ref[...], wo_s[n_att:n_att + D_INNER, :])
    x1 = x_ref[...] + gate1 * mix
    h2 = _modulated_norm2(x1, g2_ref[0], mod)
    ff = None
    for c in range(D_FF // FF_CHUNK):
        ready(1 + 2 * c)
        u = jnp.maximum(_mm(h2, w1_s[:, c * FF_CHUNK:(c + 1) * FF_CHUNK]), 0.0)
        ready(2 + 2 * c)
        part = _mm((u * u).astype(BF16), w2_s[c * FF_CHUNK:(c + 1) * FF_CHUNK, :])
        ff = part if ff is None else ff + part
    o_ref[...] = x1 + gate2 * ff


def _modulated_norm2(x1, g2, mod):
    shift2 = mod[:, 3 * D_MODEL:4 * D_MODEL]
    scale2 = mod[:, 4 * D_MODEL:5 * D_MODEL]
    return (_rms(x1) * g2 * (1.0 + scale2) + shift2).astype(BF16)


def _post_kernel(l, x_ref, oat_ref, oc_ref, mod_ref, wo_hbm, g2_ref, w1_hbm, w2_hbm, o_ref, wo_s, w1_s, w2_s, sem):
    first = pl.program_id(0) == 0

    @pl.when(first)
    def _():
        copies = _post_weight_copies(l, wo_hbm, w1_hbm, w2_hbm, wo_s, w1_s, w2_s, sem)
        for cp in copies:
            cp.start()
        _post_body(x_ref, oat_ref, oc_ref, mod_ref, g2_ref, wo_s, w1_s, w2_s, o_ref, copies)

    @pl.when(jnp.logical_not(first))
    def _():
        _post_body(x_ref, oat_ref, oc_ref, mod_ref, g2_ref, wo_s, w1_s, w2_s, o_ref, None)


def _post_call(x2d, oat, oc, mod4, P, l, latent):
    T = x2d.shape[0]
    tm = TM_TOK
    tpb = DEC_SEQ // tm
    row = lambda i: (i, 0)
    lay = lambda i: (l, 0, 0)
    if latent:
        mod_map = lambda i: (l, i // tpb, 0, 0)
    else:
        mod_map = lambda i: (l, DEC_BATCH, 0, 0)
    hbm = pl.BlockSpec(memory_space=pl.ANY)
    return pl.pallas_call(
        functools.partial(_post_kernel, l),
        grid=(T // tm,),
        in_specs=[pl.BlockSpec((tm, D_MODEL), row),
                  pl.BlockSpec((tm, WVA + WD), row),
                  pl.BlockSpec((tm, D_INNER), row),
                  pl.BlockSpec((1, 1, 1, N_MOD * D_MODEL), mod_map),
                  hbm,
                  pl.BlockSpec((1, 1, D_MODEL), lay),
                  hbm,
                  hbm],
        out_specs=pl.BlockSpec((tm, D_MODEL), row),
        out_shape=jax.ShapeDtypeStruct((T, D_MODEL), F32),
        scratch_shapes=[pltpu.VMEM((D_MODEL, D_MODEL), BF16), pltpu.VMEM((D_MODEL, D_FF), BF16),
                        pltpu.VMEM((D_FF, D_MODEL), BF16),
                        pltpu.SemaphoreType.DMA((1 + 2 * (D_FF // FF_CHUNK),))],
        compiler_params=pltpu.CompilerParams(dimension_semantics=("arbitrary",),
                                             vmem_limit_bytes=V7X_VMEM_LIMIT),
        name="post_lat" if latent else "post_ctx",
    )(x2d, oat, oc, mod4, P["wout"], P["g2"], P["wff1"], P["wff2"])


def _rope_tables():
    t = np.arange(DEC_SEQ)
    half = ROPE_A // 2
    freqs = ROPE_BASE ** (-np.arange(0, half, 2, dtype=np.float64) / half)
    ang_r = (t // GRID_W)[:, None] * freqs
    ang_c = (t % GRID_W)[:, None] * freqs
    cos32 = np.concatenate([np.cos(ang_r), np.cos(ang_r), np.cos(ang_c), np.cos(ang_c)], axis=-1)
    sin32 = np.concatenate([-np.sin(ang_r), np.sin(ang_r), -np.sin(ang_c), np.sin(ang_c)], axis=-1)
    ones = np.ones((DEC_SEQ, NOPE_A))
    zeros = np.zeros((DEC_SEQ, NOPE_A))
    pad1 = np.ones((DEC_SEQ, LANES - QK_A))
    pad0 = np.zeros((DEC_SEQ, LANES - QK_A))
    cosa = np.tile(np.concatenate([ones, cos32, pad1], axis=-1), (1, H_A))
    sina = np.tile(np.concatenate([zeros, sin32, pad0], axis=-1), (1, H_A))
    cosd = np.tile(cos32, (1, WD // ROPE_A))
    sind = np.tile(sin32, (1, WD // ROPE_A))
    return tuple(jnp.asarray(a, F32) for a in (cosa, sina, cosd, sind))


def _constants():
    lane = np.arange(WD)
    bd32 = (lane[:, None] // DH_B == lane[None, :] // DH_B).astype(np.float32)
    bd64 = (lane[:, None] // (2 * DH_B) == lane[None, :] // (2 * DH_B)).astype(np.float32)
    i = np.arange(CHUNK)
    tri = np.stack([(i[None, :] <= i[:, None]), (i[None, :] >= i[:, None])]).astype(np.float32)
    return dict(bd32=jnp.asarray(bd32, BF16), bd64=jnp.asarray(bd64, BF16),
                tri=jnp.asarray(tri, BF16))


def _pad_last(a, n):
    return jnp.pad(a, [(0, 0)] * (a.ndim - 1) + [(0, n - a.shape[-1])])


def _prep_params(norm1_g, norm2_g, w_in, w_out, mla_q_norm_g, mla_kv_norm_g, w_uq, w_ukv, mla_qk_norm_q,
                 mla_qk_norm_k, diff_q_norm_g, diff_k_norm_g, diff_lq1, diff_lk1, diff_lq2, diff_lk2,
                 diff_subln_g, ssm_conv_w, ssm_conv_b, ssm_A_log, ssm_dt_bias, ssm_D, ssm_norm_g, w_ff1, w_ff2):
    P = _constants()
    P["win_t"] = jnp.swapaxes(w_in, 1, 2).astype(BF16)
    P["wuq"] = _pad_last(w_uq.reshape(DEPTH, Q_RANK, H_A, QK_A), LANES).reshape(DEPTH, Q_RANK, WA).astype(BF16)
    wkv = w_ukv.reshape(DEPTH, KV_RANK, H_A, NOPE_A + V_A)
    P["wuk"] = _pad_last(wkv[..., :NOPE_A], LANES).reshape(DEPTH, KV_RANK, WA).astype(BF16)
    P["wuv"] = wkv[..., NOPE_A:].reshape(DEPTH, KV_RANK, WVA).astype(BF16)
    P["wout"] = w_out.astype(BF16)
    P["wff1"] = w_ff1.astype(BF16)
    P["wff2"] = w_ff2.astype(BF16)
    vec = lambda a: a.reshape(DEPTH, 1, -1)
    P["g1"] = vec(norm1_g)
    P["g2"] = vec(norm2_g)
    P["gq"] = vec(mla_q_norm_g)
    P["gkv"] = vec(mla_kv_norm_g)
    P["gqa"] = vec(jnp.tile(_pad_last(mla_qk_norm_q, LANES), (1, H_A))) * (QK_A ** -0.5)
    P["gka"] = vec(jnp.tile(_pad_last(mla_qk_norm_k, LANES), (1, H_A)))
    P["gdq"] = vec(jnp.tile(diff_q_norm_g, (1, WD // DH_B))) * (DH_B ** -0.5)
    P["gdk"] = vec(jnp.tile(diff_k_norm_g, (1, WD // DH_B)))
    P["gsub"] = vec(jnp.tile(diff_subln_g, (1, H_B)))
    P["lq1"], P["lk1"], P["lq2"], P["lk2"] = vec(diff_lq1), vec(diff_lk1), vec(diff_lq2), vec(diff_lk2)
    P["cw"] = jnp.pad(ssm_conv_w, ((0, 0), (0, 8 - CONV_K), (0, 0)))
    P["cb"] = vec(ssm_conv_b)
    P["alog"] = vec(_pad_last(ssm_A_log.reshape(DEPTH, 2 * H_C), LANES))
    P["dtb"] = vec(_pad_last(ssm_dt_bias.reshape(DEPTH, 2 * H_C), LANES))
    P["dexp"] = vec(jnp.repeat(ssm_D, P_C, axis=-1))
    P["gssm"] = vec(ssm_norm_g)
    P["cosa"], P["sina"], P["cosd"], P["sind"] = _rope_tables()
    return P


def kernel(x_prompt, x_sample, cache_mla_ckv, cache_mla_krope, cache_diff_k, cache_diff_v, state_ssm, c, c_ctx, norm1_g, norm2_g, w_ada, b_ada, w_in, w_out, mla_q_norm_g, mla_kv_norm_g, w_uq, w_ukv, mla_qk_norm_q, mla_qk_norm_k, diff_q_norm_g, diff_k_norm_g, diff_lq1, diff_lk1, diff_lq2, diff_lk2, diff_subln_g, ssm_conv_w, ssm_conv_b, ssm_A_log, ssm_dt_bias, ssm_D, ssm_norm_g, w_ff1, w_ff2):
    P = _prep_params(norm1_g, norm2_g, w_in, w_out, mla_q_norm_g, mla_kv_norm_g, w_uq, w_ukv, mla_qk_norm_q,
                     mla_qk_norm_k, diff_q_norm_g, diff_k_norm_g, diff_lq1, diff_lk1, diff_lq2, diff_lk2,
                     diff_subln_g, ssm_conv_w, ssm_conv_b, ssm_A_log, ssm_dt_bias, ssm_D, ssm_norm_g,
                     w_ff1, w_ff2)

    cvec = jnp.concatenate([c, c_ctx[None, :], jnp.zeros((8 - DEC_BATCH - 1, D_MODEL), F32)], axis=0)
    mod4 = _mod_call(cvec, w_ada, b_ada).reshape(DEPTH, 8, 1, N_MOD * D_MODEL)

    kca, vca = _cache_call(cache_mla_ckv, _pad_last(cache_mla_krope, LANES), P)
    kcd = cache_diff_k.reshape(DEC_BATCH, DEPTH, PAST_LEN, WD).astype(BF16)
    vcd = cache_diff_v.reshape(DEC_BATCH, DEPTH, PAST_LEN, WD).astype(BF16)
    st = jnp.transpose(state_ssm, (0, 1, 2, 5, 3, 4)).reshape(DEC_BATCH, DEPTH, 2, N_C, D_INNER)
    half = D_INNER // G_C
    zero = jnp.zeros_like(st[..., :half])
    s0 = jnp.concatenate([jnp.concatenate([st[..., :half], zero], axis=-1),
                          jnp.concatenate([zero, st[..., half:]], axis=-1)], axis=-2)

    xp = x_prompt.reshape(BATCH * SEQ, D_MODEL)
    xs = x_sample.reshape(DEC_BATCH * DEC_SEQ, D_MODEL)
    ckv_l, krope_l, kd_l, vd_l, st_l = [], [], [], [], []
    for l in range(DEPTH):
        qa, ka, va, qd, kd, vd, z, xbc, dt, ckvn, krope, kd32, vd32 = _pre_call(xp, mod4, P, l, False)
        r3 = lambda a: a.reshape(BATCH, SEQ, a.shape[-1])
        oat = _attn_call(r3(qa), r3(ka), r3(va), r3(qd), r3(kd), r3(vd), None, P, l)
        oc, st_new = _ssd_call(r3(z), r3(xbc), r3(dt), None, P, l)
        xp = _post_call(xp, oat.reshape(BATCH * SEQ, -1), oc.reshape(BATCH * SEQ, -1), mod4, P, l, False)
        ckv_l.append(ckvn.reshape(BATCH, SEQ, KV_RANK))
        krope_l.append(krope.reshape(BATCH, SEQ, ROPE_A))
        kd_l.append(jnp.swapaxes(kd32, 1, 2).reshape(BATCH, SEQ, H_B, 2 * DH_B))
        vd_l.append(jnp.swapaxes(vd32, 1, 2).reshape(BATCH, SEQ, H_B, 2 * DH_B))
        st_l.append(st_new.reshape(BATCH, 2, H_C, P_C, N_C))
        qa, ka, va, qd, kd, vd, z, xbc, dt = _pre_call(xs, mod4, P, l, True)
        r3 = lambda a: a.reshape(DEC_BATCH, DEC_SEQ, a.shape[-1])
        oat = _attn_call(r3(qa), r3(ka), r3(va), r3(qd), r3(kd), r3(vd), (kca, vca, kcd, vcd), P, l)
        oc, = _ssd_call(r3(z), r3(xbc), r3(dt), s0, P, l)
        xs = _post_call(xs, oat.reshape(DEC_BATCH * DEC_SEQ, -1), oc.reshape(DEC_BATCH * DEC_SEQ, -1),
                        mod4, P, l, True)
    return (xp.reshape(BATCH, SEQ, D_MODEL), xs.reshape(DEC_BATCH, DEC_SEQ, D_MODEL),
            jnp.stack(ckv_l, axis=1), jnp.stack(krope_l, axis=1), jnp.stack(kd_l, axis=1),
            jnp.stack(vd_l, axis=1), jnp.stack(st_l, axis=1))
```

```python
import functools
import math

import numpy as np
import jax
import jax.numpy as jnp
from jax import lax
from jax.experimental import pallas as pl
from jax.experimental.pallas import tpu as pltpu

D_MODEL = 1024
BATCH = 16
SEQ = 256
DEPTH = 4
DEC_BATCH = 4
DEC_SEQ = 2048
PAST_LEN = 256
GRID_W = 64
ROPE_BASE = 10000.0
EPS = 1e-6
CHUNK = 128
D_FF = 4 * D_MODEL
N_MOD = 6
H_A = 4
NOPE_A = 64
ROPE_A = 32
V_A = 64
Q_RANK = 256
KV_RANK = 128
H_B = 4
DH_B = 32
H_C = 8
P_C = 64
N_C = 64
G_C = 2
CONV_K = 5
D_INNER = H_C * P_C
CONV_DIM = D_INNER + 2 * G_C * N_C

F32 = jnp.float32
BF16 = jnp.bfloat16

LANES = 128
QK_A = NOPE_A + ROPE_A
WA = H_A * LANES
WD = H_B * 2 * DH_B
WVA = H_A * V_A
R_CQ, R_CKV, R_KR, R_DQ, R_DK, R_DV, R_Z, R_XBC, R_DT, D_IN_PROJ = (
    0, 256, 384, 416, 672, 928, 1184, 1696, 2464, 2480)
V7X_VMEM_LIMIT = 56 * 1024 * 1024
TM_TOK = 512
TM_PRE = 1024
TQ = 512
FF_CHUNK = 1024
CONV_COLS = 256


def _mm(a, b):
    return jnp.dot(a, b, preferred_element_type=F32)


def _mm_nt(a, b):
    return lax.dot_general(a, b, (((1,), (1,)), ((), ())), preferred_element_type=F32)


def _split3(x):
    hi = x.astype(BF16)
    r = x - hi.astype(F32)
    mid = r.astype(BF16)
    lo = (r - mid.astype(F32)).astype(BF16)
    return hi, mid, lo


def _sigmoid(x):
    return 1.0 / (1.0 + jnp.exp(-x))


def _rms(x, n=None):
    n = x.shape[-1] if n is None else n
    return x * lax.rsqrt(jnp.sum(x * x, axis=-1, keepdims=True) * (1.0 / n) + EPS)


def _rms_heads128(x, n_real):
    parts = [_rms(x[:, i * LANES:(i + 1) * LANES], n_real) for i in range(x.shape[1] // LANES)]
    return jnp.concatenate(parts, axis=-1)


def _rms_segments(x, bd, seg):
    sq = x * x
    hi = sq.astype(BF16)
    lo = (sq - hi.astype(F32)).astype(BF16)
    ssq = _mm(hi, bd) + _mm(lo, bd)
    return x * lax.rsqrt(ssq * (1.0 / seg) + EPS)


def _rope(x, cos, sin_signed):
    lane = lax.broadcasted_iota(jnp.int32, (1, LANES), 1)
    lo = (lane % 16) < 8
    parts = []
    for i in range(x.shape[1] // LANES):
        xb = x[:, i * LANES:(i + 1) * LANES]
        parts.append(jnp.where(lo, pltpu.roll(xb, LANES - 8, 1), pltpu.roll(xb, 8, 1)))
    return x * cos + jnp.concatenate(parts, axis=-1) * sin_signed


def _mod_kernel(c_ref, w_ref, b_ref, o_ref):
    cv = c_ref[...]
    s = cv * _sigmoid(cv)
    o_ref[0] = _mm(s.astype(BF16), w_ref[0].astype(BF16)) + b_ref[0]


def _mod_call(cvec, w_ada, b_ada):
    tn = 1536
    n_out = N_MOD * D_MODEL
    return pl.pallas_call(
        _mod_kernel,
        grid=(DEPTH, n_out // tn),
        in_specs=[pl.BlockSpec((8, D_MODEL), lambda l, j: (0, 0)),
                  pl.BlockSpec((1, D_MODEL, tn), lambda l, j: (l, 0, j)),
                  pl.BlockSpec((1, 1, tn), lambda l, j: (l, 0, j))],
        out_specs=pl.BlockSpec((1, 8, tn), lambda l, j: (l, 0, j)),
        out_shape=jax.ShapeDtypeStruct((DEPTH, 8, n_out), F32),
        compiler_params=pltpu.CompilerParams(dimension_semantics=("arbitrary", "arbitrary")),
        name="mod",
    )(cvec, w_ada, b_ada.reshape(DEPTH, 1, n_out))


def _tile_rope_key(kr):
    return jnp.concatenate([pltpu.roll(kr, NOPE_A, 1)] * H_A, axis=-1)


def _cache_kernel(ckv_ref, kr_ref, wuk_ref, wuv_ref, gka_ref, k_ref, v_ref):
    for b in range(DEC_BATCH):
        ckv = ckv_ref[b, 0].astype(BF16)
        kpre = _mm(ckv, wuk_ref[0]) + _tile_rope_key(kr_ref[b, 0])
        k_ref[b, 0] = (_rms_heads128(kpre, QK_A) * gka_ref[0]).astype(BF16)
        v_ref[b, 0] = _mm(ckv, wuv_ref[0]).astype(BF16)


def _cache_call(cache_ckv, cache_kr_pad, P):
    bl = lambda l: (0, l, 0, 0)
    lay = lambda l: (l, 0, 0)
    return pl.pallas_call(
        _cache_kernel,
        grid=(DEPTH,),
        in_specs=[pl.BlockSpec((DEC_BATCH, 1, PAST_LEN, KV_RANK), bl),
                  pl.BlockSpec((DEC_BATCH, 1, PAST_LEN, LANES), bl),
                  pl.BlockSpec((1, KV_RANK, WA), lay),
                  pl.BlockSpec((1, KV_RANK, WVA), lay),
                  pl.BlockSpec((1, 1, WA), lay)],
        out_specs=[pl.BlockSpec((DEC_BATCH, 1, PAST_LEN, WA), bl),
                   pl.BlockSpec((DEC_BATCH, 1, PAST_LEN, WVA), bl)],
        out_shape=[jax.ShapeDtypeStruct((DEC_BATCH, DEPTH, PAST_LEN, WA), BF16),
                   jax.ShapeDtypeStruct((DEC_BATCH, DEPTH, PAST_LEN, WVA), BF16)],
        compiler_params=pltpu.CompilerParams(dimension_semantics=("arbitrary",)),
        name="cache_expand",
    )(cache_ckv, cache_kr_pad, P["wuk"], P["wuv"], P["gka"])


def _pre_kernel(latent, *refs):
    (x_ref, mod_ref, g1_ref, wint_ref, gq_ref, gkv_ref, wuq_ref, wuk_ref, wuv_ref,
     gqa_ref, gka_ref, gdq_ref, gdk_ref, bd32_ref) = refs[:14]
    refs = refs[14:]
    if latent:
        cosa_ref, sina_ref, cosd_ref, sind_ref = refs[:4]
        refs = refs[4:]
    qa_ref, ka_ref, va_ref, qd_ref, kd_ref, vd_ref, z_ref, xbc_ref, dt_ref = refs[:9]
    refs = refs[9:]
    if not latent:
        ckvn_ref, krope_ref, kd32_ref, vd32_ref = refs

    x = x_ref[...]
    mod = mod_ref[0, 0]
    shift1 = mod[:, 0:D_MODEL]
    scale1 = mod[:, D_MODEL:2 * D_MODEL]
    h = (_rms(x) * g1_ref[0] * (1.0 + scale1) + shift1).astype(BF16)

    def proj(r0, r1):
        return _mm_nt(h, wint_ref[0, r0:r1, :])

    lane128 = lax.broadcasted_iota(jnp.int32, (1, LANES), 1)

    cqn = (_rms(proj(R_CQ, R_CKV)) * gq_ref[0]).astype(BF16)
    q = _rms_heads128(_mm(cqn, wuq_ref[0]), QK_A) * gqa_ref[0]
    if latent:
        q = _rope(q, cosa_ref[...], sina_ref[...])
    qa_ref[...] = q.astype(BF16)

    ckvn = _rms(proj(R_CKV, R_KR)) * gkv_ref[0]
    kr = jnp.where(lane128 < ROPE_A, proj(R_KR, R_KR + LANES), 0.0)
    ckvb = ckvn.astype(BF16)
    k = _rms_heads128(_mm(ckvb, wuk_ref[0]) + _tile_rope_key(kr), QK_A) * gka_ref[0]
    if latent:
        k = _rope(k, cosa_ref[...], sina_ref[...])
    ka_ref[...] = k.astype(BF16)
    va_ref[...] = _mm(ckvb, wuv_ref[0]).astype(BF16)

    bd32 = bd32_ref[...]
    qd = _rms_segments(proj(R_DQ, R_DK), bd32, DH_B) * gdq_ref[0]
    kd = _rms_segments(proj(R_DK, R_DV), bd32, DH_B) * gdk_ref[0]
    vd = proj(R_DV, R_Z)
    if latent:
        qd = _rope(qd, cosd_ref[...], sind_ref[...])
        kd = _rope(kd, cosd_ref[...], sind_ref[...])
    qd_ref[...] = qd.astype(BF16)
    kd_ref[...] = kd.astype(BF16)
    vd_ref[...] = vd.astype(BF16)

    z_ref[...] = proj(R_Z, R_XBC)
    xbc_ref[...] = proj(R_XBC, R_DT)
    dt_tail = pltpu.roll(proj(D_IN_PROJ - LANES, D_IN_PROJ), 2 * H_C, 1)
    dt_ref[...] = jnp.where(lane128 < 2 * H_C, dt_tail, 0.0)

    if not latent:
        ckvn_ref[...] = ckvn
        krope_ref[...] = kr[:, 0:ROPE_A]
        for bb in range(x_ref.shape[0] // SEQ):
            kd32_ref[bb] = kd[bb * SEQ:(bb + 1) * SEQ, :].T
            vd32_ref[bb] = vd[bb * SEQ:(bb + 1) * SEQ, :].T


def _pre_call(x2d, mod4, P, l, latent):
    T = x2d.shape[0]
    tm = TM_PRE
    tpb = DEC_SEQ // tm
    row = lambda i: (i, 0)
    lay = lambda i: (l, 0, 0)
    if latent:
        mod_map = lambda i: (l, i // tpb, 0, 0)
    else:
        mod_map = lambda i: (l, DEC_BATCH, 0, 0)
    full = lambda i: (0, 0)
    in_specs = [pl.BlockSpec((tm, D_MODEL), row),
                pl.BlockSpec((1, 1, 1, N_MOD * D_MODEL), mod_map),
                pl.BlockSpec((1, 1, D_MODEL), lay),
                pl.BlockSpec((1, D_IN_PROJ, D_MODEL), lay),
                pl.BlockSpec((1, 1, Q_RANK), lay),
                pl.BlockSpec((1, 1, KV_RANK), lay),
                pl.BlockSpec((1, Q_RANK, WA), lay),
                pl.BlockSpec((1, KV_RANK, WA), lay),
                pl.BlockSpec((1, KV_RANK, WVA), lay),
                pl.BlockSpec((1, 1, WA), lay),
                pl.BlockSpec((1, 1, WA), lay),
                pl.BlockSpec((1, 1, WD), lay),
                pl.BlockSpec((1, 1, WD), lay),
                pl.BlockSpec((WD, WD), full)]
    args = [x2d, mod4, P["g1"], P["win_t"], P["gq"], P["gkv"], P["wuq"], P["wuk"], P["wuv"],
            P["gqa"], P["gka"], P["gdq"], P["gdk"], P["bd32"]]
    if latent:
        pos = lambda i: (i % tpb, 0)
        in_specs += [pl.BlockSpec((tm, WA), pos), pl.BlockSpec((tm, WA), pos),
                     pl.BlockSpec((tm, WD), pos), pl.BlockSpec((tm, WD), pos)]
        args += [P["cosa"], P["sina"], P["cosd"], P["sind"]]
    widths = [(WA, BF16), (WA, BF16), (WVA, BF16), (WD, BF16), (WD, BF16), (WD, BF16),
              (D_INNER, F32), (CONV_DIM, F32), (LANES, F32)]
    if not latent:
        widths += [(KV_RANK, F32), (ROPE_A, F32), (WD, F32), (WD, F32)]
    out_specs = [pl.BlockSpec((tm, w), row) for w, _ in widths]
    out_shape = [jax.ShapeDtypeStruct((T, w), dt) for w, dt in widths]
    if not latent:
        for i_t in (11, 12):
            out_specs[i_t] = pl.BlockSpec((tm // SEQ, WD, SEQ), lambda i: (i, 0, 0))
            out_shape[i_t] = jax.ShapeDtypeStruct((T // SEQ, WD, SEQ), F32)
    return pl.pallas_call(
        functools.partial(_pre_kernel, latent),
        grid=(T // tm,),
        in_specs=in_specs, out_specs=out_specs, out_shape=out_shape,
        compiler_params=pltpu.CompilerParams(dimension_semantics=("arbitrary",),
                                             vmem_limit_bytes=V7X_VMEM_LIMIT),
        name="pre_lat" if latent else "pre_ctx",
    )(*args)


def _softmax(q, key_refs, c0, c1):
    s = [_mm_nt(q, kr[:, c0:c1]) for kr in key_refs]
    m = s[0].max(axis=-1, keepdims=True)
    for si in s[1:]:
        m = jnp.maximum(m, si.max(axis=-1, keepdims=True))
    p = [jnp.exp(si - m) for si in s]
    den = p[0].sum(axis=-1, keepdims=True)
    for pi in p[1:]:
        den = den + pi.sum(axis=-1, keepdims=True)
    return p, den


def _pv(p, val_refs):
    pv = _mm(p[0].astype(BF16), val_refs[0][...])
    for pi, vr in zip(p[1:], val_refs[1:]):
        pv = pv + _mm(pi.astype(BF16), vr[...])
    return pv


def _attn_kernel(has_cache, lam_init, *refs):
    qa_ref, ka_ref, va_ref, qd_ref, kd_ref, vd_ref = refs[:6]
    refs = refs[6:]
    if has_cache:
        kca_ref, vca_ref, kcd_ref, vcd_ref = refs[:4]
        refs = refs[4:]
    lq1_ref, lk1_ref, lq2_ref, lk2_ref, gsub_ref, bd64_ref, o_ref = refs

    ka_refs = [ka_ref.at[0]]
    va_refs = [va_ref.at[0]]
    kd_refs = [kd_ref.at[0]]
    vd_refs = [vd_ref.at[0]]
    if has_cache:
        ka_refs = [kca_ref.at[0, 0]] + ka_refs
        va_refs = [vca_ref.at[0, 0]] + va_refs
        kd_refs = [kcd_ref.at[0, 0]] + kd_refs
        vd_refs = [vcd_ref.at[0, 0]] + vd_refs

    lane256 = lax.broadcasted_iota(jnp.int32, (1, WD), 1)
    lane128 = lax.broadcasted_iota(jnp.int32, (1, LANES), 1)

    qa = qa_ref[0]
    oa = None
    for hh in range(H_A):
        p, den = _softmax(qa[:, hh * LANES:(hh + 1) * LANES], ka_refs, hh * LANES, (hh + 1) * LANES)
        contrib = jnp.where(lane256 // V_A == hh, _pv(p, va_refs) * (1.0 / den), 0.0)
        oa = contrib if oa is None else oa + contrib

    lam = (jnp.exp(jnp.sum(lq1_ref[0] * lk1_ref[0], axis=-1, keepdims=True))
           - jnp.exp(jnp.sum(lq2_ref[0] * lk2_ref[0], axis=-1, keepdims=True)) + lam_init)
    qd = qd_ref[0]
    od = None
    for hh in range(H_B):
        blk = hh // 2
        qblk = qd[:, blk * LANES:(blk + 1) * LANES]
        maps = []
        for mm_i in range(2):
            seg = (hh % 2) * 2 + mm_i
            qm = jnp.where(lane128 // DH_B == seg, qblk, jnp.zeros_like(qblk))
            p, den = _softmax(qm, kd_refs, blk * LANES, (blk + 1) * LANES)
            maps.append(_pv(p, vd_refs) * (1.0 / den))
        contrib = jnp.where(lane256 // (2 * DH_B) == hh, maps[0] - lam * maps[1], 0.0)
        od = contrib if od is None else od + contrib
    od = _rms_segments(od, bd64_ref[...], 2 * DH_B) * (gsub_ref[0] * (1.0 - lam_init))
    o_ref[0] = jnp.concatenate([oa, od], axis=-1).astype(BF16)


def _attn_call(qa, ka, va, qd, kd, vd, caches, P, l):
    B, L = qa.shape[0], qa.shape[1]
    has_cache = caches is not None
    lam_init = 0.8 - 0.6 * math.exp(-0.3 * l)
    qmap = lambda b, i: (b, i, 0)
    kmap = lambda b, i: (b, 0, 0)
    lay = lambda b, i: (l, 0, 0)
    tq = min(TQ, L)
    in_specs = [pl.BlockSpec((1, tq, WA), qmap), pl.BlockSpec((1, L, WA), kmap),
                pl.BlockSpec((1, L, WVA), kmap), pl.BlockSpec((1, tq, WD), qmap),
                pl.BlockSpec((1, L, WD), kmap), pl.BlockSpec((1, L, WD), kmap)]
    args = [qa, ka, va, qd, kd, vd]
    if has_cache:
        cmap = lambda b, i: (b, l, 0, 0)
        in_specs += [pl.BlockSpec((1, 1, PAST_LEN, WA), cmap), pl.BlockSpec((1, 1, PAST_LEN, WVA), cmap),
                     pl.BlockSpec((1, 1, PAST_LEN, WD), cmap), pl.BlockSpec((1, 1, PAST_LEN, WD), cmap)]
        args += list(caches)
    in_specs += [pl.BlockSpec((1, 1, DH_B), lay)] * 4
    in_specs += [pl.BlockSpec((1, 1, WD), lay), pl.BlockSpec((WD, WD), lambda b, i: (0, 0))]
    args += [P["lq1"], P["lk1"], P["lq2"], P["lk2"], P["gsub"], P["bd64"]]
    return pl.pallas_call(
        functools.partial(_attn_kernel, has_cache, lam_init),
        grid=(B, L // tq),
        in_specs=in_specs,
        out_specs=pl.BlockSpec((1, tq, WVA + WD), qmap),
        out_shape=jax.ShapeDtypeStruct((B, L, WVA + WD), BF16),
        compiler_params=pltpu.CompilerParams(dimension_semantics=("arbitrary", "arbitrary"),
                                             vmem_limit_bytes=V7X_VMEM_LIMIT),
        name="attn_lat" if has_cache else "attn_ctx",
    )(*args)


def _ssd_kernel(L, has_h0, *refs):
    z_ref, xbc_ref, dt_ref = refs[:3]
    refs = refs[3:]
    if has_h0:
        s0_ref = refs[0]
        refs = refs[1:]
    cw_ref, cb_ref, alog_ref, dtb_ref, dexp_ref, g_ref, tri_ref = refs[:7]
    refs = refs[7:]
    o_ref = refs[0]
    refs = refs[1:]
    if not has_h0:
        st_ref = refs[0]
        refs = refs[1:]
    xpad, xc, yacc, cum_s, bt_s, cumt_s, dtt_s, wt_s, el_s, s_scr = refs
    nc = L // CHUNK
    halo = 8
    nd = 2 * H_C

    xpad[0:halo, :] = jnp.zeros((halo, CONV_DIM), F32)
    xpad[L + halo:L + 2 * halo, :] = jnp.zeros((halo, CONV_DIM), F32)
    xpad[halo:L + halo, :] = xbc_ref[0]
    cw = cw_ref[0]
    cb = cb_ref[0]
    dtb = dtb_ref[0]
    a_neg = -jnp.exp(alog_ref[0])
    lane128 = lax.broadcasted_iota(jnp.int32, (1, LANES), 1)
    fwd_lane = lane128 < H_C
    tri_f = tri_ref[0]
    tri_b = tri_ref[1]

    static_prep = nc <= 2

    def prep_body(c, carry):
        base = c * CHUNK if static_prep else pl.multiple_of(c * CHUNK, CHUNK)
        accs = []
        for g0 in range(0, CONV_DIM, CONV_COLS):
            a = cb[:, g0:g0 + CONV_COLS]
            if not static_prep:
                win = xpad[pl.ds(base, CHUNK + 2 * halo), g0:g0 + CONV_COLS]
            for k in range(CONV_K):
                off = halo - CONV_K // 2 + k
                if static_prep:
                    tap = xpad[base + off:base + off + CHUNK, g0:g0 + CONV_COLS]
                else:
                    tap = win[off:off + CHUNK, :]
                a = a + tap * cw[k:k + 1, g0:g0 + CONV_COLS]
            accs.append(a)
        acc = jnp.concatenate(accs, axis=-1)
        act = acc * _sigmoid(acc)
        xc[pl.ds(base, CHUNK), :] = act
        bt_s[pl.ds(base, CHUNK), :] = act[:, D_INNER:D_INNER + LANES].T
        dtr = dt_ref[0, pl.ds(base, CHUNK), :] + dtb
        dtc = jnp.maximum(dtr, 0.0) + jnp.log(1.0 + jnp.exp(-jnp.abs(dtr)))
        hi, mid, lo = _split3(dtc * a_neg)
        cum_f = _mm(tri_f, hi) + _mm(tri_f, mid) + _mm(tri_f, lo)
        cum_b = _mm(tri_b, hi) + _mm(tri_b, mid) + _mm(tri_b, lo)
        cum = jnp.where(fwd_lane, cum_f, cum_b)
        last = jnp.where(fwd_lane, cum[CHUNK - 1:CHUNK, :], cum[0:1, :])
        cum_s[pl.ds(base, CHUNK), :] = cum
        rows = pl.ds(c * nd, nd) if static_prep else pl.ds(pl.multiple_of(c * nd, nd), nd)
        cumt_s[rows, :] = cum.T[0:nd, :]
        dtt_s[rows, :] = dtc.T[0:nd, :]
        wt_s[rows, :] = (jnp.exp(last - cum) * dtc).T[0:nd, :]
        el_rows = pl.ds(c * 8, 8) if static_prep else pl.ds(pl.multiple_of(c * 8, 8), 8)
        el_s[el_rows, :] = jnp.broadcast_to(jnp.exp(last), (8, LANES))
        return carry

    if static_prep:
        for c_static in range(nc):
            prep_body(c_static, 0)
    else:
        lax.fori_loop(0, nc, prep_body, 0, unroll=2)

    row_i = lax.broadcasted_iota(jnp.int32, (CHUNK, CHUNK), 0)
    col_j = lax.broadcasted_iota(jnp.int32, (CHUNK, CHUNK), 1)
    blockmask = (lax.broadcasted_iota(jnp.int32, (2 * N_C, D_INNER), 0) // N_C
                 == lax.broadcasted_iota(jnp.int32, (2 * N_C, D_INNER), 1) // (D_INNER // G_C))
    dexp = dexp_ref[0]
    gnorm = g_ref[0]

    for d in range(2):
        if has_h0:
            s_scr[...] = s0_ref[0, 0, d]
        else:
            s_scr[...] = jnp.zeros((2 * N_C, D_INNER), F32)
        causal = (row_i >= col_j) if d == 0 else (col_j >= row_i)

        def per_head_lanes(cols, d=d):
            parts = []
            for pair in range(H_C // 2):
                i0 = d * H_C + 2 * pair
                parts.append(jnp.where(lane128 < P_C, cols[:, i0:i0 + 1], cols[:, i0 + 1:i0 + 2]))
            return jnp.concatenate(parts, axis=-1)

        def chunk_body(step, carry, d=d, causal=causal, per_head_lanes=per_head_lanes):
            c = step if d == 0 else nc - 1 - step
            base = pl.multiple_of(c * CHUNK, CHUNK)
            rows = pl.ds(pl.multiple_of(c * nd, nd), nd)
            xs = xc[pl.ds(base, CHUNK), 0:D_INNER]
            c_c = xc[pl.ds(base, CHUNK), D_INNER + LANES:D_INNER + 2 * LANES]
            cum = cum_s[pl.ds(base, CHUNK), :]
            cum_t = cumt_s[rows, :]
            dt_t = dtt_s[rows, :]
            w_t = wt_s[rows, :]
            b_t = bt_s[pl.ds(base, CHUNK), :]
            cb16 = c_c.astype(BF16)
            bt16 = b_t.astype(BF16)
            cbs = [_mm(jnp.where(lane128 // N_C == g, cb16, jnp.zeros_like(cb16)), bt16)
                   for g in range(G_C)]
            sv = s_scr[...]

            def blockdiag(v):
                return jnp.concatenate([jnp.where(lane128 < P_C, v, 0.0),
                                        jnp.where(lane128 < P_C, 0.0, v)], axis=0).astype(BF16)

            ys, upds = [], []
            for pair in range(H_C // 2):
                xbd = blockdiag(xs[:, pair * LANES:(pair + 1) * LANES])
                sbd = blockdiag(sv[:, pair * LANES:(pair + 1) * LANES])
                ms, es, ws = [], [], []
                for hh in (2 * pair, 2 * pair + 1):
                    idx = d * H_C + hh
                    col = jnp.broadcast_to(cum[:, idx:idx + 1], (CHUNK, CHUNK))
                    dec = jnp.where(causal, jnp.exp(col - cum_t[idx:idx + 1, :]), 0.0)
                    ms.append((cbs[hh // (H_C // G_C)] * dec * dt_t[idx:idx + 1, :]).astype(BF16))
                    es.append((jnp.exp(col) * c_c).astype(BF16))
                    ws.append((b_t * w_t[idx:idx + 1, :]).astype(BF16))
                ys.append(_mm(jnp.concatenate(ms + es, axis=-1), jnp.concatenate([xbd, sbd], axis=0)))
                upds.append(_mm(jnp.concatenate(ws, axis=-1), xbd))
            y = jnp.concatenate(ys, axis=-1)
            cd = per_head_lanes(el_s[pl.ds(pl.multiple_of(c * 8, 8), 1), :])
            s_scr[...] = sv * cd + jnp.where(blockmask, jnp.concatenate(upds, axis=-1), 0.0)
            if d == 0:
                yacc[pl.ds(base, CHUNK), :] = y
            else:
                zc = z_ref[0, pl.ds(base, CHUNK), :]
                yt = (yacc[pl.ds(base, CHUNK), :] + y + dexp * xs) * (zc * _sigmoid(zc))
                o_ref[0, pl.ds(base, CHUNK), :] = (_rms(yt) * gnorm).astype(BF16)
            return carry

        lax.fori_loop(0, nc, chunk_body, 0, unroll=2)
        if not has_h0:
            st = s_scr[...].T
            st_ref[0, d] = (st + pltpu.roll(st, N_C, 1))[:, 0:N_C]


def _ssd_call(z, xbc, dt, s0, P, l):
    B, L = z.shape[0], z.shape[1]
    has_h0 = s0 is not None
    bmap = lambda b: (b, 0, 0)
    lay = lambda b: (l, 0, 0)
    in_specs = [pl.BlockSpec((1, L, D_INNER), bmap), pl.BlockSpec((1, L, CONV_DIM), bmap),
                pl.BlockSpec((1, L, LANES), bmap)]
    args = [z, xbc, dt]
    if has_h0:
        in_specs.append(pl.BlockSpec((1, 1, 2, 2 * N_C, D_INNER), lambda b: (b, l, 0, 0, 0)))
        args.append(s0)
    in_specs += [pl.BlockSpec((1, 8, CONV_DIM), lay), pl.BlockSpec((1, 1, CONV_DIM), lay),
                 pl.BlockSpec((1, 1, LANES), lay), pl.BlockSpec((1, 1, LANES), lay),
                 pl.BlockSpec((1, 1, D_INNER), lay), pl.BlockSpec((1, 1, D_INNER), lay),
                 pl.BlockSpec((2, CHUNK, CHUNK), lambda b: (0, 0, 0))]
    args += [P["cw"], P["cb"], P["alog"], P["dtb"], P["dexp"], P["gssm"], P["tri"]]
    out_specs = [pl.BlockSpec((1, L, D_INNER), bmap)]
    out_shape = [jax.ShapeDtypeStruct((B, L, D_INNER), BF16)]
    if not has_h0:
        out_specs.append(pl.BlockSpec((1, 2, D_INNER, N_C), lambda b: (b, 0, 0, 0)))
        out_shape.append(jax.ShapeDtypeStruct((B, 2, D_INNER, N_C), F32))
    return pl.pallas_call(
        functools.partial(_ssd_kernel, L, has_h0),
        grid=(B,),
        in_specs=in_specs, out_specs=out_specs, out_shape=out_shape,
        scratch_shapes=[pltpu.VMEM((L + 16, CONV_DIM), F32), pltpu.VMEM((L, CONV_DIM), F32),
                        pltpu.VMEM((L, D_INNER), F32), pltpu.VMEM((L, LANES), F32),
                        pltpu.VMEM((L, LANES), F32)]
        + [pltpu.VMEM((L // CHUNK * 2 * H_C, LANES), F32)] * 3
        + [pltpu.VMEM((L // CHUNK * 8, LANES), F32), pltpu.VMEM((2 * N_C, D_INNER), F32)],
        compiler_params=pltpu.CompilerParams(dimension_semantics=("arbitrary",),
                                             vmem_limit_bytes=V7X_VMEM_LIMIT),
        name="ssd_lat" if has_h0 else "ssd_ctx",
    )(*args)


def _post_kernel(x_ref, oat_ref, oc_ref, mod_ref, wo_ref, g2_ref, w1_ref, w2_ref, o_ref):
    mod = mod_ref[0, 0]
    gate1 = mod[:, 2 * D_MODEL:3 * D_MODEL]
    shift2 = mod[:, 3 * D_MODEL:4 * D_MODEL]
    scale2 = mod[:, 4 * D_MODEL:5 * D_MODEL]
    gate2 = mod[:, 5 * D_MODEL:6 * D_MODEL]
    n_att = WVA + WD
    mix = _mm(oat_ref[...], wo_ref[0, 0:n_att, :]) + _mm(oc_ref[...], wo_ref[0, n_att:n_att + D_INNER, :])
    x1 = x_ref[...] + gate1 * mix
    h2 = (_rms(x1) * g2_ref[0] * (1.0 + scale2) + shift2).astype(BF16)
    ff = None
    for c in range(D_FF // FF_CHUNK):
        u = jnp.maximum(_mm(h2, w1_ref[0, :, c * FF_CHUNK:(c + 1) * FF_CHUNK]), 0.0)
        part = _mm((u * u).astype(BF16), w2_ref[0, c * FF_CHUNK:(c + 1) * FF_CHUNK, :])
        ff = part if ff is None else ff + part
    o_ref[...] = x1 + gate2 * ff


def _post_call(x2d, oat, oc, mod4, P, l, latent):
    T = x2d.shape[0]
    tm = TM_TOK
    tpb = DEC_SEQ // tm
    row = lambda i: (i, 0)
    lay = lambda i: (l, 0, 0)
    if latent:
        mod_map = lambda i: (l, i // tpb, 0, 0)
    else:
        mod_map = lambda i: (l, DEC_BATCH, 0, 0)
    single = pl.Buffered(1)
    return pl.pallas_call(
        _post_kernel,
        grid=(T // tm,),
        in_specs=[pl.BlockSpec((tm, D_MODEL), row),
                  pl.BlockSpec((tm, WVA + WD), row),
                  pl.BlockSpec((tm, D_INNER), row),
                  pl.BlockSpec((1, 1, 1, N_MOD * D_MODEL), mod_map),
                  pl.BlockSpec((1, D_MODEL, D_MODEL), lay, pipeline_mode=single),
                  pl.BlockSpec((1, 1, D_MODEL), lay),
                  pl.BlockSpec((1, D_MODEL, D_FF), lay, pipeline_mode=single),
                  pl.BlockSpec((1, D_FF, D_MODEL), lay, pipeline_mode=single)],
        out_specs=pl.BlockSpec((tm, D_MODEL), row),
        out_shape=jax.ShapeDtypeStruct((T, D_MODEL), F32),
        compiler_params=pltpu.CompilerParams(dimension_semantics=("arbitrary",),
                                             vmem_limit_bytes=V7X_VMEM_LIMIT),
        name="post_lat" if latent else "post_ctx",
    )(x2d, oat, oc, mod4, P["wout"], P["g2"], P["wff1"], P["wff2"])


def _rope_tables():
    t = np.arange(DEC_SEQ)
    half = ROPE_A // 2
    freqs = ROPE_BASE ** (-np.arange(0, half, 2, dtype=np.float64) / half)
    ang_r = (t // GRID_W)[:, None] * freqs
    ang_c = (t % GRID_W)[:, None] * freqs
    cos32 = np.concatenate([np.cos(ang_r), np.cos(ang_r), np.cos(ang_c), np.cos(ang_c)], axis=-1)
    sin32 = np.concatenate([-np.sin(ang_r), np.sin(ang_r), -np.sin(ang_c), np.sin(ang_c)], axis=-1)
    ones = np.ones((DEC_SEQ, NOPE_A))
    zeros = np.zeros((DEC_SEQ, NOPE_A))
    pad1 = np.ones((DEC_SEQ, LANES - QK_A))
    pad0 = np.zeros((DEC_SEQ, LANES - QK_A))
    cosa = np.tile(np.concatenate([ones, cos32, pad1], axis=-1), (1, H_A))
    sina = np.tile(np.concatenate([zeros, sin32, pad0], axis=-1), (1, H_A))
    cosd = np.tile(cos32, (1, WD // ROPE_A))
    sind = np.tile(sin32, (1, WD // ROPE_A))
    return tuple(jnp.asarray(a, F32) for a in (cosa, sina, cosd, sind))


def _constants():
    lane = np.arange(WD)
    bd32 = (lane[:, None] // DH_B == lane[None, :] // DH_B).astype(np.float32)
    bd64 = (lane[:, None] // (2 * DH_B) == lane[None, :] // (2 * DH_B)).astype(np.float32)
    i = np.arange(CHUNK)
    tri = np.stack([(i[None, :] <= i[:, None]), (i[None, :] >= i[:, None])]).astype(np.float32)
    return dict(bd32=jnp.asarray(bd32, BF16), bd64=jnp.asarray(bd64, BF16),
                tri=jnp.asarray(tri, BF16))


def _pad_last(a, n):
    return jnp.pad(a, [(0, 0)] * (a.ndim - 1) + [(0, n - a.shape[-1])])


def _prep_params(norm1_g, norm2_g, w_in, w_out, mla_q_norm_g, mla_kv_norm_g, w_uq, w_ukv, mla_qk_norm_q,
                 mla_qk_norm_k, diff_q_norm_g, diff_k_norm_g, diff_lq1, diff_lk1, diff_lq2, diff_lk2,
                 diff_subln_g, ssm_conv_w, ssm_conv_b, ssm_A_log, ssm_dt_bias, ssm_D, ssm_norm_g, w_ff1, w_ff2):
    P = _constants()
    P["win_t"] = jnp.swapaxes(w_in, 1, 2).astype(BF16)
    P["wuq"] = _pad_last(w_uq.reshape(DEPTH, Q_RANK, H_A, QK_A), LANES).reshape(DEPTH, Q_RANK, WA).astype(BF16)
    wkv = w_ukv.reshape(DEPTH, KV_RANK, H_A, NOPE_A + V_A)
    P["wuk"] = _pad_last(wkv[..., :NOPE_A], LANES).reshape(DEPTH, KV_RANK, WA).astype(BF16)
    P["wuv"] = wkv[..., NOPE_A:].reshape(DEPTH, KV_RANK, WVA).astype(BF16)
    P["wout"] = w_out.astype(BF16)
    P["wff1"] = w_ff1.astype(BF16)
    P["wff2"] = w_ff2.astype(BF16)
    vec = lambda a: a.reshape(DEPTH, 1, -1)
    P["g1"] = vec(norm1_g)
    P["g2"] = vec(norm2_g)
    P["gq"] = vec(mla_q_norm_g)
    P["gkv"] = vec(mla_kv_norm_g)
    P["gqa"] = vec(jnp.tile(_pad_last(mla_qk_norm_q, LANES), (1, H_A))) * (QK_A ** -0.5)
    P["gka"] = vec(jnp.tile(_pad_last(mla_qk_norm_k, LANES), (1, H_A)))
    P["gdq"] = vec(jnp.tile(diff_q_norm_g, (1, WD // DH_B))) * (DH_B ** -0.5)
    P["gdk"] = vec(jnp.tile(diff_k_norm_g, (1, WD // DH_B)))
    P["gsub"] = vec(jnp.tile(diff_subln_g, (1, H_B)))
    P["lq1"], P["lk1"], P["lq2"], P["lk2"] = vec(diff_lq1), vec(diff_lk1), vec(diff_lq2), vec(diff_lk2)
    P["cw"] = jnp.pad(ssm_conv_w, ((0, 0), (0, 8 - CONV_K), (0, 0)))
    P["cb"] = vec(ssm_conv_b)
    P["alog"] = vec(_pad_last(ssm_A_log.reshape(DEPTH, 2 * H_C), LANES))
    P["dtb"] = vec(_pad_last(ssm_dt_bias.reshape(DEPTH, 2 * H_C), LANES))
    P["dexp"] = vec(jnp.repeat(ssm_D, P_C, axis=-1))
    P["gssm"] = vec(ssm_norm_g)
    P["cosa"], P["sina"], P["cosd"], P["sind"] = _rope_tables()
    return P


def kernel(x_prompt, x_sample, cache_mla_ckv, cache_mla_krope, cache_diff_k, cache_diff_v, state_ssm, c, c_ctx, norm1_g, norm2_g, w_ada, b_ada, w_in, w_out, mla_q_norm_g, mla_kv_norm_g, w_uq, w_ukv, mla_qk_norm_q, mla_qk_norm_k, diff_q_norm_g, diff_k_norm_g, diff_lq1, diff_lk1, diff_lq2, diff_lk2, diff_subln_g, ssm_conv_w, ssm_conv_b, ssm_A_log, ssm_dt_bias, ssm_D, ssm_norm_g, w_ff1, w_ff2):
    P = _prep_params(norm1_g, norm2_g, w_in, w_out, mla_q_norm_g, mla_kv_norm_g, w_uq, w_ukv, mla_qk_norm_q,
                     mla_qk_norm_k, diff_q_norm_g, diff_k_norm_g, diff_lq1, diff_lk1, diff_lq2, diff_lk2,
                     diff_subln_g, ssm_conv_w, ssm_conv_b, ssm_A_log, ssm_dt_bias, ssm_D, ssm_norm_g,
                     w_ff1, w_ff2)

    cvec = jnp.concatenate([c, c_ctx[None, :], jnp.zeros((8 - DEC_BATCH - 1, D_MODEL), F32)], axis=0)
    mod4 = _mod_call(cvec, w_ada, b_ada).reshape(DEPTH, 8, 1, N_MOD * D_MODEL)

    kca, vca = _cache_call(cache_mla_ckv, _pad_last(cache_mla_krope, LANES), P)
    kcd = cache_diff_k.reshape(DEC_BATCH, DEPTH, PAST_LEN, WD).astype(BF16)
    vcd = cache_diff_v.reshape(DEC_BATCH, DEPTH, PAST_LEN, WD).astype(BF16)
    st = jnp.transpose(state_ssm, (0, 1, 2, 5, 3, 4)).reshape(DEC_BATCH, DEPTH, 2, N_C, D_INNER)
    half = D_INNER // G_C
    zero = jnp.zeros_like(st[..., :half])
    s0 = jnp.concatenate([jnp.concatenate([st[..., :half], zero], axis=-1),
                          jnp.concatenate([zero, st[..., half:]], axis=-1)], axis=-2)

    xp = x_prompt.reshape(BATCH * SEQ, D_MODEL)
    xs = x_sample.reshape(DEC_BATCH * DEC_SEQ, D_MODEL)
    ckv_l, krope_l, kd_l, vd_l, st_l = [], [], [], [], []
    for l in range(DEPTH):
        qa, ka, va, qd, kd, vd, z, xbc, dt, ckvn, krope, kd32, vd32 = _pre_call(xp, mod4, P, l, False)
        r3 = lambda a: a.reshape(BATCH, SEQ, a.shape[-1])
        oat = _attn_call(r3(qa), r3(ka), r3(va), r3(qd), r3(kd), r3(vd), None, P, l)
        oc, st_new = _ssd_call(r3(z), r3(xbc), r3(dt), None, P, l)
        xp = _post_call(xp, oat.reshape(BATCH * SEQ, -1), oc.reshape(BATCH * SEQ, -1), mod4, P, l, False)
        ckv_l.append(ckvn.reshape(BATCH, SEQ, KV_RANK))
        krope_l.append(krope.reshape(BATCH, SEQ, ROPE_A))
        kd_l.append(jnp.swapaxes(kd32, 1, 2).reshape(BATCH, SEQ, H_B, 2 * DH_B))
        vd_l.append(jnp.swapaxes(vd32, 1, 2).reshape(BATCH, SEQ, H_B, 2 * DH_B))
        st_l.append(st_new.reshape(BATCH, 2, H_C, P_C, N_C))
        qa, ka, va, qd, kd, vd, z, xbc, dt = _pre_call(xs, mod4, P, l, True)
        r3 = lambda a: a.reshape(DEC_BATCH, DEC_SEQ, a.shape[-1])
        oat = _attn_call(r3(qa), r3(ka), r3(va), r3(qd), r3(kd), r3(vd), (kca, vca, kcd, vcd), P, l)
        oc, = _ssd_call(r3(z), r3(xbc), r3(dt), s0, P, l)
        xs = _post_call(xs, oat.reshape(DEC_BATCH * DEC_SEQ, -1), oc.reshape(DEC_BATCH * DEC_SEQ, -1),
                        mod4, P, l, True)
    return (xp.reshape(BATCH, SEQ, D_MODEL), xs.reshape(DEC_BATCH, DEC_SEQ, D_MODEL),
            jnp.stack(ckv_l, axis=1), jnp.stack(krope_l, axis=1), jnp.stack(kd_l, axis=1),
            jnp.stack(vd_l, axis=1), jnp.stack(st_l, axis=1))
```
